```python
import math
import jax, jax.numpy as jnp
from jax import lax
import numpy as np


D_MODEL = 2048
BATCH = 2
SEQ = 4096
DEPTH = 2
DEC_BATCH = 32
DEC_SEQ = 8
PAST_LEN = 8192
PAGE_SIZE = 128

N_META = 16
M_HEADS = 4
M_HEAD_DIM = 256
M_WIDTH = M_HEADS * M_HEAD_DIM
M_CHUNK = 64
D_HEADS = 8
D_QK_DIM = 64
D_V_DIM = 2 * D_QK_DIM
D_QK_WIDTH = D_HEADS * 2 * D_QK_DIM
D_V_WIDTH = D_HEADS * D_V_DIM
Q_BLOCK = 128
FFN_DIM = ((8 * D_MODEL // 3 + 255) // 256) * 256
IN_SIZES = (M_WIDTH, M_WIDTH, M_WIDTH, M_WIDTH, 2 * M_HEADS, D_QK_WIDTH, D_QK_WIDTH, D_V_WIDTH, D_MODEL, D_MODEL)
N_IN = 4 * M_WIDTH + 2 * M_HEADS + 2 * D_QK_WIDTH + D_V_WIDTH + 2 * D_MODEL
EPS = 1e-6

kernel_name = "hybrid_mlstm_diffattn_gated_decoder_step"


def rms_norm(x, gain):
    xf = x.astype(jnp.float32)
    y = xf * lax.rsqrt(jnp.mean(xf * xf, axis=-1, keepdims=True) + EPS)
    return (y * gain.astype(jnp.float32)).astype(x.dtype)


def alibi_slopes():
    return 2.0 ** (-8.0 * jnp.arange(1, D_HEADS + 1, dtype=jnp.float32) / D_HEADS)


def lambda_init(layer):
    return 0.8 - 0.6 * math.exp(-0.3 * layer)


def project_inputs(x, norm_mix, w_in, b_if, q_gain, k_gain):
    B, T, _ = x.shape
    proj = rms_norm(x, norm_mix) @ w_in
    offsets = [int(v) for v in np.cumsum(IN_SIZES)[:-1]]
    qm, km, vm, om, gif, qd, kd, vd, gm, gd = jnp.split(proj, offsets, axis=-1)
    qm = qm.reshape(B, T, M_HEADS, M_HEAD_DIM)
    km = km.reshape(B, T, M_HEADS, M_HEAD_DIM) * (M_HEAD_DIM ** -0.5)
    vm = vm.reshape(B, T, M_HEADS, M_HEAD_DIM)
    gif = gif.astype(jnp.float32) + b_if.astype(jnp.float32)
    ig = gif[..., :M_HEADS]
    logf = jax.nn.log_sigmoid(gif[..., M_HEADS:])
    qd = rms_norm(qd.reshape(B, T, D_HEADS, 2, D_QK_DIM), q_gain)
    kd = rms_norm(kd.reshape(B, T, D_HEADS, 2, D_QK_DIM), k_gain)
    vd = vd.reshape(B, T, D_HEADS, D_V_DIM)
    return qm, km, vm, om, ig, logf, qd, kd, vd, gm, gd


def mlstm_chunk(carry, inp):
    C, n, m = carry
    q, k, v, ig, logf = inp
    L = q.shape[2]
    tri = jnp.tril(jnp.ones((L, L), dtype=bool))
    b = jnp.cumsum(logf, axis=-1)
    D = jnp.where(tri, b[..., :, None] - b[..., None, :] + ig[..., None, :], -jnp.inf)
    inter = b + m[..., None]
    m_t = jnp.maximum(inter, jnp.max(D, axis=-1))
    w = jnp.exp(D - m_t[..., None])
    s = jnp.einsum('bhtd,bhsd->bhts', q, k) * w
    a = jnp.exp(inter - m_t)
    num = jnp.einsum('bhts,bhsd->bhtd', s, v) + a[..., None] * jnp.einsum('bhvk,bhtk->bhtv', C, q)
    den = jnp.sum(s, axis=-1) + a * jnp.einsum('bhk,bhtk->bht', n, q)
    h = num / jnp.maximum(jnp.abs(den), jnp.exp(-m_t))[..., None]
    bL = b[..., -1]
    g = bL[..., None] - b + ig
    m_new = jnp.maximum(bL + m, jnp.max(g, axis=-1))
    ws = jnp.exp(g - m_new[..., None])
    decay = jnp.exp(bL + m - m_new)
    C_new = decay[..., None, None] * C + jnp.einsum('bhs,bhsv,bhsk->bhvk', ws, v, k)
    n_new = decay[..., None] * n + jnp.einsum('bhs,bhsk->bhk', ws, k)
    return (C_new, n_new, m_new), h


def to_heads_first(q, k, v, ig, logf):
    f32 = jnp.float32
    return (jnp.moveaxis(q, 1, 2).astype(f32), jnp.moveaxis(k, 1, 2).astype(f32),
            jnp.moveaxis(v, 1, 2).astype(f32), jnp.moveaxis(ig, 1, 2), jnp.moveaxis(logf, 1, 2))


def mlstm_prompt(q, k, v, ig, logf):
    q, k, v, ig, logf = to_heads_first(q, k, v, ig, logf)
    B = q.shape[0]
    state = (jnp.zeros((B, M_HEADS, M_HEAD_DIM, M_HEAD_DIM), jnp.float32),
             jnp.zeros((B, M_HEADS, M_HEAD_DIM), jnp.float32),
             jnp.zeros((B, M_HEADS), jnp.float32))
    state, h_meta = mlstm_chunk(state, tuple(a[:, :, :N_META] for a in (q, k, v, ig, logf)))
    nc = SEQ // M_CHUNK

    def chunked(a):
        a = a[:, :, N_META:]
        return jnp.moveaxis(a.reshape(B, M_HEADS, nc, M_CHUNK, *a.shape[3:]), 2, 0)

    state, h_real = lax.scan(mlstm_chunk, state, tuple(chunked(a) for a in (q, k, v, ig, logf)))
    h_real = jnp.moveaxis(h_real, 0, 2).reshape(B, M_HEADS, SEQ, M_HEAD_DIM)
    h = jnp.concatenate([h_meta, h_real], axis=2)
    return jnp.moveaxis(h, 1, 2), state


def mlstm_sample(q, k, v, ig, logf, C, n, m):
    inp = to_heads_first(q, k, v, ig, logf)
    state, h = mlstm_chunk((C.astype(jnp.float32), n.astype(jnp.float32), m.astype(jnp.float32)), inp)
    return jnp.moveaxis(h, 1, 2), state


def mlstm_out(h, om, gain):
    B, T = om.shape[:2]
    h = rms_norm(h, gain.reshape(M_HEADS, M_HEAD_DIM)).astype(om.dtype)
    return (h * jax.nn.sigmoid(om.reshape(B, T, M_HEADS, M_HEAD_DIM))).reshape(B, T, M_WIDTH)


def diff_scores(q, k, qpos, kpos, slopes):
    s = jnp.einsum('bqhcd,bkhcd->bchqk', q, k).astype(jnp.float32) * (D_QK_DIM ** -0.5)
    dist = qpos[:, None] - kpos[None, :]
    bias = -slopes[:, None, None] * dist.astype(jnp.float32)
    return jnp.where(dist >= 0, s + bias, -jnp.inf)


def diff_mix(p, v, lam):
    a = p[:, 0] - lam * p[:, 1]
    return jnp.einsum('bhqk,bkhd->bqhd', a.astype(v.dtype), v)


def diff_block(q, k, v, qpos, kpos, lam, slopes):
    p = jax.nn.softmax(diff_scores(q, k, qpos, kpos, slopes), axis=-1)
    return diff_mix(p, v, lam)


def diff_prompt(qd, kd, vd, lam, slopes):
    B, T = qd.shape[:2]
    kpos = jnp.arange(T, dtype=jnp.int32)
    meta_pos = jnp.arange(N_META, dtype=jnp.int32)
    out_meta = diff_block(qd[:, :N_META], kd[:, :N_META], vd[:, :N_META], meta_pos, meta_pos, lam, slopes)
    nb = SEQ // Q_BLOCK
    q_blocks = jnp.moveaxis(qd[:, N_META:].reshape(B, nb, Q_BLOCK, D_HEADS, 2, D_QK_DIM), 1, 0)
    qpos_blocks = (N_META + jnp.arange(SEQ, dtype=jnp.int32)).reshape(nb, Q_BLOCK)
    out = lax.map(lambda a: diff_block(a[0], kd, vd, a[1], kpos, lam, slopes), (q_blocks, qpos_blocks))
    out = jnp.moveaxis(out, 0, 1).reshape(B, SEQ, D_HEADS, D_V_DIM)
    return jnp.concatenate([out_meta, out], axis=1)


def diff_sample(qd, kd, vd, k_past, v_past, lam, slopes):
    qpos = PAST_LEN + jnp.arange(DEC_SEQ, dtype=jnp.int32)
    ppos = jnp.arange(PAST_LEN, dtype=jnp.int32)
    s = jnp.concatenate([diff_scores(qd, k_past, qpos, ppos, slopes),
                         diff_scores(qd, kd, qpos, qpos, slopes)], axis=-1)
    p = jax.nn.softmax(s, axis=-1)
    return diff_mix(p[..., :PAST_LEN], v_past, lam) + diff_mix(p[..., PAST_LEN:], vd, lam)


def diff_out(o, gain, lam_init):
    B, T = o.shape[:2]
    return (rms_norm(o, gain.reshape(D_HEADS, D_V_DIM)) * (1.0 - lam_init)).reshape(B, T, D_V_WIDTH)


def merge_and_ffn(x, hm, hd, gm, gd, w_pm, w_pd, w_out, norm_ffn, w_gu, w_down):
    z = jax.nn.sigmoid(gm) * (hm @ w_pm) + jax.nn.sigmoid(gd) * (hd @ w_pd)
    x = x + z @ w_out
    g, u = jnp.split(rms_norm(x, norm_ffn) @ w_gu, 2, axis=-1)
    return x + (jax.nn.silu(g) * u) @ w_down


def setup_inputs(seed: int = 0) -> dict:
    key = jax.random.key(seed)
    ks = jax.random.split(key, 28)
    n_pages = PAST_LEN // PAGE_SIZE
    n_phys = (DEC_BATCH * n_pages * 5) // 4

    def nrm(k, shape, scale):
        return jax.random.normal(k, shape, jnp.float32) * scale

    def gain(k, shape):
        return 1.0 + 0.02 * jax.random.normal(k, shape, jnp.float32)

    perm = jax.random.permutation(ks[0], n_phys)
    page_table = perm[: DEC_BATCH * n_pages].reshape(DEC_BATCH, n_pages).astype(jnp.int32)
    b_i = nrm(ks[11], (DEPTH, M_HEADS), 0.1)
    b_f = jnp.linspace(3.0, 6.0, M_HEADS, dtype=jnp.float32)[None, :] + nrm(ks[12], (DEPTH, M_HEADS), 0.1)
    return {
        "x_prompt": nrm(ks[1], (BATCH, SEQ, D_MODEL), 1.0),
        "x_sample": nrm(ks[2], (DEC_BATCH, DEC_SEQ, D_MODEL), 1.0),
        "cache_k": nrm(ks[3], (DEPTH, n_phys, PAGE_SIZE, D_HEADS, 2 * D_QK_DIM), 1.0),
        "cache_v": nrm(ks[4], (DEPTH, n_phys, PAGE_SIZE, D_HEADS, D_V_DIM), 1.0),
        "state_C": nrm(ks[5], (DEPTH, DEC_BATCH, M_HEADS, M_HEAD_DIM, M_HEAD_DIM), 0.1),
        "state_n": nrm(ks[6], (DEPTH, DEC_BATCH, M_HEADS, M_HEAD_DIM), 0.1),
        "state_m": nrm(ks[7], (DEPTH, DEC_BATCH, M_HEADS), 0.5),
        "page_table": page_table,
        "meta_tokens": nrm(ks[8], (N_META, D_MODEL), 1.0),
        "norm_mix": gain(ks[9], (DEPTH, D_MODEL)),
        "w_in": nrm(ks[10], (DEPTH, D_MODEL, N_IN), D_MODEL ** -0.5),
        "b_if": jnp.concatenate([b_i, b_f], axis=-1),
        "q_gain": gain(ks[13], (DEPTH, D_QK_DIM)),
        "k_gain": gain(ks[14], (DEPTH, D_QK_DIM)),
        "lambda_q1": nrm(ks[15], (DEPTH, D_QK_DIM), 0.1),
        "lambda_k1": nrm(ks[16], (DEPTH, D_QK_DIM), 0.1),
        "lambda_q2": nrm(ks[17], (DEPTH, D_QK_DIM), 0.1),
        "lambda_k2": nrm(ks[18], (DEPTH, D_QK_DIM), 0.1),
        "mlstm_norm": gain(ks[19], (DEPTH, M_WIDTH)),
        "diff_norm": gain(ks[20], (DEPTH, D_V_WIDTH)),
        "w_pm": nrm(ks[21], (DEPTH, M_WIDTH, D_MODEL), M_WIDTH ** -0.5),
        "w_pd": nrm(ks[22], (DEPTH, D_V_WIDTH, D_MODEL), D_V_WIDTH ** -0.5),
        "w_out": nrm(ks[23], (DEPTH, D_MODEL, D_MODEL), D_MODEL ** -0.5),
        "norm_ffn": gain(ks[24], (DEPTH, D_MODEL)),
        "w_gu": nrm(ks[25], (DEPTH, D_MODEL, 2 * FFN_DIM), D_MODEL ** -0.5),
        "w_down": nrm(ks[26], (DEPTH, FFN_DIM, D_MODEL), FFN_DIM ** -0.5),
    }


def reference(x_prompt, x_sample, cache_k, cache_v, state_C, state_n, state_m, page_table,
              meta_tokens, norm_mix, w_in, b_if, q_gain, k_gain, lambda_q1, lambda_k1, lambda_q2, lambda_k2,
              mlstm_norm, diff_norm, w_pm, w_pd, w_out, norm_ffn, w_gu, w_down):
    slopes = alibi_slopes()
    meta = jnp.broadcast_to(meta_tokens.astype(x_prompt.dtype)[None], (x_prompt.shape[0], N_META, D_MODEL))
    xp = jnp.concatenate([meta, x_prompt], axis=1)
    xs = x_sample
    Bp, Tp = xp.shape[:2]
    Bs = xs.shape[0]
    kp_rows, vp_rows, Cp, np_, mp = [], [], [], [], []
    ks_rows, vs_rows, Cs, ns_, ms = [], [], [], [], []
    for l in range(DEPTH):
        lam_init = lambda_init(l)
        lam = (jnp.exp(jnp.sum(lambda_q1[l].astype(jnp.float32) * lambda_k1[l].astype(jnp.float32)))
               - jnp.exp(jnp.sum(lambda_q2[l].astype(jnp.float32) * lambda_k2[l].astype(jnp.float32)))
               + lam_init)
        qm, km, vm, om, ig, logf, qd, kd, vd, gm, gd = project_inputs(xp, norm_mix[l], w_in[l], b_if[l], q_gain[l], k_gain[l])
        hm, (C_p, n_p, m_p) = mlstm_prompt(qm, km, vm, ig, logf)
        hd = diff_prompt(qd, kd, vd, lam, slopes)
        xp = merge_and_ffn(xp, mlstm_out(hm, om, mlstm_norm[l]), diff_out(hd, diff_norm[l], lam_init), gm, gd,
                           w_pm[l], w_pd[l], w_out[l], norm_ffn[l], w_gu[l], w_down[l])
        kp_rows.append(kd.reshape(Bp, Tp, D_HEADS, 2 * D_QK_DIM))
        vp_rows.append(vd)
        Cp.append(C_p)
        np_.append(n_p)
        mp.append(m_p)
        qm, km, vm, om, ig, logf, qd, kd, vd, gm, gd = project_inputs(xs, norm_mix[l], w_in[l], b_if[l], q_gain[l], k_gain[l])
        hm, (C_s, n_s, m_s) = mlstm_sample(qm, km, vm, ig, logf, state_C[l], state_n[l], state_m[l])
        k_past = cache_k[l][page_table].reshape(Bs, PAST_LEN, D_HEADS, 2, D_QK_DIM).astype(kd.dtype)
        v_past = cache_v[l][page_table].reshape(Bs, PAST_LEN, D_HEADS, D_V_DIM).astype(vd.dtype)
        hd = diff_sample(qd, kd, vd, k_past, v_past, lam, slopes)
        xs = merge_and_ffn(xs, mlstm_out(hm, om, mlstm_norm[l]), diff_out(hd, diff_norm[l], lam_init), gm, gd,
                           w_pm[l], w_pd[l], w_out[l], norm_ffn[l], w_gu[l], w_down[l])
        ks_rows.append(kd.reshape(Bs, DEC_SEQ, D_HEADS, 2 * D_QK_DIM))
        vs_rows.append(vd)
        Cs.append(C_s)
        ns_.append(n_s)
        ms.append(m_s)
    y_prompt = xp[:, N_META:]
    return (y_prompt, xs,
            jnp.stack(kp_rows), jnp.stack(vp_rows), jnp.stack(Cp), jnp.stack(np_), jnp.stack(mp),
            jnp.stack(ks_rows), jnp.stack(vs_rows), jnp.stack(Cs), jnp.stack(ns_), jnp.stack(ms))
```

```python
import functools
import math

import jax
import jax.numpy as jnp
from jax import lax
from jax.experimental import pallas as pl
from jax.experimental.pallas import tpu as pltpu

D_MODEL = 2048
BATCH = 2
SEQ = 4096
DEPTH = 2
DEC_BATCH = 32
DEC_SEQ = 8
PAST_LEN = 8192
PAGE_SIZE = 128
N_META = 16
M_HEADS = 4
M_HEAD_DIM = 256
M_WIDTH = M_HEADS * M_HEAD_DIM
D_HEADS = 8
D_QK_DIM = 64
D_V_DIM = 2 * D_QK_DIM
D_QK_WIDTH = D_HEADS * 2 * D_QK_DIM
D_V_WIDTH = D_HEADS * D_V_DIM
EPS = 1e-6

LANES = 128
SUBLANES = 8
ROW_BLOCK = 256
SAMPLE_CHUNK = 128
PAGES_PER_STEP = 8
VMEM_LIMIT = 56 * 1024 * 1024

F32 = jnp.float32
BF16 = jnp.bfloat16
NEG_INF = float("-inf")


def _ffn_dim():
    return ((8 * D_MODEL // 3 + 255) // 256) * 256


def _rows():
    n_prompt = BATCH * SEQ
    n_sample = DEC_BATCH * DEC_SEQ
    assert n_prompt % ROW_BLOCK == 0 and n_sample % ROW_BLOCK == 0 and N_META <= ROW_BLOCK
    return n_prompt, n_sample, n_prompt + n_sample, n_prompt + n_sample + ROW_BLOCK


def _row_tile(r):
    return 512 if r % 512 == 0 else ROW_BLOCK


def _col_tile(n):
    for t in (1024, 512, 256, 128):
        if n % t == 0:
            return t
    raise ValueError(n)


def _params(*sem):
    return pltpu.CompilerParams(dimension_semantics=sem, vmem_limit_bytes=VMEM_LIMIT)


def _dot(a, b):
    return jnp.dot(a, b, preferred_element_type=F32)


def _dot_nt(a, b):
    return lax.dot_general(a, b, (((1,), (1,)), ((), ())), preferred_element_type=F32)


def _dot_tn(a, b):
    return lax.dot_general(a, b, (((0,), (0,)), ((), ())), preferred_element_type=F32)


def _sigmoid(x):
    return 1.0 / (1.0 + jnp.exp(-x))


def _rms(x, gain):
    ms = jnp.mean(x * x, axis=-1, keepdims=True)
    return x * lax.rsqrt(ms + EPS) * gain


def _norm_matmul_kernel(x_ref, g_ref, w_ref, o_ref, xn_ref):
    @pl.when(pl.program_id(1) == 0)
    def _():
        xn_ref[...] = _rms(x_ref[...], g_ref[...]).astype(BF16)

    o_ref[...] = _dot(xn_ref[...], w_ref[...]).astype(o_ref.dtype)


def _norm_matmul(x, gain, w, out_dtype, name):
    r, d = x.shape
    n = w.shape[1]
    tm, tn = _row_tile(r), _col_tile(n)
    return pl.pallas_call(
        _norm_matmul_kernel,
        grid=(r // tm, n // tn),
        in_specs=[pl.BlockSpec((tm, d), lambda i, j: (i, 0)),
                  pl.BlockSpec((1, d), lambda i, j: (0, 0)),
                  pl.BlockSpec((d, tn), lambda i, j: (0, j))],
        out_specs=pl.BlockSpec((tm, tn), lambda i, j: (i, j)),
        out_shape=jax.ShapeDtypeStruct((r, n), out_dtype),
        scratch_shapes=[pltpu.VMEM((tm, d), BF16)],
        compiler_params=_params("parallel", "arbitrary"),
        name=name,
    )(x, gain.reshape(1, d), w)


def _norm_swiglu_kernel(x_ref, g_ref, wg_ref, wu_ref, o_ref, xn_ref):
    @pl.when(pl.program_id(1) == 0)
    def _():
        xn_ref[...] = _rms(x_ref[...], g_ref[...]).astype(BF16)

    xn = xn_ref[...]
    g = _dot(xn, wg_ref[...])
    u = _dot(xn, wu_ref[...])
    o_ref[...] = (g * _sigmoid(g) * u).astype(o_ref.dtype)


def _norm_swiglu(x, gain, w_gu):
    r, d = x.shape
    f = w_gu.shape[1] // 2
    tm, tn = _row_tile(r), _col_tile(f)
    nj = f // tn
    return pl.pallas_call(
        _norm_swiglu_kernel,
        grid=(r // tm, nj),
        in_specs=[pl.BlockSpec((tm, d), lambda i, j: (i, 0)),
                  pl.BlockSpec((1, d), lambda i, j: (0, 0)),
                  pl.BlockSpec((d, tn), lambda i, j: (0, j)),
                  pl.BlockSpec((d, tn), lambda i, j: (0, j + nj))],
        out_specs=pl.BlockSpec((tm, tn), lambda i, j: (i, j)),
        out_shape=jax.ShapeDtypeStruct((r, f), BF16),
        scratch_shapes=[pltpu.VMEM((tm, d), BF16)],
        compiler_params=_params("parallel", "arbitrary"),
        name="ffn_up",
    )(x, gain.reshape(1, d), w_gu, w_gu)


def _matmul_res_kernel(a_ref, w_ref, r_ref, o_ref):
    o_ref[...] = r_ref[...] + _dot(a_ref[...], w_ref[...])


def _matmul_res(a, w, res, name):
    r, k = a.shape
    n = w.shape[1]
    tm, tn = _row_tile(r), _col_tile(n)
    return pl.pallas_call(
        _matmul_res_kernel,
        grid=(r // tm, n // tn),
        in_specs=[pl.BlockSpec((tm, k), lambda i, j: (i, 0)),
                  pl.BlockSpec((k, tn), lambda i, j: (0, j)),
                  pl.BlockSpec((tm, tn), lambda i, j: (i, j))],
        out_specs=pl.BlockSpec((tm, tn), lambda i, j: (i, j)),
        out_shape=jax.ShapeDtypeStruct((r, n), F32),
        compiler_params=_params("parallel", "arbitrary"),
        name=name,
    )(a, w, res)


def _merge_kernel(hm_ref, hd_ref, wpm_ref, wpd_ref, gm_ref, gd_ref, o_ref):
    zm = _dot(hm_ref[...].astype(BF16), wpm_ref[...])
    zd = _dot(hd_ref[...].astype(BF16), wpd_ref[...])
    o_ref[...] = (_sigmoid(gm_ref[...]) * zm + _sigmoid(gd_ref[...]) * zd).astype(o_ref.dtype)


def _merge(hm, hd, w_pm, w_pd, gates):
    r = hm.shape[0]
    n = w_pm.shape[1]
    tm, tn = _row_tile(r), _col_tile(math.gcd(n, M_WIDTH))
    gm0, gd0 = M_WIDTH // tn, (M_WIDTH + n) // tn
    return pl.pallas_call(
        _merge_kernel,
        grid=(r // tm, n // tn),
        in_specs=[pl.BlockSpec((tm, M_WIDTH), lambda i, j: (i, 0)),
                  pl.BlockSpec((tm, D_V_WIDTH), lambda i, j: (i, 0)),
                  pl.BlockSpec((M_WIDTH, tn), lambda i, j: (0, j)),
                  pl.BlockSpec((D_V_WIDTH, tn), lambda i, j: (0, j)),
                  pl.BlockSpec((tm, tn), lambda i, j: (i, j + gm0)),
                  pl.BlockSpec((tm, tn), lambda i, j: (i, j + gd0))],
        out_specs=pl.BlockSpec((tm, tn), lambda i, j: (i, j)),
        out_shape=jax.ShapeDtypeStruct((r, n), BF16),
        compiler_params=_params("parallel", "arbitrary"),
        name="merge",
    )(hm, hd, w_pm, w_pd, gates, gates)


def _half_norm(x, gain):
    lo = lax.broadcasted_iota(jnp.int32, x.shape, 1) < D_QK_DIM
    x2 = x * x
    s_lo = jnp.sum(jnp.where(lo, x2, 0.0), axis=1, keepdims=True)
    s_hi = jnp.sum(jnp.where(lo, 0.0, x2), axis=1, keepdims=True)
    ms = jnp.where(lo, s_lo, s_hi) * (1.0 / D_QK_DIM)
    return x * lax.rsqrt(ms + EPS) * gain


def _qk_norm_kernel(q_ref, k_ref, v_ref, qg_ref, kg_ref, qo_ref, kf_ref, kb_ref, vb_ref):
    for h in range(D_HEADS):
        sl = slice(h * LANES, (h + 1) * LANES)
        qo_ref[:, sl] = _half_norm(q_ref[:, sl], qg_ref[...]) * (D_QK_DIM ** -0.5)
        kn = _half_norm(k_ref[:, sl], kg_ref[...])
        kf_ref[:, sl] = kn
        kb_ref[:, sl] = kn.astype(BF16)
    vb_ref[...] = v_ref[...].astype(BF16)


def _qk_norm(qkv, q_gain, k_gain):
    r = qkv.shape[0]
    w = D_QK_WIDTH
    tm = _row_tile(r)
    spec = lambda c: pl.BlockSpec((tm, w), lambda i: (i, c))
    gspec = pl.BlockSpec((1, LANES), lambda i: (0, 0))
    return pl.pallas_call(
        _qk_norm_kernel,
        grid=(r // tm,),
        in_specs=[spec(0), spec(1), spec(2), gspec, gspec],
        out_specs=[spec(0), spec(0), spec(0), spec(0)],
        out_shape=[jax.ShapeDtypeStruct((r, w), F32), jax.ShapeDtypeStruct((r, w), F32),
                   jax.ShapeDtypeStruct((r, w), BF16), jax.ShapeDtypeStruct((r, w), BF16)],
        compiler_params=_params("parallel"),
        name="qk_norm",
    )(qkv, qkv, qkv, jnp.tile(q_gain, 2).reshape(1, LANES), jnp.tile(k_gain, 2).reshape(1, LANES))


def _split3(x):
    hi = x.astype(BF16)
    r1 = x - hi.astype(F32)
    mid = r1.astype(BF16)
    lo = (r1 - mid.astype(F32)).astype(BF16)
    return hi, mid, lo


def _log_sigmoid(x):
    return jnp.minimum(x, 0.0) - jnp.log1p(jnp.exp(-jnp.abs(x)))


def _mlstm_gates(graw_col, graw_row, bias_col, bias_row, n_valid):
    l = graw_col.shape[0]
    g_col = graw_col + bias_col
    g_row = graw_row + bias_row
    is_f_col = lax.broadcasted_iota(jnp.int32, g_col.shape, 1) >= M_HEADS
    is_f_row = lax.broadcasted_iota(jnp.int32, g_row.shape, 0) >= M_HEADS
    ok_col = lax.broadcasted_iota(jnp.int32, g_col.shape, 0) < n_valid
    ok_row = lax.broadcasted_iota(jnp.int32, g_row.shape, 1) < n_valid
    lf_col = jnp.where(is_f_col & ok_col, _log_sigmoid(g_col), 0.0)
    lf_row = jnp.where(is_f_row & ok_row, _log_sigmoid(g_row), 0.0)
    ig_col = jnp.where(ok_col, g_col, NEG_INF)
    ig_row = jnp.where(ok_row, g_row, NEG_INF)
    rr = lax.broadcasted_iota(jnp.int32, (l, l), 0)
    cc = lax.broadcasted_iota(jnp.int32, (l, l), 1)
    tri_l = (cc <= rr).astype(BF16)
    tri_u = (rr <= cc).astype(BF16)
    b_col = sum(_dot(tri_l, p) for p in _split3(lf_col))
    b_row = sum(_dot(p, tri_u) for p in _split3(lf_row))
    return ig_col, b_col, ig_row, b_row, cc <= rr


def _mlstm_chunk(q, k, v, ig_col, b_col, ig_row, b_row, causal, c_st, n_st, m_st):
    l = q.shape[0]
    d = b_col - b_row + ig_row
    d = jnp.where(causal, d, NEG_INF)
    inter = b_col + m_st
    m_t = jnp.maximum(inter, jnp.max(d, axis=1, keepdims=True))
    w = jnp.exp(d - m_t)
    s = _dot_nt(q, k) * w
    a = jnp.exp(inter - m_t)
    num = _dot(s.astype(BF16), v) + a * _dot_nt(q, c_st.astype(BF16))
    den = jnp.sum(s, axis=1, keepdims=True) + a * jnp.sum(q.astype(F32) * n_st, axis=1, keepdims=True)
    h = num * (1.0 / jnp.maximum(jnp.abs(den), jnp.exp(-m_t)))

    b_last = b_col[l - 1:l, :]
    g_col = b_last - b_col + ig_col
    g_row = b_last - b_row + ig_row
    m_new = jnp.maximum(b_last + m_st, jnp.max(g_row, axis=1, keepdims=True))
    ws_col = jnp.exp(g_col - m_new)
    ws_row = jnp.exp(g_row - m_new)
    decay = jnp.exp(b_last + m_st - m_new)
    wv = (v.astype(F32) * ws_col).astype(BF16)
    c_new = decay * c_st + _dot_tn(wv, k)
    n_new = decay * n_st + _dot(ws_row.astype(BF16), k)
    return h, c_new, n_new, m_new


def _mlstm_heads(q_of, k_of, v_of, om_of, gates, gain_ref, c_ref, n_ref, m_ref, h_out):
    ig_col, b_col, ig_row, b_row, causal = gates
    for h in range(M_HEADS):
        f = M_HEADS + h
        hs = slice(h * M_HEAD_DIM, (h + 1) * M_HEAD_DIM)
        out, c_new, n_new, m_new = _mlstm_chunk(
            q_of(hs), k_of(hs), v_of(hs),
            ig_col[:, h:h + 1], b_col[:, f:f + 1], ig_row[h:h + 1, :], b_row[f:f + 1, :], causal,
            c_ref[0, h], n_ref[0, h:h + 1, :], m_ref[0, h:h + 1, 0:1])
        c_ref[0, h] = c_new
        n_ref[0, h:h + 1, :] = n_new
        m_ref[0, h:h + 1, :] = jnp.broadcast_to(m_new, (1, LANES))
        h_out(hs, _rms(out, gain_ref[:, hs]) * _sigmoid(om_of(hs)))


def _mlstm_prompt_kernel(q_ref, k_ref, v_ref, om_ref, gc_ref, gr_ref, bc_ref, br_ref, gain_ref,
                         h_ref, c_ref, n_ref, m_ref):
    c = pl.program_id(1)

    @pl.when(c == 0)
    def _():
        c_ref[...] = jnp.zeros_like(c_ref)
        n_ref[...] = jnp.zeros_like(n_ref)
        m_ref[...] = jnp.zeros_like(m_ref)

    n_valid = jnp.where(c == 0, N_META, ROW_BLOCK)
    gates = _mlstm_gates(gc_ref[...], gr_ref[...], bc_ref[...], br_ref[...], n_valid)

    def h_out(hs, val):
        h_ref[:, hs] = val

    _mlstm_heads(lambda hs: q_ref[:, hs].astype(BF16), lambda hs: k_ref[:, hs].astype(BF16),
                 lambda hs: v_ref[:, hs].astype(BF16), lambda hs: om_ref[:, hs],
                 gates, gain_ref, c_ref, n_ref, m_ref, h_out)


def _mlstm_prompt(qkv, gates, g_col, g_row, b_if, gain):
    r = qkv.shape[0]
    n_prompt, _, meta_row, _ = _rows()
    nc = SEQ // ROW_BLOCK
    meta_blk = meta_row // ROW_BLOCK

    def rb(b, c):
        return jnp.where(c == 0, meta_blk, b * nc + c - 1)

    def rb_out(b, c):
        return jnp.where((c == 0) & (b > 0), b * nc, rb(b, c))

    spec = lambda col: pl.BlockSpec((ROW_BLOCK, M_WIDTH), lambda b, c: (rb(b, c), col))
    full = lambda shape: pl.BlockSpec(shape, lambda b, c: (0,) * len(shape))
    return pl.pallas_call(
        _mlstm_prompt_kernel,
        grid=(BATCH, nc + 1),
        in_specs=[spec(0), spec(1), spec(2), spec(0),
                  pl.BlockSpec((ROW_BLOCK, 2 * M_HEADS), lambda b, c: (rb(b, c), 0)),
                  pl.BlockSpec((2 * M_HEADS, ROW_BLOCK), lambda b, c: (0, rb(b, c))),
                  full((1, 2 * M_HEADS)), full((2 * M_HEADS, 1)), full((1, M_WIDTH))],
        out_specs=[pl.BlockSpec((ROW_BLOCK, M_WIDTH), lambda b, c: (rb_out(b, c), 0)),
                   pl.BlockSpec((1, M_HEADS, M_HEAD_DIM, M_HEAD_DIM), lambda b, c: (b, 0, 0, 0)),
                   pl.BlockSpec((1, M_HEADS, M_HEAD_DIM), lambda b, c: (b, 0, 0)),
                   pl.BlockSpec((1, M_HEADS, LANES), lambda b, c: (b, 0, 0))],
        out_shape=[jax.ShapeDtypeStruct((r, M_WIDTH), F32),
                   jax.ShapeDtypeStruct((BATCH, M_HEADS, M_HEAD_DIM, M_HEAD_DIM), F32),
                   jax.ShapeDtypeStruct((BATCH, M_HEADS, M_HEAD_DIM), F32),
                   jax.ShapeDtypeStruct((BATCH, M_HEADS, LANES), F32)],
        compiler_params=_params("arbitrary", "arbitrary"),
        name="mlstm_prompt",
    )(qkv, qkv, qkv, gates, g_col, g_row, b_if.reshape(1, -1), b_if.reshape(-1, 1), gain.reshape(1, -1))


def _mlstm_sample_kernel(q_ref, k_ref, v_ref, om_ref, gc_ref, gr_ref, bc_ref, br_ref, gain_ref,
                         c_in, n_in, m_in, hm_in, h_ref, c_ref, n_ref, m_ref, qp, kp, vp):
    del hm_in
    c_ref[...] = c_in[...]
    n_ref[...] = n_in[...]
    m_ref[...] = m_in[...]
    for src, dst in ((q_ref, qp), (k_ref, kp), (v_ref, vp)):
        dst[...] = jnp.zeros_like(dst)
        dst[0:DEC_SEQ, :] = src[...]
    gates = _mlstm_gates(gc_ref[0], gr_ref[0], bc_ref[...], br_ref[...], DEC_SEQ)
    om = om_ref[...]

    def h_out(hs, val):
        h_ref[:, hs] = val[0:DEC_SEQ, :]

    def om_of(hs):
        return jnp.concatenate([om[:, hs], jnp.zeros((SAMPLE_CHUNK - DEC_SEQ, M_HEAD_DIM), F32)], axis=0)

    _mlstm_heads(lambda hs: qp[:, hs].astype(BF16), lambda hs: kp[:, hs].astype(BF16),
                 lambda hs: vp[:, hs].astype(BF16), om_of,
                 gates, gain_ref, c_ref, n_ref, m_ref, h_out)


def _mlstm_sample(qkv, gates, g_col, g_row, b_if, gain, st_c, st_n, st_m, hm):
    assert DEC_SEQ == SUBLANES
    n_prompt = _rows()[0]
    blk0 = n_prompt // DEC_SEQ
    spec = lambda col: pl.BlockSpec((DEC_SEQ, M_WIDTH), lambda b: (blk0 + b, col))
    full = lambda shape: pl.BlockSpec(shape, lambda b: (0,) * len(shape))
    st_spec = lambda shape: pl.BlockSpec((1,) + shape, lambda b: (b,) + (0,) * len(shape))
    st_specs = [st_spec((M_HEADS, M_HEAD_DIM, M_HEAD_DIM)), st_spec((M_HEADS, M_HEAD_DIM)),
                st_spec((M_HEADS, LANES))]
    return pl.pallas_call(
        _mlstm_sample_kernel,
        grid=(DEC_BATCH,),
        in_specs=[spec(0), spec(1), spec(2), spec(0),
                  st_spec((SAMPLE_CHUNK, 2 * M_HEADS)), st_spec((2 * M_HEADS, SAMPLE_CHUNK)),
                  full((1, 2 * M_HEADS)), full((2 * M_HEADS, 1)), full((1, M_WIDTH))] + st_specs
                 + [pl.BlockSpec(memory_space=pl.ANY)],
        out_specs=[spec(0)] + st_specs,
        out_shape=[jax.ShapeDtypeStruct(hm.shape, F32),
                   jax.ShapeDtypeStruct(st_c.shape, F32), jax.ShapeDtypeStruct(st_n.shape, F32),
                   jax.ShapeDtypeStruct((DEC_BATCH, M_HEADS, LANES), F32)],
        scratch_shapes=[pltpu.VMEM((SAMPLE_CHUNK, M_WIDTH), F32)] * 3,
        input_output_aliases={12: 0},
        compiler_params=_params("arbitrary"),
        name="mlstm_sample",
    )(qkv, qkv, qkv, gates, g_col, g_row, b_if.reshape(1, -1), b_if.reshape(-1, 1), gain.reshape(1, -1),
      st_c, st_n, jnp.broadcast_to(st_m[:, :, None], (DEC_BATCH, M_HEADS, LANES)), hm)


def _lambda(lam_ref, lam_init):
    p = lam_ref[...]
    s1 = jnp.sum(p[0:1, :] * p[1:2, :], axis=1, keepdims=True)
    s2 = jnp.sum(p[2:3, :] * p[3:4, :], axis=1, keepdims=True)
    return jnp.exp(s1) - jnp.exp(s2) + lam_init


def _softmax_step(s, v, m_ref, l_ref, acc_ref):
    m_old = m_ref[...]
    m_new = jnp.maximum(m_old, jnp.max(s, axis=1, keepdims=True))
    alpha = jnp.exp(m_old - m_new)
    p = jnp.exp(s - m_new)
    l_ref[...] = alpha * l_ref[...] + jnp.sum(p, axis=1, keepdims=True)
    acc_ref[...] = alpha * acc_ref[...] + _dot(p.astype(BF16), v)
    m_ref[...] = m_new


def _diff_prompt_kernel(slope_ref, q_ref, k_ref, v_ref, km_ref, vm_ref, lam_ref, gain_ref, o_ref,
                        m_ref, l_ref, acc_ref, *, lam_init):
    b = pl.program_id(0)
    h = pl.program_id(1)
    is_meta = pl.program_id(2) == 0
    t = pl.program_id(2) - 1
    tq = ROW_BLOCK
    slope = slope_ref[h]

    @pl.when(jnp.logical_not(is_meta & (b > 0)))
    def _():
        q = q_ref[...].astype(BF16)
        lo = lax.broadcasted_iota(jnp.int32, q.shape, 1) < D_QK_DIM
        zero = jnp.zeros_like(q)
        q2 = jnp.concatenate([jnp.where(lo, q, zero), jnp.where(lo, zero, q)], axis=0)
        q_pos0 = jnp.where(is_meta, 0, N_META + t * tq)

        m_ref[...] = jnp.full_like(m_ref, NEG_INF)
        l_ref[...] = jnp.zeros_like(l_ref)
        acc_ref[...] = jnp.zeros_like(acc_ref)

        def rel(shape, k_pos0):
            row = lax.broadcasted_iota(jnp.int32, shape, 0)
            row = jnp.where(row >= tq, row - tq, row)
            d = lax.broadcasted_iota(jnp.int32, shape, 1) + (k_pos0 - q_pos0)
            return d, d <= row

        d, ok = rel((2 * tq, N_META), 0)
        s = _dot_nt(q2, km_ref[...]) + slope * d.astype(F32)
        _softmax_step(jnp.where(ok, s, NEG_INF), vm_ref[...], m_ref, l_ref, acc_ref)

        def full_block(kb, carry):
            ks = pl.ds(pl.multiple_of(kb * tq, tq), tq)
            d = lax.broadcasted_iota(jnp.int32, (1, tq), 1) + (N_META + kb * tq - q_pos0)
            s = _dot_nt(q2, k_ref[ks, :]) + slope * d.astype(F32)
            _softmax_step(s, v_ref[ks, :], m_ref, l_ref, acc_ref)
            return carry

        lax.fori_loop(0, jnp.where(is_meta, 0, t), full_block, 0)

        @pl.when(jnp.logical_not(is_meta))
        def _():
            ks = pl.ds(pl.multiple_of(t * tq, tq), tq)
            d, ok = rel((2 * tq, tq), q_pos0)
            s = _dot_nt(q2, k_ref[ks, :]) + slope * d.astype(F32)
            _softmax_step(jnp.where(ok, s, NEG_INF), v_ref[ks, :], m_ref, l_ref, acc_ref)

        lam = _lambda(lam_ref, lam_init)
        o = acc_ref[...] * (1.0 / l_ref[...])
        o = o[0:tq, :] - lam * o[tq:2 * tq, :]
        o_ref[...] = _rms(o, gain_ref[...]) * (1.0 - lam_init)


def _diff_prompt(qn, kb, vb, lam_pack, gain, slopes, lam_init):
    r = qn.shape[0]
    meta_row = _rows()[2]
    nq = SEQ // ROW_BLOCK
    meta_blk = meta_row // ROW_BLOCK
    assert meta_row % N_META == 0

    def qmap(b, h, qi, *_):
        return (jnp.where(qi == 0, jnp.where(b == 0, meta_blk, b * nq), b * nq + qi - 1), h)

    grid_spec = pltpu.PrefetchScalarGridSpec(
        num_scalar_prefetch=1,
        grid=(BATCH, D_HEADS, nq + 1),
        in_specs=[pl.BlockSpec((ROW_BLOCK, LANES), qmap),
                  pl.BlockSpec((SEQ, LANES), lambda b, h, qi, *_: (b, h)),
                  pl.BlockSpec((SEQ, LANES), lambda b, h, qi, *_: (b, h)),
                  pl.BlockSpec((N_META, LANES), lambda b, h, qi, *_: (meta_row // N_META, h)),
                  pl.BlockSpec((N_META, LANES), lambda b, h, qi, *_: (meta_row // N_META, h)),
                  pl.BlockSpec((SUBLANES, LANES), lambda b, h, qi, *_: (0, 0)),
                  pl.BlockSpec((1, LANES), lambda b, h, qi, *_: (0, h))],
        out_specs=pl.BlockSpec((ROW_BLOCK, LANES), qmap),
        scratch_shapes=[pltpu.VMEM((2 * ROW_BLOCK, 1), F32), pltpu.VMEM((2 * ROW_BLOCK, 1), F32),
                        pltpu.VMEM((2 * ROW_BLOCK, LANES), F32)])
    return pl.pallas_call(
        functools.partial(_diff_prompt_kernel, lam_init=lam_init),
        grid_spec=grid_spec,
        out_shape=jax.ShapeDtypeStruct((r, D_V_WIDTH), F32),
        compiler_params=_params("arbitrary", "arbitrary", "arbitrary"),
        name="diff_prompt",
    )(slopes, qn, kb, vb, kb, vb, lam_pack, gain.reshape(1, -1))


def _diff_sample_kernel(pt_ref, *refs, lam_init):
    del pt_ref
    pp = PAGES_PER_STEP
    k_pages, v_pages = refs[0:pp], refs[pp:2 * pp]
    (q_ref, kn_ref, vn_ref, lam_ref, gain_ref, hd_in, o_ref,
     qb_ref, bias_ref, kp_ref, vp_ref, m_ref, l_ref, acc_ref) = refs[2 * pp:]
    del hd_in
    g = pl.program_id(1)
    rows = 2 * D_HEADS * DEC_SEQ
    half = D_HEADS * DEC_SEQ
    cols = PAGE_SIZE * D_HEADS

    row = lax.broadcasted_iota(jnp.int32, (rows, 1), 0)
    r_tok = row % DEC_SEQ
    r_head = (row // DEC_SEQ) % D_HEADS
    slope = jnp.exp2(-(r_head + 1).astype(F32) * (8.0 / D_HEADS))

    @pl.when(g == 0)
    def _():
        lo = lax.broadcasted_iota(jnp.int32, (DEC_SEQ, LANES), 1) < D_QK_DIM
        q = q_ref[...]
        heads = [q[:, h * LANES:(h + 1) * LANES] for h in range(D_HEADS)]
        qb_ref[...] = jnp.concatenate([jnp.where(lo, x, 0.0) for x in heads]
                                      + [jnp.where(lo, 0.0, x) for x in heads], axis=0).astype(BF16)
        col = lax.broadcasted_iota(jnp.int32, (rows, cols), 1)
        bias_ref[...] = jnp.where(col % D_HEADS == r_head, slope * (col // D_HEADS).astype(F32), NEG_INF)
        m_ref[...] = jnp.full_like(m_ref, NEG_INF)
        l_ref[...] = jnp.zeros_like(l_ref)
        acc_ref[...] = jnp.zeros_like(acc_ref)

    qb = qb_ref[...]
    for p in range(pp):
        page_pos0 = (g * pp + p) * PAGE_SIZE - PAST_LEN
        k2 = k_pages[p][...].reshape(cols, LANES).astype(BF16)
        v2 = v_pages[p][...].reshape(cols, LANES).astype(BF16)
        s = _dot_nt(qb, k2) + (bias_ref[...] + slope * page_pos0.astype(F32))
        _softmax_step(s, v2, m_ref, l_ref, acc_ref)

    @pl.when(g == pl.num_programs(1) - 1)
    def _():
        for src, dst in ((kn_ref, kp_ref), (vn_ref, vp_ref)):
            dst[...] = jnp.zeros_like(dst)
            for h in range(D_HEADS):
                dst[h * DEC_SEQ:(h + 1) * DEC_SEQ, :] = src[:, h * LANES:(h + 1) * LANES]
        c = lax.broadcasted_iota(jnp.int32, (1, LANES), 1)
        c_tok = c % DEC_SEQ
        ok = (c // DEC_SEQ == r_head) & (c_tok <= r_tok)
        s = _dot_nt(qb, kp_ref[...].astype(BF16)) + slope * c_tok.astype(F32)
        _softmax_step(jnp.where(ok, s, NEG_INF), vp_ref[...].astype(BF16), m_ref, l_ref, acc_ref)

        lam = _lambda(lam_ref, lam_init)
        o = acc_ref[...] * (1.0 / l_ref[...])
        for h in range(D_HEADS):
            cs = slice(h * LANES, (h + 1) * LANES)
            oh = o[h * DEC_SEQ:(h + 1) * DEC_SEQ, :] - lam * o[half + h * DEC_SEQ:half + (h + 1) * DEC_SEQ, :]
            o_ref[:, cs] = _rms(oh, gain_ref[:, cs]) * (1.0 - lam_init)


def _diff_sample(layer, cache_k, cache_v, page_table, qn, kf, vf, lam_pack, gain, hd, lam_init):
    assert DEC_SEQ == SUBLANES and D_HEADS == SUBLANES and D_V_DIM == LANES
    assert D_HEADS * DEC_SEQ <= LANES and PAST_LEN % (PAGE_SIZE * PAGES_PER_STEP) == 0
    n_prompt = _rows()[0]
    blk0 = n_prompt // DEC_SEQ
    pp = PAGES_PER_STEP
    width = D_HEADS * LANES
    rows = 2 * D_HEADS * DEC_SEQ

    def page_spec(p):
        return pl.BlockSpec((None, None, PAGE_SIZE, D_HEADS, LANES),
                            lambda b, g, pt: (layer, pt[b, g * pp + p], 0, 0, 0))

    row_spec = pl.BlockSpec((DEC_SEQ, width), lambda b, g, pt: (blk0 + b, 0))
    grid_spec = pltpu.PrefetchScalarGridSpec(
        num_scalar_prefetch=1,
        grid=(DEC_BATCH, PAST_LEN // (PAGE_SIZE * pp)),
        in_specs=[page_spec(p) for p in range(pp)] * 2
                 + [row_spec, row_spec, row_spec,
                    pl.BlockSpec((SUBLANES, LANES), lambda b, g, pt: (0, 0)),
                    pl.BlockSpec((1, width), lambda b, g, pt: (0, 0)),
                    pl.BlockSpec(memory_space=pl.ANY)],
        out_specs=row_spec,
        scratch_shapes=[pltpu.VMEM((rows, LANES), BF16), pltpu.VMEM((rows, PAGE_SIZE * D_HEADS), F32),
                        pltpu.VMEM((LANES, LANES), F32), pltpu.VMEM((LANES, LANES), F32),
                        pltpu.VMEM((rows, 1), F32), pltpu.VMEM((rows, 1), F32), pltpu.VMEM((rows, LANES), F32)])
    n_in = 2 * pp + 6
    return pl.pallas_call(
        functools.partial(_diff_sample_kernel, lam_init=lam_init),
        grid_spec=grid_spec,
        out_shape=jax.ShapeDtypeStruct(hd.shape, F32),
        input_output_aliases={n_in: 0},
        compiler_params=_params("arbitrary", "arbitrary"),
        name="diff_sample",
    )(page_table, *([cache_k] * pp), *([cache_v] * pp), qn, kf, vf, lam_pack, gain.reshape(1, -1), hd)


def _prep_weights(l, w_in, w_pm, w_pd, w_out, w_gu, w_down):
    sizes = (M_WIDTH, M_WIDTH, M_WIDTH, M_WIDTH, 2 * M_HEADS, D_QK_WIDTH, D_QK_WIDTH, D_V_WIDTH, D_MODEL, D_MODEL)
    offs = [0]
    for s in sizes:
        offs.append(offs[-1] + s)
    w = w_in[l]
    col = lambda i: w[:, offs[i]:offs[i + 1]]
    w_mlstm = jnp.concatenate([col(0), col(1) * (M_HEAD_DIM ** -0.5), col(2)], axis=1).astype(BF16)
    w_gif = jnp.pad(col(4), ((0, 0), (0, LANES - 2 * M_HEADS))).astype(BF16)
    w_diff = jnp.concatenate([col(5), col(6), col(7)], axis=1).astype(BF16)
    w_gate = jnp.concatenate([col(3), col(8), col(9)], axis=1).astype(BF16)
    return (w_mlstm, w_gif, w_diff, w_gate, w_pm[l].astype(BF16), w_pd[l].astype(BF16),
            w_out[l].astype(BF16), w_gu[l].astype(BF16), w_down[l].astype(BF16))


def kernel(x_prompt, x_sample, cache_k, cache_v, state_C, state_n, state_m, page_table, meta_tokens, norm_mix, w_in, b_if, q_gain, k_gain, lambda_q1, lambda_k1, lambda_q2, lambda_k2, mlstm_norm, diff_norm, w_pm, w_pd, w_out, norm_ffn, w_gu, w_down):
    assert M_HEAD_DIM ** -0.5 == 2.0 ** round(math.log2(M_HEAD_DIM ** -0.5))
    assert D_QK_DIM ** -0.5 == 2.0 ** round(math.log2(D_QK_DIM ** -0.5))
    n_prompt, n_sample, meta_row, r = _rows()
    x = jnp.concatenate([x_prompt.reshape(n_prompt, D_MODEL), x_sample.reshape(n_sample, D_MODEL),
                         meta_tokens.astype(F32), jnp.zeros((r - meta_row - N_META, D_MODEL), F32)], axis=0)
    slopes = 2.0 ** (-8.0 * jnp.arange(1, D_HEADS + 1, dtype=F32) / D_HEADS)

    def with_meta(rows_arr, tail):
        meta = jnp.broadcast_to(rows_arr[meta_row:meta_row + N_META][None], (BATCH, N_META) + tail)
        return jnp.concatenate([meta, rows_arr[:n_prompt].reshape((BATCH, SEQ) + tail)], axis=1)

    outs = [[] for _ in range(10)]
    for l in range(DEPTH):
        lam_init = 0.8 - 0.6 * math.exp(-0.3 * l)
        w_mlstm, w_gif, w_diff, w_gate, wpm, wpd, wout, wgu, wdown = _prep_weights(l, w_in, w_pm, w_pd, w_out, w_gu, w_down)
        qkv_m = _norm_matmul(x, norm_mix[l], w_mlstm, F32, "proj_mlstm")
        graw = _norm_matmul(x, norm_mix[l], w_gif, F32, "proj_gates")[:, :2 * M_HEADS]
        qkv_d = _norm_matmul(x, norm_mix[l], w_diff, F32, "proj_diff")
        gates = _norm_matmul(x, norm_mix[l], w_gate, F32, "proj_merge_gates")

        hm, c_p, n_p, m_p = _mlstm_prompt(qkv_m, gates, graw, graw.T, b_if[l], mlstm_norm[l])
        gs = graw[n_prompt:n_prompt + n_sample].reshape(DEC_BATCH, DEC_SEQ, 2 * M_HEADS)
        gs = jnp.pad(gs, ((0, 0), (0, SAMPLE_CHUNK - DEC_SEQ), (0, 0)))
        hm, c_s, n_s, m_s = _mlstm_sample(qkv_m, gates, gs, gs.transpose(0, 2, 1), b_if[l], mlstm_norm[l],
                                          state_C[l].astype(F32), state_n[l].astype(F32), state_m[l].astype(F32), hm)

        qn, kf, kb, vb = _qk_norm(qkv_d, q_gain[l], k_gain[l])
        vf = qkv_d[:, 2 * D_QK_WIDTH:]
        lam_pack = jnp.pad(jnp.stack([lambda_q1[l], lambda_k1[l], lambda_q2[l], lambda_k2[l]]).astype(F32),
                           ((0, SUBLANES - 4), (0, LANES - D_QK_DIM)))
        hd = _diff_prompt(qn, kb, vb, lam_pack, diff_norm[l], slopes, lam_init)
        hd = _diff_sample(l, cache_k, cache_v, page_table, qn, kf, vf, lam_pack, diff_norm[l], hd, lam_init)

        z = _merge(hm, hd, wpm, wpd, gates)
        x = _matmul_res(z, wout, x, "out_proj")
        act = _norm_swiglu(x, norm_ffn[l], wgu)
        x = _matmul_res(act, wdown, x, "ffn_down")

        head_shape = (D_HEADS, D_V_DIM)
        kf3 = kf.reshape(r, *head_shape)
        vf3 = vf.reshape(r, *head_shape)
        vals = (with_meta(kf3, head_shape), with_meta(vf3, head_shape), c_p, n_p, m_p[:, :, 0],
                kf3[n_prompt:n_prompt + n_sample].reshape(DEC_BATCH, DEC_SEQ, *head_shape),
                vf3[n_prompt:n_prompt + n_sample].reshape(DEC_BATCH, DEC_SEQ, *head_shape),
                c_s, n_s, m_s[:, :, 0])
        for o, v in zip(outs, vals):
            o.append(v)

    y_prompt = x[:n_prompt].reshape(BATCH, SEQ, D_MODEL)
    y_sample = x[n_prompt:n_prompt + n_sample].reshape(DEC_BATCH, DEC_SEQ, D_MODEL)
    return (y_prompt, y_sample) + tuple(jnp.stack(o) for o in outs)
```

```python
import functools
import math

import jax
import jax.numpy as jnp
from jax import lax
from jax.experimental import pallas as pl
from jax.experimental.pallas import tpu as pltpu

D_MODEL = 2048
BATCH = 2
SEQ = 4096
DEPTH = 2
DEC_BATCH = 32
DEC_SEQ = 8
PAST_LEN = 8192
PAGE_SIZE = 128
N_META = 16
M_HEADS = 4
M_HEAD_DIM = 256
M_WIDTH = M_HEADS * M_HEAD_DIM
D_HEADS = 8
D_QK_DIM = 64
D_V_DIM = 2 * D_QK_DIM
D_QK_WIDTH = D_HEADS * 2 * D_QK_DIM
D_V_WIDTH = D_HEADS * D_V_DIM
EPS = 1e-6

LANES = 128
SUBLANES = 8
ROW_BLOCK = 256
SAMPLE_CHUNK = 128
KV_UNROLL = 2
PAGES_PER_STEP = 8
VMEM_LIMIT = 56 * 1024 * 1024

F32 = jnp.float32
BF16 = jnp.bfloat16
NEG_INF = float("-inf")
LOG2E = math.log2(math.e)
Q_SCALE = D_QK_DIM ** -0.5 * LOG2E
ONES_ROWS = 16


def _ffn_dim():
    return ((8 * D_MODEL // 3 + 255) // 256) * 256


def _rows():
    n_prompt = BATCH * SEQ
    n_sample = DEC_BATCH * DEC_SEQ
    assert n_prompt % ROW_BLOCK == 0 and n_sample % ROW_BLOCK == 0 and N_META <= ROW_BLOCK
    return n_prompt, n_sample, n_prompt + n_sample, n_prompt + n_sample + ROW_BLOCK


def _row_tile(r):
    return 512 if r % 512 == 0 else ROW_BLOCK


def _col_tile(n):
    for t in (1024, 512, 256, 128):
        if n % t == 0:
            return t
    raise ValueError(n)


def _params(*sem):
    return pltpu.CompilerParams(dimension_semantics=sem, vmem_limit_bytes=VMEM_LIMIT)


def _dot(a, b):
    return jnp.dot(a, b, preferred_element_type=F32)


def _dot_nt(a, b):
    return lax.dot_general(a, b, (((1,), (1,)), ((), ())), preferred_element_type=F32)


def _dot_tn(a, b):
    return lax.dot_general(a, b, (((0,), (0,)), ((), ())), preferred_element_type=F32)


def _sigmoid(x):
    return 1.0 / (1.0 + jnp.exp(-x))


def _rms(x, gain):
    ms = jnp.mean(x * x, axis=-1, keepdims=True)
    return x * lax.rsqrt(ms + EPS) * gain


def _norm_matmul_kernel(x_ref, g_ref, w_ref, o_ref, xn_ref):
    @pl.when(pl.program_id(1) == 0)
    def _():
        xn_ref[...] = _rms(x_ref[...], g_ref[...]).astype(BF16)

    o_ref[...] = _dot(xn_ref[...], w_ref[...]).astype(o_ref.dtype)


def _norm_matmul(x, gain, w, out_dtype, name):
    r, d = x.shape
    n = w.shape[1]
    tm, tn = _row_tile(r), _col_tile(n)
    return pl.pallas_call(
        _norm_matmul_kernel,
        grid=(r // tm, n // tn),
        in_specs=[pl.BlockSpec((tm, d), lambda i, j: (i, 0)),
                  pl.BlockSpec((1, d), lambda i, j: (0, 0)),
                  pl.BlockSpec((d, tn), lambda i, j: (0, j))],
        out_specs=pl.BlockSpec((tm, tn), lambda i, j: (i, j)),
        out_shape=jax.ShapeDtypeStruct((r, n), out_dtype),
        scratch_shapes=[pltpu.VMEM((tm, d), BF16)],
        compiler_params=_params("parallel", "arbitrary"),
        name=name,
    )(x, gain.reshape(1, d), w)


def _norm_swiglu_kernel(x_ref, g_ref, wg_ref, wu_ref, o_ref, xn_ref):
    @pl.when(pl.program_id(1) == 0)
    def _():
        xn_ref[...] = _rms(x_ref[...], g_ref[...]).astype(BF16)

    xn = xn_ref[...]
    g = _dot(xn, wg_ref[...])
    u = _dot(xn, wu_ref[...])
    o_ref[...] = (g * _sigmoid(g) * u).astype(o_ref.dtype)


def _norm_swiglu(x, gain, w_gu):
    r, d = x.shape
    f = w_gu.shape[1] // 2
    tm, tn = _row_tile(r), _col_tile(f)
    nj = f // tn
    return pl.pallas_call(
        _norm_swiglu_kernel,
        grid=(r // tm, nj),
        in_specs=[pl.BlockSpec((tm, d), lambda i, j: (i, 0)),
                  pl.BlockSpec((1, d), lambda i, j: (0, 0)),
                  pl.BlockSpec((d, tn), lambda i, j: (0, j)),
                  pl.BlockSpec((d, tn), lambda i, j: (0, j + nj))],
        out_specs=pl.BlockSpec((tm, tn), lambda i, j: (i, j)),
        out_shape=jax.ShapeDtypeStruct((r, f), BF16),
        scratch_shapes=[pltpu.VMEM((tm, d), BF16)],
        compiler_params=_params("parallel", "arbitrary"),
        name="ffn_up",
    )(x, gain.reshape(1, d), w_gu, w_gu)


def _matmul_res_kernel(a_ref, w_ref, r_ref, o_ref):
    o_ref[...] = r_ref[...] + _dot(a_ref[...], w_ref[...])


def _matmul_res(a, w, res, name):
    r, k = a.shape
    n = w.shape[1]
    tm, tn = _row_tile(r), _col_tile(n)
    return pl.pallas_call(
        _matmul_res_kernel,
        grid=(r // tm, n // tn),
        in_specs=[pl.BlockSpec((tm, k), lambda i, j: (i, 0)),
                  pl.BlockSpec((k, tn), lambda i, j: (0, j)),
                  pl.BlockSpec((tm, tn), lambda i, j: (i, j))],
        out_specs=pl.BlockSpec((tm, tn), lambda i, j: (i, j)),
        out_shape=jax.ShapeDtypeStruct((r, n), F32),
        compiler_params=_params("parallel", "arbitrary"),
        name=name,
    )(a, w, res)


def _merge_kernel(hm_ref, hd_ref, wpm_ref, wpd_ref, gm_ref, gd_ref, o_ref):
    zm = _dot(hm_ref[...].astype(BF16), wpm_ref[...])
    zd = _dot(hd_ref[...].astype(BF16), wpd_ref[...])
    o_ref[...] = (_sigmoid(gm_ref[...]) * zm + _sigmoid(gd_ref[...]) * zd).astype(o_ref.dtype)


def _merge(hm, hd, w_pm, w_pd, gates):
    r = hm.shape[0]
    n = w_pm.shape[1]
    tm, tn = _row_tile(r), _col_tile(math.gcd(n, M_WIDTH))
    gm0, gd0 = M_WIDTH // tn, (M_WIDTH + n) // tn
    return pl.pallas_call(
        _merge_kernel,
        grid=(r // tm, n // tn),
        in_specs=[pl.BlockSpec((tm, M_WIDTH), lambda i, j: (i, 0)),
                  pl.BlockSpec((tm, D_V_WIDTH), lambda i, j: (i, 0)),
                  pl.BlockSpec((M_WIDTH, tn), lambda i, j: (0, j)),
                  pl.BlockSpec((D_V_WIDTH, tn), lambda i, j: (0, j)),
                  pl.BlockSpec((tm, tn), lambda i, j: (i, j + gm0)),
                  pl.BlockSpec((tm, tn), lambda i, j: (i, j + gd0))],
        out_specs=pl.BlockSpec((tm, tn), lambda i, j: (i, j)),
        out_shape=jax.ShapeDtypeStruct((r, n), BF16),
        compiler_params=_params("parallel", "arbitrary"),
        name="merge",
    )(hm, hd, w_pm, w_pd, gates, gates)


def _half_norm(x, gain):
    lo = lax.broadcasted_iota(jnp.int32, x.shape, 1) < D_QK_DIM
    x2 = x * x
    s_lo = jnp.sum(jnp.where(lo, x2, 0.0), axis=1, keepdims=True)
    s_hi = jnp.sum(jnp.where(lo, 0.0, x2), axis=1, keepdims=True)
    ms = jnp.where(lo, s_lo, s_hi) * (1.0 / D_QK_DIM)
    return x * lax.rsqrt(ms + EPS) * gain


def _qk_norm_kernel(q_ref, k_ref, v_ref, qg_ref, kg_ref, qo_ref, kf_ref, vf_ref, kb_ref, vt_ref):
    for h in range(D_HEADS):
        sl = slice(h * LANES, (h + 1) * LANES)
        qo_ref[:, sl] = _half_norm(q_ref[:, sl], qg_ref[...]) * Q_SCALE
        kn = _half_norm(k_ref[:, sl], kg_ref[...])
        kf_ref[:, h, :] = kn
        vf_ref[:, h, :] = v_ref[:, sl]
        kb_ref[:, sl] = kn.astype(BF16)
    for i in range(vt_ref.shape[0]):
        vt_ref[i] = v_ref[i * ROW_BLOCK:(i + 1) * ROW_BLOCK, :].T.astype(BF16)


def _qk_norm(qkv, q_gain, k_gain):
    r = qkv.shape[0]
    w = D_QK_WIDTH
    tm = _row_tile(r)
    nb = tm // ROW_BLOCK
    spec = lambda c: pl.BlockSpec((tm, w), lambda i: (i, c))
    gspec = pl.BlockSpec((1, LANES), lambda i: (0, 0))
    hspec = pl.BlockSpec((tm, D_HEADS, LANES), lambda i: (i, 0, 0))
    return pl.pallas_call(
        _qk_norm_kernel,
        grid=(r // tm,),
        in_specs=[spec(0), spec(1), spec(2), gspec, gspec],
        out_specs=[spec(0), hspec, hspec, spec(0),
                   pl.BlockSpec((nb, D_V_WIDTH, ROW_BLOCK), lambda i: (i, 0, 0))],
        out_shape=[jax.ShapeDtypeStruct((r, w), F32),
                   jax.ShapeDtypeStruct((r, D_HEADS, LANES), F32), jax.ShapeDtypeStruct((r, D_HEADS, LANES), F32),
                   jax.ShapeDtypeStruct((r, w), BF16),
                   jax.ShapeDtypeStruct((r // ROW_BLOCK, D_V_WIDTH, ROW_BLOCK), BF16)],
        compiler_params=_params("parallel"),
        name="qk_norm",
    )(qkv, qkv, qkv, jnp.tile(q_gain, 2).reshape(1, LANES), jnp.tile(k_gain, 2).reshape(1, LANES))


def _split3(x):
    hi = x.astype(BF16)
    r1 = x - hi.astype(F32)
    mid = r1.astype(BF16)
    lo = (r1 - mid.astype(F32)).astype(BF16)
    return hi, mid, lo


def _log_sigmoid(x):
    return jnp.minimum(x, 0.0) - jnp.log1p(jnp.exp(-jnp.abs(x)))


def _mlstm_gates(graw_col, graw_row, bias_col, bias_row, n_valid):
    l = graw_col.shape[0]
    g_col = graw_col + bias_col
    g_row = graw_row + bias_row
    is_f_col = lax.broadcasted_iota(jnp.int32, g_col.shape, 1) >= M_HEADS
    is_f_row = lax.broadcasted_iota(jnp.int32, g_row.shape, 0) >= M_HEADS
    ok_col = lax.broadcasted_iota(jnp.int32, g_col.shape, 0) < n_valid
    ok_row = lax.broadcasted_iota(jnp.int32, g_row.shape, 1) < n_valid
    lf_col = jnp.where(is_f_col & ok_col, _log_sigmoid(g_col), 0.0)
    lf_row = jnp.where(is_f_row & ok_row, _log_sigmoid(g_row), 0.0)
    ig_col = jnp.where(ok_col, g_col, NEG_INF)
    ig_row = jnp.where(ok_row, g_row, NEG_INF)
    rr = lax.broadcasted_iota(jnp.int32, (l, l), 0)
    cc = lax.broadcasted_iota(jnp.int32, (l, l), 1)
    tri_l = (cc <= rr).astype(BF16)
    tri_u = (rr <= cc).astype(BF16)
    b_col = sum(_dot(tri_l, p) for p in _split3(lf_col))
    b_row = sum(_dot(p, tri_u) for p in _split3(lf_row))
    return ig_col, b_col, ig_row, b_row, cc <= rr


def _mlstm_chunk(q, k, v, ig_col, b_col, ig_row, b_row, causal, c_st, n_st, m_st):
    l = q.shape[0]
    d = b_col - b_row + ig_row
    d = jnp.where(causal, d, NEG_INF)
    inter = b_col + m_st
    m_t = jnp.maximum(inter, jnp.max(d, axis=1, keepdims=True))
    w = jnp.exp(d - m_t)
    s = _dot_nt(q, k) * w
    a = jnp.exp(inter - m_t)
    num = _dot(s.astype(BF16), v) + a * _dot_nt(q, c_st.astype(BF16))
    den = jnp.sum(s, axis=1, keepdims=True) + a * jnp.sum(q.astype(F32) * n_st, axis=1, keepdims=True)
    h = num * (1.0 / jnp.maximum(jnp.abs(den), jnp.exp(-m_t)))

    b_last = b_col[l - 1:l, :]
    g_col = b_last - b_col + ig_col
    g_row = b_last - b_row + ig_row
    m_new = jnp.maximum(b_last + m_st, jnp.max(g_row, axis=1, keepdims=True))
    ws_col = jnp.exp(g_col - m_new)
    ws_row = jnp.exp(g_row - m_new)
    decay = jnp.exp(b_last + m_st - m_new)
    wv = (v.astype(F32) * ws_col).astype(BF16)
    c_new = decay * c_st + _dot_tn(wv, k)
    n_new = decay * n_st + _dot(ws_row.astype(BF16), k)
    return h, c_new, n_new, m_new


def _mlstm_heads(q_of, k_of, v_of, om_of, gates, gain_ref, c_ref, n_ref, m_ref, h_out):
    ig_col, b_col, ig_row, b_row, causal = gates
    for h in range(M_HEADS):
        f = M_HEADS + h
        hs = slice(h * M_HEAD_DIM, (h + 1) * M_HEAD_DIM)
        out, c_new, n_new, m_new = _mlstm_chunk(
            q_of(hs), k_of(hs), v_of(hs),
            ig_col[:, h:h + 1], b_col[:, f:f + 1], ig_row[h:h + 1, :], b_row[f:f + 1, :], causal,
            c_ref[0, h], n_ref[0, h:h + 1, :], m_ref[0, h:h + 1, 0:1])
        c_ref[0, h] = c_new
        n_ref[0, h:h + 1, :] = n_new
        m_ref[0, h:h + 1, :] = jnp.broadcast_to(m_new, (1, LANES))
        h_out(hs, _rms(out, gain_ref[:, hs]) * _sigmoid(om_of(hs)))


def _mlstm_prompt_kernel(q_ref, k_ref, v_ref, om_ref, gc_ref, gr_ref, bc_ref, br_ref, gain_ref,
                         h_ref, c_ref, n_ref, m_ref):
    c = pl.program_id(1)

    @pl.when(c == 0)
    def _():
        c_ref[...] = jnp.zeros_like(c_ref)
        n_ref[...] = jnp.zeros_like(n_ref)
        m_ref[...] = jnp.zeros_like(m_ref)

    n_valid = jnp.where(c == 0, N_META, ROW_BLOCK)
    gates = _mlstm_gates(gc_ref[...], gr_ref[...], bc_ref[...], br_ref[...], n_valid)

    def h_out(hs, val):
        h_ref[:, hs] = val

    _mlstm_heads(lambda hs: q_ref[:, hs].astype(BF16), lambda hs: k_ref[:, hs].astype(BF16),
                 lambda hs: v_ref[:, hs].astype(BF16), lambda hs: om_ref[:, hs],
                 gates, gain_ref, c_ref, n_ref, m_ref, h_out)


def _mlstm_prompt(qkv, gates, g_col, g_row, b_if, gain):
    r = qkv.shape[0]
    n_prompt, _, meta_row, _ = _rows()
    nc = SEQ // ROW_BLOCK
    meta_blk = meta_row // ROW_BLOCK

    def rb(b, c):
        return jnp.where(c == 0, meta_blk, b * nc + c - 1)

    def rb_out(b, c):
        return jnp.where((c == 0) & (b > 0), b * nc, rb(b, c))

    spec = lambda col: pl.BlockSpec((ROW_BLOCK, M_WIDTH), lambda b, c: (rb(b, c), col))
    full = lambda shape: pl.BlockSpec(shape, lambda b, c: (0,) * len(shape))
    return pl.pallas_call(
        _mlstm_prompt_kernel,
        grid=(BATCH, nc + 1),
        in_specs=[spec(0), spec(1), spec(2), spec(0),
                  pl.BlockSpec((ROW_BLOCK, 2 * M_HEADS), lambda b, c: (rb(b, c), 0)),
                  pl.BlockSpec((2 * M_HEADS, ROW_BLOCK), lambda b, c: (0, rb(b, c))),
                  full((1, 2 * M_HEADS)), full((2 * M_HEADS, 1)), full((1, M_WIDTH))],
        out_specs=[pl.BlockSpec((ROW_BLOCK, M_WIDTH), lambda b, c: (rb_out(b, c), 0)),
                   pl.BlockSpec((1, M_HEADS, M_HEAD_DIM, M_HEAD_DIM), lambda b, c: (b, 0, 0, 0)),
                   pl.BlockSpec((1, M_HEADS, M_HEAD_DIM), lambda b, c: (b, 0, 0)),
                   pl.BlockSpec((1, M_HEADS, LANES), lambda b, c: (b, 0, 0))],
        out_shape=[jax.ShapeDtypeStruct((r, M_WIDTH), F32),
                   jax.ShapeDtypeStruct((BATCH, M_HEADS, M_HEAD_DIM, M_HEAD_DIM), F32),
                   jax.ShapeDtypeStruct((BATCH, M_HEADS, M_HEAD_DIM), F32),
                   jax.ShapeDtypeStruct((BATCH, M_HEADS, LANES), F32)],
        compiler_params=_params("arbitrary", "arbitrary"),
        name="mlstm_prompt",
    )(qkv, qkv, qkv, gates, g_col, g_row, b_if.reshape(1, -1), b_if.reshape(-1, 1), gain.reshape(1, -1))


def _mlstm_sample_kernel(q_ref, k_ref, v_ref, om_ref, gc_ref, gr_ref, bc_ref, br_ref, gain_ref,
                         c_in, n_in, m_in, hm_in, h_ref, c_ref, n_ref, m_ref, qp, kp, vp):
    del hm_in
    c_ref[...] = c_in[...]
    n_ref[...] = n_in[...]
    m_ref[...] = m_in[...]
    for src, dst in ((q_ref, qp), (k_ref, kp), (v_ref, vp)):
        dst[...] = jnp.zeros_like(dst)
        dst[0:DEC_SEQ, :] = src[...]
    gates = _mlstm_gates(gc_ref[0], gr_ref[0], bc_ref[...], br_ref[...], DEC_SEQ)
    om = om_ref[...]

    def h_out(hs, val):
        h_ref[:, hs] = val[0:DEC_SEQ, :]

    def om_of(hs):
        return jnp.concatenate([om[:, hs], jnp.zeros((SAMPLE_CHUNK - DEC_SEQ, M_HEAD_DIM), F32)], axis=0)

    _mlstm_heads(lambda hs: qp[:, hs].astype(BF16), lambda hs: kp[:, hs].astype(BF16),
                 lambda hs: vp[:, hs].astype(BF16), om_of,
                 gates, gain_ref, c_ref, n_ref, m_ref, h_out)


def _mlstm_sample(qkv, gates, g_col, g_row, b_if, gain, st_c, st_n, st_m, hm):
    assert DEC_SEQ == SUBLANES
    n_prompt = _rows()[0]
    blk0 = n_prompt // DEC_SEQ
    spec = lambda col: pl.BlockSpec((DEC_SEQ, M_WIDTH), lambda b: (blk0 + b, col))
    full = lambda shape: pl.BlockSpec(shape, lambda b: (0,) * len(shape))
    st_spec = lambda shape: pl.BlockSpec((1,) + shape, lambda b: (b,) + (0,) * len(shape))
    st_specs = [st_spec((M_HEADS, M_HEAD_DIM, M_HEAD_DIM)), st_spec((M_HEADS, M_HEAD_DIM)),
                st_spec((M_HEADS, LANES))]
    return pl.pallas_call(
        _mlstm_sample_kernel,
        grid=(DEC_BATCH,),
        in_specs=[spec(0), spec(1), spec(2), spec(0),
                  st_spec((SAMPLE_CHUNK, 2 * M_HEADS)), st_spec((2 * M_HEADS, SAMPLE_CHUNK)),
                  full((1, 2 * M_HEADS)), full((2 * M_HEADS, 1)), full((1, M_WIDTH))] + st_specs
                 + [pl.BlockSpec(memory_space=pl.ANY)],
        out_specs=[spec(0)] + st_specs,
        out_shape=[jax.ShapeDtypeStruct(hm.shape, F32),
                   jax.ShapeDtypeStruct(st_c.shape, F32), jax.ShapeDtypeStruct(st_n.shape, F32),
                   jax.ShapeDtypeStruct((DEC_BATCH, M_HEADS, LANES), F32)],
        scratch_shapes=[pltpu.VMEM((SAMPLE_CHUNK, M_WIDTH), F32)] * 3,
        input_output_aliases={12: 0},
        compiler_params=_params("arbitrary"),
        name="mlstm_sample",
    )(qkv, qkv, qkv, gates, g_col, g_row, b_if.reshape(1, -1), b_if.reshape(-1, 1), gain.reshape(1, -1),
      st_c, st_n, jnp.broadcast_to(st_m[:, :, None], (DEC_BATCH, M_HEADS, LANES)), hm)


def _lambda(lam_ref, lam_init):
    p = lam_ref[...]
    s1 = jnp.sum(p[0:1, :] * p[1:2, :], axis=1, keepdims=True)
    s2 = jnp.sum(p[2:3, :] * p[3:4, :], axis=1, keepdims=True)
    return jnp.exp(s1) - jnp.exp(s2) + lam_init


def _diff_prompt_kernel(slope_ref, q_ref, k_ref, vt_ref, km_ref, vtm_ref, lam_ref, gain_ref, o_ref,
                        base_ref, m_ref, acc_ref, *, lam_init):
    b = pl.program_id(0)
    h = pl.program_id(1)
    is_meta = pl.program_id(2) == 0
    t = pl.program_id(2) - 1
    tq = ROW_BLOCK
    slope = slope_ref[h] * LOG2E

    @pl.when(jnp.logical_not(is_meta & (b > 0)))
    def _():
        q = q_ref[...].astype(BF16)
        lo = lax.broadcasted_iota(jnp.int32, q.shape, 1) < D_QK_DIM
        zero = jnp.zeros_like(q)
        q2 = jnp.concatenate([jnp.where(lo, q, zero), jnp.where(lo, zero, q)], axis=0)
        q_pos0 = jnp.where(is_meta, 0, N_META + t * tq)
        ones = jnp.ones((ONES_ROWS, tq), BF16)

        base_ref[...] = slope * lax.broadcasted_iota(jnp.int32, base_ref.shape, 0).astype(F32)
        m_ref[...] = jnp.full_like(m_ref, NEG_INF)
        acc_ref[...] = jnp.zeros_like(acc_ref)

        def visible(n_keys, k_off):
            key = lax.broadcasted_iota(jnp.int32, (n_keys, 2 * tq), 0) + k_off
            qry = lax.broadcasted_iota(jnp.int32, (n_keys, 2 * tq), 1)
            return key <= jnp.where(qry >= tq, qry - tq, qry)

        def step(s, vt, k_off):
            c = slope * k_off.astype(F32)
            m_old = m_ref[...]
            m_new = jnp.maximum(m_old, jnp.max(s, axis=0, keepdims=True) + c)
            p = jnp.exp2(s - (m_new - c)).astype(BF16)
            lhs = jnp.concatenate([vt, ones[:, 0:vt.shape[1]]], axis=0)
            acc_ref[...] = jnp.exp2(m_old - m_new) * acc_ref[...] + _dot(lhs, p)
            m_ref[...] = m_new

        def full_block(kb):
            ks = pl.ds(pl.multiple_of(kb * tq, tq), tq)
            step(_dot_nt(k_ref[ks, :], q2) + base_ref[...], vt_ref[kb], N_META + kb * tq - q_pos0)

        n_full = jnp.where(is_meta, 0, t)
        n_main = n_full // KV_UNROLL

        def unrolled(i, carry):
            for u in range(KV_UNROLL):
                full_block(i * KV_UNROLL + u)
            return carry

        def single(kb, carry):
            full_block(kb)
            return carry

        lax.fori_loop(0, n_main, unrolled, 0)
        lax.fori_loop(n_main * KV_UNROLL, n_full, single, 0)

        s = _dot_nt(km_ref[...], q2) + base_ref[0:N_META, :]
        step(jnp.where(visible(N_META, -q_pos0), s, NEG_INF), vtm_ref[0][:, 0:N_META], -q_pos0)
        td = jnp.maximum(t, 0)
        ks = pl.ds(pl.multiple_of(td * tq, tq), tq)
        s = _dot_nt(k_ref[ks, :], q2) + base_ref[...]
        hide = jnp.where(is_meta, 2 * tq, 0)
        step(jnp.where(visible(tq, hide), s, NEG_INF), vt_ref[td], jnp.int32(0))

        lam = _lambda(lam_ref, lam_init)
        acc = acc_ref[...]
        o_t = acc[0:D_V_DIM, :] * (1.0 / acc[D_V_DIM:D_V_DIM + 1, :])
        o = (o_t[:, 0:tq] - lam * o_t[:, tq:2 * tq]).T
        o_ref[...] = _rms(o, gain_ref[...]) * (1.0 - lam_init)


def _diff_prompt(qn, kb, vt, lam_pack, gain, slopes, lam_init):
    r = qn.shape[0]
    meta_row = _rows()[2]
    nq = SEQ // ROW_BLOCK
    meta_blk = meta_row // ROW_BLOCK
    assert meta_row % N_META == 0 and D_V_DIM == LANES

    def qmap(b, h, qi, *_):
        return (jnp.where(qi == 0, jnp.where(b == 0, meta_blk, b * nq), b * nq + qi - 1), h)

    grid_spec = pltpu.PrefetchScalarGridSpec(
        num_scalar_prefetch=1,
        grid=(BATCH, D_HEADS, nq + 1),
        in_specs=[pl.BlockSpec((ROW_BLOCK, LANES), qmap),
                  pl.BlockSpec((SEQ, LANES), lambda b, h, qi, *_: (b, h)),
                  pl.BlockSpec((nq, D_V_DIM, ROW_BLOCK), lambda b, h, qi, *_: (b, h, 0)),
                  pl.BlockSpec((N_META, LANES), lambda b, h, qi, *_: (meta_row // N_META, h)),
                  pl.BlockSpec((1, D_V_DIM, ROW_BLOCK), lambda b, h, qi, *_: (meta_blk, h, 0)),
                  pl.BlockSpec((SUBLANES, LANES), lambda b, h, qi, *_: (0, 0)),
                  pl.BlockSpec((1, LANES), lambda b, h, qi, *_: (0, h))],
        out_specs=pl.BlockSpec((ROW_BLOCK, LANES), qmap),
        scratch_shapes=[pltpu.VMEM((ROW_BLOCK, 2 * ROW_BLOCK), F32), pltpu.VMEM((1, 2 * ROW_BLOCK), F32),
                        pltpu.VMEM((D_V_DIM + ONES_ROWS, 2 * ROW_BLOCK), F32)])
    return pl.pallas_call(
        functools.partial(_diff_prompt_kernel, lam_init=lam_init),
        grid_spec=grid_spec,
        out_shape=jax.ShapeDtypeStruct((r, D_V_WIDTH), F32),
        compiler_params=_params("arbitrary", "arbitrary", "arbitrary"),
        name="diff_prompt",
    )(slopes, qn, kb, vt, kb, vt, lam_pack, gain.reshape(1, -1))


def _diff_sample_kernel(pt_ref, *refs, lam_init):
    del pt_ref
    pp = PAGES_PER_STEP
    k_pages, v_pages = refs[0:pp], refs[pp:2 * pp]
    (q_ref, kn_ref, vn_ref, lam_ref, gain_ref, hd_in, o_ref,
     qb_ref, bias_ref, kp_ref, vp_ref, m_ref, acc_ref) = refs[2 * pp:]
    del hd_in
    g = pl.program_id(1)
    rows = 2 * D_HEADS * DEC_SEQ
    half = D_HEADS * DEC_SEQ
    cols = PAGE_SIZE * D_HEADS

    row = lax.broadcasted_iota(jnp.int32, (rows, 1), 0)
    r_tok = row % DEC_SEQ
    r_head = (row // DEC_SEQ) % D_HEADS
    slope = jnp.exp2(-(r_head + 1).astype(F32) * (8.0 / D_HEADS)) * LOG2E

    def step(s, v, c):
        m_old = m_ref[...]
        m_new = jnp.maximum(m_old, jnp.max(s, axis=1, keepdims=True) + c)
        p = jnp.exp2(s - (m_new - c)).astype(BF16)
        v_ones = jnp.concatenate([v, jnp.ones_like(v)], axis=1)
        acc_ref[...] = jnp.exp2(m_old - m_new) * acc_ref[...] + _dot(p, v_ones)
        m_ref[...] = m_new

    @pl.when(g == 0)
    def _():
        lo = lax.broadcasted_iota(jnp.int32, (DEC_SEQ, LANES), 1) < D_QK_DIM
        q = q_ref[...]
        heads = [q[:, h * LANES:(h + 1) * LANES] for h in range(D_HEADS)]
        qb_ref[...] = jnp.concatenate([jnp.where(lo, x, 0.0) for x in heads]
                                      + [jnp.where(lo, 0.0, x) for x in heads], axis=0).astype(BF16)
        col = lax.broadcasted_iota(jnp.int32, (rows, cols), 1)
        bias_ref[...] = jnp.where(col % D_HEADS == r_head, slope * (col // D_HEADS).astype(F32), NEG_INF)
        m_ref[...] = jnp.full_like(m_ref, NEG_INF)
        acc_ref[...] = jnp.zeros_like(acc_ref)

    qb = qb_ref[...]
    for p in range(pp):
        page_pos0 = (g * pp + p) * PAGE_SIZE - PAST_LEN
        k2 = k_pages[p][...].reshape(cols, LANES).astype(BF16)
        v2 = v_pages[p][...].reshape(cols, LANES).astype(BF16)
        step(_dot_nt(qb, k2) + bias_ref[...], v2, slope * page_pos0.astype(F32))

    @pl.when(g == pl.num_programs(1) - 1)
    def _():
        for src, dst in ((kn_ref, kp_ref), (vn_ref, vp_ref)):
            dst[...] = jnp.zeros_like(dst)
            dst[0:half, :] = src[...].reshape(half, LANES)
        c = lax.broadcasted_iota(jnp.int32, (1, LANES), 1)
        c_tok = c // D_HEADS
        ok = (c % D_HEADS == r_head) & (c_tok <= r_tok)
        s = _dot_nt(qb, kp_ref[...].astype(BF16)) + slope * c_tok.astype(F32)
        step(jnp.where(ok, s, NEG_INF), vp_ref[...].astype(BF16), jnp.zeros((rows, 1), F32))

        lam = _lambda(lam_ref, lam_init)
        acc = acc_ref[...]
        o = acc[:, 0:LANES] * (1.0 / acc[:, LANES:2 * LANES])
        for h in range(D_HEADS):
            cs = slice(h * LANES, (h + 1) * LANES)
            oh = o[h * DEC_SEQ:(h + 1) * DEC_SEQ, :] - lam * o[half + h * DEC_SEQ:half + (h + 1) * DEC_SEQ, :]
            o_ref[:, cs] = _rms(oh, gain_ref[:, cs]) * (1.0 - lam_init)


def _diff_sample(layer, cache_k, cache_v, page_table, qn, kf, vf, lam_pack, gain, hd, lam_init):
    assert DEC_SEQ == SUBLANES and D_HEADS == SUBLANES and D_V_DIM == LANES
    assert D_HEADS * DEC_SEQ <= LANES and PAST_LEN % (PAGE_SIZE * PAGES_PER_STEP) == 0
    n_prompt = _rows()[0]
    blk0 = n_prompt // DEC_SEQ
    pp = PAGES_PER_STEP
    width = D_HEADS * LANES
    rows = 2 * D_HEADS * DEC_SEQ

    def page_spec(p):
        return pl.BlockSpec((None, None, PAGE_SIZE, D_HEADS, LANES),
                            lambda b, g, pt: (layer, pt[b, g * pp + p], 0, 0, 0))

    row_spec = pl.BlockSpec((DEC_SEQ, width), lambda b, g, pt: (blk0 + b, 0))
    new_spec = pl.BlockSpec((DEC_SEQ, D_HEADS, LANES), lambda b, g, pt: (blk0 + b, 0, 0))
    grid_spec = pltpu.PrefetchScalarGridSpec(
        num_scalar_prefetch=1,
        grid=(DEC_BATCH, PAST_LEN // (PAGE_SIZE * pp)),
        in_specs=[page_spec(p) for p in range(pp)] * 2
                 + [row_spec, new_spec, new_spec,
                    pl.BlockSpec((SUBLANES, LANES), lambda b, g, pt: (0, 0)),
                    pl.BlockSpec((1, width), lambda b, g, pt: (0, 0)),
                    pl.BlockSpec(memory_space=pl.ANY)],
        out_specs=row_spec,
        scratch_shapes=[pltpu.VMEM((rows, LANES), BF16), pltpu.VMEM((rows, PAGE_SIZE * D_HEADS), F32),
                        pltpu.VMEM((LANES, LANES), F32), pltpu.VMEM((LANES, LANES), F32),
                        pltpu.VMEM((rows, 1), F32), pltpu.VMEM((rows, 2 * LANES), F32)])
    n_in = 2 * pp + 6
    return pl.pallas_call(
        functools.partial(_diff_sample_kernel, lam_init=lam_init),
        grid_spec=grid_spec,
        out_shape=jax.ShapeDtypeStruct(hd.shape, F32),
        input_output_aliases={n_in: 0},
        compiler_params=_params("arbitrary", "arbitrary"),
        name="diff_sample",
    )(page_table, *([cache_k] * pp), *([cache_v] * pp), qn, kf, vf, lam_pack, gain.reshape(1, -1), hd)


def _prep_weights(l, w_in, w_pm, w_pd, w_out, w_gu, w_down):
    sizes = (M_WIDTH, M_WIDTH, M_WIDTH, M_WIDTH, 2 * M_HEADS, D_QK_WIDTH, D_QK_WIDTH, D_V_WIDTH, D_MODEL, D_MODEL)
    offs = [0]
    for s in sizes:
        offs.append(offs[-1] + s)
    w = w_in[l]
    col = lambda i: w[:, offs[i]:offs[i + 1]]
    w_mlstm = jnp.concatenate([col(0), col(1) * (M_HEAD_DIM ** -0.5), col(2)], axis=1).astype(BF16)
    w_gif = jnp.pad(col(4), ((0, 0), (0, LANES - 2 * M_HEADS))).astype(BF16)
    w_diff = jnp.concatenate([col(5), col(6), col(7)], axis=1).astype(BF16)
    w_gate = jnp.concatenate([col(3), col(8), col(9)], axis=1).astype(BF16)
    return (w_mlstm, w_gif, w_diff, w_gate, w_pm[l].astype(BF16), w_pd[l].astype(BF16),
            w_out[l].astype(BF16), w_gu[l].astype(BF16), w_down[l].astype(BF16))


def kernel(x_prompt, x_sample, cache_k, cache_v, state_C, state_n, state_m, page_table, meta_tokens, norm_mix, w_in, b_if, q_gain, k_gain, lambda_q1, lambda_k1, lambda_q2, lambda_k2, mlstm_norm, diff_norm, w_pm, w_pd, w_out, norm_ffn, w_gu, w_down):
    assert M_HEAD_DIM ** -0.5 == 2.0 ** round(math.log2(M_HEAD_DIM ** -0.5))
    n_prompt, n_sample, meta_row, r = _rows()
    x = jnp.concatenate([x_prompt.reshape(n_prompt, D_MODEL), x_sample.reshape(n_sample, D_MODEL),
                         meta_tokens.astype(F32), jnp.zeros((r - meta_row - N_META, D_MODEL), F32)], axis=0)
    slopes = 2.0 ** (-8.0 * jnp.arange(1, D_HEADS + 1, dtype=F32) / D_HEADS)

    def with_meta(rows_arr, tail):
        meta = jnp.broadcast_to(rows_arr[meta_row:meta_row + N_META][None], (BATCH, N_META) + tail)
        return jnp.concatenate([meta, rows_arr[:n_prompt].reshape((BATCH, SEQ) + tail)], axis=1)

    outs = [[] for _ in range(10)]
    for l in range(DEPTH):
        lam_init = 0.8 - 0.6 * math.exp(-0.3 * l)
        w_mlstm, w_gif, w_diff, w_gate, wpm, wpd, wout, wgu, wdown = _prep_weights(l, w_in, w_pm, w_pd, w_out, w_gu, w_down)
        qkv_m = _norm_matmul(x, norm_mix[l], w_mlstm, F32, "proj_mlstm")
        graw = _norm_matmul(x, norm_mix[l], w_gif, F32, "proj_gates")[:, :2 * M_HEADS]
        qkv_d = _norm_matmul(x, norm_mix[l], w_diff, F32, "proj_diff")
        gates = _norm_matmul(x, norm_mix[l], w_gate, F32, "proj_merge_gates")

        hm, c_p, n_p, m_p = _mlstm_prompt(qkv_m, gates, graw, graw.T, b_if[l], mlstm_norm[l])
        gs = graw[n_prompt:n_prompt + n_sample].reshape(DEC_BATCH, DEC_SEQ, 2 * M_HEADS)
        gs = jnp.pad(gs, ((0, 0), (0, SAMPLE_CHUNK - DEC_SEQ), (0, 0)))
        hm, c_s, n_s, m_s = _mlstm_sample(qkv_m, gates, gs, gs.transpose(0, 2, 1), b_if[l], mlstm_norm[l],
                                          state_C[l].astype(F32), state_n[l].astype(F32), state_m[l].astype(F32), hm)

        qn, kf, vf, kb, vt = _qk_norm(qkv_d, q_gain[l], k_gain[l])
        lam_pack = jnp.pad(jnp.stack([lambda_q1[l], lambda_k1[l], lambda_q2[l], lambda_k2[l]]).astype(F32),
                           ((0, SUBLANES - 4), (0, LANES - D_QK_DIM)))
        hd = _diff_prompt(qn, kb, vt, lam_pack, diff_norm[l], slopes, lam_init)
        hd = _diff_sample(l, cache_k, cache_v, page_table, qn, kf, vf, lam_pack, diff_norm[l], hd, lam_init)

        z = _merge(hm, hd, wpm, wpd, gates)
        x = _matmul_res(z, wout, x, "out_proj")
        act = _norm_swiglu(x, norm_ffn[l], wgu)
        x = _matmul_res(act, wdown, x, "ffn_down")

        head_shape = (D_HEADS, D_V_DIM)
        vals = (with_meta(kf, head_shape), with_meta(vf, head_shape), c_p, n_p, m_p[:, :, 0],
                kf[n_prompt:n_prompt + n_sample].reshape(DEC_BATCH, DEC_SEQ, *head_shape),
                vf[n_prompt:n_prompt + n_sample].reshape(DEC_BATCH, DEC_SEQ, *head_shape),
                c_s, n_s, m_s[:, :, 0])
        for o, v in zip(outs, vals):
            o.append(v)

    y_prompt = x[:n_prompt].reshape(BATCH, SEQ, D_MODEL)
    y_sample = x[n_prompt:n_prompt + n_sample].reshape(DEC_BATCH, DEC_SEQ, D_MODEL)
    return (y_prompt, y_sample) + tuple(jnp.stack(o) for o in outs)
```

```python
import functools
import math

import jax
import jax.numpy as jnp
from jax import lax
from jax.experimental import pallas as pl
from jax.experimental.pallas import tpu as pltpu

D_MODEL = 2048
BATCH = 2
SEQ = 4096
DEPTH = 2
DEC_BATCH = 32
DEC_SEQ = 8
PAST_LEN = 8192
PAGE_SIZE = 128
N_META = 16
M_HEADS = 4
M_HEAD_DIM = 256
M_WIDTH = M_HEADS * M_HEAD_DIM
D_HEADS = 8
D_QK_DIM = 64
D_V_DIM = 2 * D_QK_DIM
D_QK_WIDTH = D_HEADS * 2 * D_QK_DIM
D_V_WIDTH = D_HEADS * D_V_DIM
EPS = 1e-6

LANES = 128
SUBLANES = 8
ROW_BLOCK = 256
MATMUL_ROWS = 1100
K_SCALE = M_HEAD_DIM ** -0.5
SAMPLE_CHUNK = 128
KV_UNROLL = 1
HEAD_GROUP = 4
PAGES_PER_STEP = 8
VMEM_LIMIT = 56 * 1024 * 1024

F32 = jnp.float32
BF16 = jnp.bfloat16
NEG_INF = float("-inf")
LOG2E = math.log2(math.e)
Q_SCALE = D_QK_DIM ** -0.5 * LOG2E
ONES_ROWS = 16


def _ffn_dim():
    return ((8 * D_MODEL // 3 + 255) // 256) * 256


def _rows():
    n_prompt = BATCH * SEQ
    n_sample = DEC_BATCH * DEC_SEQ
    assert n_prompt % ROW_BLOCK == 0 and n_sample % ROW_BLOCK == 0 and N_META <= ROW_BLOCK
    return n_prompt, n_sample, n_prompt + n_sample, n_prompt + n_sample + ROW_BLOCK


def _row_tile(r):
    return 512 if r % 512 == 0 else ROW_BLOCK


def _tile(n, cap, mult=16):
    for t in range(min(cap, n) // mult * mult, 0, -mult):
        if n % t == 0:
            return t
    raise ValueError((n, cap, mult))


def _col_tile(n):
    for t in (1024, 512, 256, 128):
        if n % t == 0:
            return t
    raise ValueError(n)


def _params(*sem):
    return pltpu.CompilerParams(dimension_semantics=sem, vmem_limit_bytes=VMEM_LIMIT)


def _dot(a, b):
    return jnp.dot(a, b, preferred_element_type=F32)


def _dot_nt(a, b):
    return lax.dot_general(a, b, (((1,), (1,)), ((), ())), preferred_element_type=F32)


def _dot_tn(a, b):
    return lax.dot_general(a, b, (((0,), (0,)), ((), ())), preferred_element_type=F32)


def _sigmoid(x):
    return 1.0 / (1.0 + jnp.exp(-x))


def _rms(x, gain):
    ms = jnp.mean(x * x, axis=-1, keepdims=True)
    return x * lax.rsqrt(ms + EPS) * gain


def _rmsnorm_kernel(x_ref, g_ref, o_ref):
    o_ref[...] = _rms(x_ref[...], g_ref[...]).astype(o_ref.dtype)


def _rmsnorm(x, gain, name):
    r, d = x.shape
    tm = _tile(r, MATMUL_ROWS)
    return pl.pallas_call(
        _rmsnorm_kernel,
        grid=(r // tm,),
        in_specs=[pl.BlockSpec((tm, d), lambda i: (i, 0)), pl.BlockSpec((1, d), lambda i: (0, 0))],
        out_specs=pl.BlockSpec((tm, d), lambda i: (i, 0)),
        out_shape=jax.ShapeDtypeStruct((r, d), BF16),
        compiler_params=_params("parallel"),
        name=name,
    )(x, gain.reshape(1, d))


def _weight_spec(w, layer, tn, col0):
    if w.ndim == 3:
        return pl.BlockSpec((None, w.shape[1], tn), lambda j, i: (layer, 0, col0 + j))
    return pl.BlockSpec((w.shape[0], tn), lambda j, i: (0, col0 + j))


def _cast_weights(w_refs, wb_refs):
    @pl.when(pl.program_id(1) == 0)
    def _():
        for w_ref, wb_ref in zip(w_refs, wb_refs):
            wb_ref[...] = w_ref[...].astype(BF16)


def _matmul_kernel(a_ref, w_ref, o_ref, wb_ref):
    _cast_weights([w_ref], [wb_ref])
    o_ref[...] = _dot(a_ref[...], wb_ref[...]).astype(o_ref.dtype)


def _matmul(a, w, layer, n, out_dtype, name, tn=None):
    r, k = a.shape
    tm, tn = _tile(r, MATMUL_ROWS), tn or _col_tile(n)
    return pl.pallas_call(
        _matmul_kernel,
        grid=(n // tn, r // tm),
        in_specs=[pl.BlockSpec((tm, k), lambda j, i: (i, 0)), _weight_spec(w, layer, tn, 0)],
        out_specs=pl.BlockSpec((tm, tn), lambda j, i: (i, j)),
        out_shape=jax.ShapeDtypeStruct((r, n), out_dtype),
        scratch_shapes=[pltpu.VMEM((k, tn), BF16)],
        compiler_params=_params("parallel", "arbitrary"),
        name=name,
    )(a, w)


def _swiglu_kernel(a_ref, wg_ref, wu_ref, o_ref, wgb_ref, wub_ref):
    _cast_weights([wg_ref, wu_ref], [wgb_ref, wub_ref])
    a = a_ref[...]
    g = _dot(a, wgb_ref[...])
    u = _dot(a, wub_ref[...])
    o_ref[...] = (g * _sigmoid(g) * u).astype(o_ref.dtype)


def _swiglu(a, w_gu, layer):
    r, k = a.shape
    f = w_gu.shape[2] // 2
    tm, tn = _tile(r, MATMUL_ROWS), min(512, _col_tile(f))
    nj = f // tn
    return pl.pallas_call(
        _swiglu_kernel,
        grid=(nj, r // tm),
        in_specs=[pl.BlockSpec((tm, k), lambda j, i: (i, 0)),
                  _weight_spec(w_gu, layer, tn, 0), _weight_spec(w_gu, layer, tn, nj)],
        out_specs=pl.BlockSpec((tm, tn), lambda j, i: (i, j)),
        out_shape=jax.ShapeDtypeStruct((r, f), BF16),
        scratch_shapes=[pltpu.VMEM((k, tn), BF16)] * 2,
        compiler_params=_params("parallel", "arbitrary"),
        name="ffn_up",
    )(a, w_gu, w_gu)


def _matmul_res_kernel(a_ref, w_ref, r_ref, o_ref, wb_ref):
    _cast_weights([w_ref], [wb_ref])
    o_ref[...] = r_ref[...] + _dot(a_ref[...], wb_ref[...])


def _matmul_res(a, w, layer, res, name, rows):
    r, k = a.shape
    n = w.shape[2]
    tm, tn = _tile(r, rows), min(512, _col_tile(n))
    return pl.pallas_call(
        _matmul_res_kernel,
        grid=(n // tn, r // tm),
        in_specs=[pl.BlockSpec((tm, k), lambda j, i: (i, 0)), _weight_spec(w, layer, tn, 0),
                  pl.BlockSpec((tm, tn), lambda j, i: (i, j))],
        out_specs=pl.BlockSpec((tm, tn), lambda j, i: (i, j)),
        out_shape=jax.ShapeDtypeStruct((r, n), F32),
        scratch_shapes=[pltpu.VMEM((k, tn), BF16)],
        compiler_params=_params("parallel", "arbitrary"),
        name=name,
    )(a, w, res)


def _merge_kernel(hm_ref, hd_ref, wpm_ref, wpd_ref, gm_ref, gd_ref, o_ref, wpmb_ref, wpdb_ref):
    _cast_weights([wpm_ref, wpd_ref], [wpmb_ref, wpdb_ref])
    zm = _dot(hm_ref[...].astype(BF16), wpmb_ref[...])
    zd = _dot(hd_ref[...].astype(BF16), wpdb_ref[...])
    o_ref[...] = (_sigmoid(gm_ref[...]) * zm + _sigmoid(gd_ref[...]) * zd).astype(o_ref.dtype)


def _merge(hm, hd, w_pm, w_pd, layer, gates, gate_col):
    r = hm.shape[0]
    n = w_pm.shape[2]
    tm, tn = _tile(r, MATMUL_ROWS // 2), _col_tile(math.gcd(n, gate_col))
    gm0, gd0 = gate_col // tn, (gate_col + n) // tn
    return pl.pallas_call(
        _merge_kernel,
        grid=(n // tn, r // tm),
        in_specs=[pl.BlockSpec((tm, M_WIDTH), lambda j, i: (i, 0)),
                  pl.BlockSpec((tm, D_V_WIDTH), lambda j, i: (i, 0)),
                  _weight_spec(w_pm, layer, tn, 0), _weight_spec(w_pd, layer, tn, 0),
                  pl.BlockSpec((tm, tn), lambda j, i: (i, j + gm0)),
                  pl.BlockSpec((tm, tn), lambda j, i: (i, j + gd0))],
        out_specs=pl.BlockSpec((tm, tn), lambda j, i: (i, j)),
        out_shape=jax.ShapeDtypeStruct((r, n), BF16),
        scratch_shapes=[pltpu.VMEM((M_WIDTH, tn), BF16), pltpu.VMEM((D_V_WIDTH, tn), BF16)],
        compiler_params=_params("parallel", "arbitrary"),
        name="merge",
    )(hm, hd, w_pm, w_pd, gates, gates)


def _half_norm(x, gain):
    lo = lax.broadcasted_iota(jnp.int32, x.shape, 1) < D_QK_DIM
    x2 = x * x
    s_lo = jnp.sum(jnp.where(lo, x2, 0.0), axis=1, keepdims=True)
    s_hi = jnp.sum(jnp.where(lo, 0.0, x2), axis=1, keepdims=True)
    ms = jnp.where(lo, s_lo, s_hi) * (1.0 / D_QK_DIM)
    return x * lax.rsqrt(ms + EPS) * gain


def _qk_norm_kernel(q_ref, k_ref, v_ref, qg_ref, kg_ref, qo_ref, kf_ref, vf_ref, kb_ref, vt_ref):
    for h in range(D_HEADS):
        sl = slice(h * LANES, (h + 1) * LANES)
        qo_ref[:, sl] = _half_norm(q_ref[:, sl], qg_ref[...]) * Q_SCALE
        kn = _half_norm(k_ref[:, sl], kg_ref[...])
        kf_ref[:, h, :] = kn
        vf_ref[:, h, :] = v_ref[:, sl]
        kb_ref[:, sl] = kn.astype(BF16)
    for i in range(vt_ref.shape[0]):
        vt_ref[i] = v_ref[i * ROW_BLOCK:(i + 1) * ROW_BLOCK, :].T.astype(BF16)


def _qk_norm(qkv, q_gain, k_gain):
    r = qkv.shape[0]
    w = D_QK_WIDTH
    tm = _row_tile(r)
    nb = tm // ROW_BLOCK
    spec = lambda c: pl.BlockSpec((tm, w), lambda i: (i, c))
    gspec = pl.BlockSpec((1, LANES), lambda i: (0, 0))
    hspec = pl.BlockSpec((tm, D_HEADS, LANES), lambda i: (i, 0, 0))
    return pl.pallas_call(
        _qk_norm_kernel,
        grid=(r // tm,),
        in_specs=[spec(0), spec(1), spec(2), gspec, gspec],
        out_specs=[spec(0), hspec, hspec, spec(0),
                   pl.BlockSpec((nb, D_V_WIDTH, ROW_BLOCK), lambda i: (i, 0, 0))],
        out_shape=[jax.ShapeDtypeStruct((r, w), F32),
                   jax.ShapeDtypeStruct((r, D_HEADS, LANES), F32), jax.ShapeDtypeStruct((r, D_HEADS, LANES), F32),
                   jax.ShapeDtypeStruct((r, w), BF16),
                   jax.ShapeDtypeStruct((r // ROW_BLOCK, D_V_WIDTH, ROW_BLOCK), BF16)],
        compiler_params=_params("parallel"),
        name="qk_norm",
    )(qkv, qkv, qkv, jnp.tile(q_gain, 2).reshape(1, LANES), jnp.tile(k_gain, 2).reshape(1, LANES))


def _split3(x):
    hi = x.astype(BF16)
    r1 = x - hi.astype(F32)
    mid = r1.astype(BF16)
    lo = (r1 - mid.astype(F32)).astype(BF16)
    return hi, mid, lo


def _log_sigmoid(x):
    return jnp.minimum(x, 0.0) - jnp.log1p(jnp.exp(-jnp.abs(x)))


def _mlstm_gates(graw_col, graw_row, bias_col, bias_row, n_valid):
    l = graw_col.shape[0]
    g_col = graw_col + bias_col
    g_row = graw_row + bias_row
    is_f_col = lax.broadcasted_iota(jnp.int32, g_col.shape, 1) >= M_HEADS
    is_f_row = lax.broadcasted_iota(jnp.int32, g_row.shape, 0) >= M_HEADS
    ok_col = lax.broadcasted_iota(jnp.int32, g_col.shape, 0) < n_valid
    ok_row = lax.broadcasted_iota(jnp.int32, g_row.shape, 1) < n_valid
    lf_col = jnp.where(is_f_col & ok_col, _log_sigmoid(g_col), 0.0)
    lf_row = jnp.where(is_f_row & ok_row, _log_sigmoid(g_row), 0.0)
    ig_col = jnp.where(ok_col, g_col, NEG_INF)
    ig_row = jnp.where(ok_row, g_row, NEG_INF)
    rr = lax.broadcasted_iota(jnp.int32, (l, l), 0)
    cc = lax.broadcasted_iota(jnp.int32, (l, l), 1)
    tri_l = (cc <= rr).astype(BF16)
    tri_u = (rr <= cc).astype(BF16)
    b_col = sum(_dot(tri_l, p) for p in _split3(lf_col))
    b_row = sum(_dot(p, tri_u) for p in _split3(lf_row))
    return ig_col, b_col, ig_row, b_row, cc <= rr


def _mlstm_chunk(q, k, v, ig_col, b_col, ig_row, b_row, causal, c_st, n_st, m_st):
    l = q.shape[0]
    d = b_col - b_row + ig_row
    d = jnp.where(causal, d, NEG_INF)
    inter = b_col + m_st
    m_t = jnp.maximum(inter, jnp.max(d, axis=1, keepdims=True))
    w = jnp.exp(d - m_t)
    s = _dot_nt(q, k) * w
    a = jnp.exp(inter - m_t)
    num = _dot(s.astype(BF16), v) + a * _dot_nt(q, c_st.astype(BF16))
    den = jnp.sum(s, axis=1, keepdims=True) + a * jnp.sum(q.astype(F32) * n_st, axis=1, keepdims=True)
    h = num * (1.0 / jnp.maximum(jnp.abs(den), jnp.exp(-m_t)))

    b_last = b_col[l - 1:l, :]
    g_col = b_last - b_col + ig_col
    g_row = b_last - b_row + ig_row
    m_new = jnp.maximum(b_last + m_st, jnp.max(g_row, axis=1, keepdims=True))
    ws_col = jnp.exp(g_col - m_new)
    ws_row = jnp.exp(g_row - m_new)
    decay = jnp.exp(b_last + m_st - m_new)
    wv = (v.astype(F32) * ws_col).astype(BF16)
    c_new = decay * c_st + _dot_tn(wv, k)
    n_new = decay * n_st + _dot(ws_row.astype(BF16), k)
    return h, c_new, n_new, m_new


def _mlstm_heads(q_of, k_of, v_of, om_of, gates, gain_ref, c_ref, n_ref, m_ref, h_out):
    ig_col, b_col, ig_row, b_row, causal = gates
    for h in range(M_HEADS):
        f = M_HEADS + h
        hs = slice(h * M_HEAD_DIM, (h + 1) * M_HEAD_DIM)
        out, c_new, n_new, m_new = _mlstm_chunk(
            q_of(hs), k_of(hs), v_of(hs),
            ig_col[:, h:h + 1], b_col[:, f:f + 1], ig_row[h:h + 1, :], b_row[f:f + 1, :], causal,
            c_ref[0, h], n_ref[0, h:h + 1, :], m_ref[0, h:h + 1, 0:1])
        c_ref[0, h] = c_new
        n_ref[0, h:h + 1, :] = n_new
        m_ref[0, h:h + 1, :] = jnp.broadcast_to(m_new, (1, LANES))
        h_out(hs, _rms(out, gain_ref[:, hs]) * _sigmoid(om_of(hs)))


def _mlstm_prompt_kernel(q_ref, k_ref, v_ref, om_ref, gc_ref, gr_ref, bc_ref, br_ref, gain_ref,
                         h_ref, c_ref, n_ref, m_ref):
    c = pl.program_id(1)

    @pl.when(c == 0)
    def _():
        c_ref[...] = jnp.zeros_like(c_ref)
        n_ref[...] = jnp.zeros_like(n_ref)
        m_ref[...] = jnp.zeros_like(m_ref)

    n_valid = jnp.where(c == 0, N_META, ROW_BLOCK)
    gates = _mlstm_gates(gc_ref[...], gr_ref[...], bc_ref[...], br_ref[...], n_valid)

    def h_out(hs, val):
        h_ref[:, hs] = val

    _mlstm_heads(lambda hs: q_ref[:, hs].astype(BF16), lambda hs: (k_ref[:, hs] * K_SCALE).astype(BF16),
                 lambda hs: v_ref[:, hs].astype(BF16), lambda hs: om_ref[:, hs],
                 gates, gain_ref, c_ref, n_ref, m_ref, h_out)


def _mlstm_prompt(qkv, gates, g_col, g_row, b_if, gain):
    r = qkv.shape[0]
    n_prompt, _, meta_row, _ = _rows()
    nc = SEQ // ROW_BLOCK
    meta_blk = meta_row // ROW_BLOCK

    def rb(b, c):
        return jnp.where(c == 0, meta_blk, b * nc + c - 1)

    def rb_out(b, c):
        return jnp.where((c == 0) & (b > 0), b * nc, rb(b, c))

    spec = lambda col: pl.BlockSpec((ROW_BLOCK, M_WIDTH), lambda b, c: (rb(b, c), col))
    full = lambda shape: pl.BlockSpec(shape, lambda b, c: (0,) * len(shape))
    return pl.pallas_call(
        _mlstm_prompt_kernel,
        grid=(BATCH, nc + 1),
        in_specs=[spec(0), spec(1), spec(2), spec(3),
                  pl.BlockSpec((ROW_BLOCK, 2 * M_HEADS), lambda b, c: (rb(b, c), 0)),
                  pl.BlockSpec((2 * M_HEADS, ROW_BLOCK), lambda b, c: (0, rb(b, c))),
                  full((1, 2 * M_HEADS)), full((2 * M_HEADS, 1)), full((1, M_WIDTH))],
        out_specs=[pl.BlockSpec((ROW_BLOCK, M_WIDTH), lambda b, c: (rb_out(b, c), 0)),
                   pl.BlockSpec((1, M_HEADS, M_HEAD_DIM, M_HEAD_DIM), lambda b, c: (b, 0, 0, 0)),
                   pl.BlockSpec((1, M_HEADS, M_HEAD_DIM), lambda b, c: (b, 0, 0)),
                   pl.BlockSpec((1, M_HEADS, LANES), lambda b, c: (b, 0, 0))],
        out_shape=[jax.ShapeDtypeStruct((r, M_WIDTH), F32),
                   jax.ShapeDtypeStruct((BATCH, M_HEADS, M_HEAD_DIM, M_HEAD_DIM), F32),
                   jax.ShapeDtypeStruct((BATCH, M_HEADS, M_HEAD_DIM), F32),
                   jax.ShapeDtypeStruct((BATCH, M_HEADS, LANES), F32)],
        compiler_params=_params("arbitrary", "arbitrary"),
        name="mlstm_prompt",
    )(qkv, qkv, qkv, gates, g_col, g_row, b_if.reshape(1, -1), b_if.reshape(-1, 1), gain.reshape(1, -1))


def _mlstm_sample_kernel(q_ref, k_ref, v_ref, om_ref, gc_ref, gr_ref, bc_ref, br_ref, gain_ref,
                         c_in, n_in, m_in, hm_in, h_ref, c_ref, n_ref, m_ref, qp, kp, vp):
    del hm_in
    c_ref[...] = c_in[...]
    n_ref[...] = n_in[...]
    m_ref[...] = m_in[...]
    for src, dst in ((q_ref, qp), (k_ref, kp), (v_ref, vp)):
        dst[...] = jnp.zeros_like(dst)
        dst[0:DEC_SEQ, :] = src[...]
    gates = _mlstm_gates(gc_ref[0], gr_ref[0], bc_ref[...], br_ref[...], DEC_SEQ)
    om = om_ref[...]

    def h_out(hs, val):
        h_ref[:, hs] = val[0:DEC_SEQ, :]

    def om_of(hs):
        return jnp.concatenate([om[:, hs], jnp.zeros((SAMPLE_CHUNK - DEC_SEQ, M_HEAD_DIM), F32)], axis=0)

    _mlstm_heads(lambda hs: qp[:, hs].astype(BF16), lambda hs: (kp[:, hs] * K_SCALE).astype(BF16),
                 lambda hs: vp[:, hs].astype(BF16), om_of,
                 gates, gain_ref, c_ref, n_ref, m_ref, h_out)


def _mlstm_sample(qkv, gates, g_col, g_row, b_if, gain, st_c, st_n, st_m, hm):
    assert DEC_SEQ == SUBLANES
    n_prompt = _rows()[0]
    blk0 = n_prompt // DEC_SEQ
    spec = lambda col: pl.BlockSpec((DEC_SEQ, M_WIDTH), lambda b: (blk0 + b, col))
    full = lambda shape: pl.BlockSpec(shape, lambda b: (0,) * len(shape))
    st_spec = lambda shape: pl.BlockSpec((1,) + shape, lambda b: (b,) + (0,) * len(shape))
    st_specs = [st_spec((M_HEADS, M_HEAD_DIM, M_HEAD_DIM)), st_spec((M_HEADS, M_HEAD_DIM)),
                st_spec((M_HEADS, LANES))]
    return pl.pallas_call(
        _mlstm_sample_kernel,
        grid=(DEC_BATCH,),
        in_specs=[spec(0), spec(1), spec(2), spec(3),
                  st_spec((SAMPLE_CHUNK, 2 * M_HEADS)), st_spec((2 * M_HEADS, SAMPLE_CHUNK)),
                  full((1, 2 * M_HEADS)), full((2 * M_HEADS, 1)), full((1, M_WIDTH))] + st_specs
                 + [pl.BlockSpec(memory_space=pl.ANY)],
        out_specs=[spec(0)] + st_specs,
        out_shape=[jax.ShapeDtypeStruct(hm.shape, F32),
                   jax.ShapeDtypeStruct(st_c.shape, F32), jax.ShapeDtypeStruct(st_n.shape, F32),
                   jax.ShapeDtypeStruct((DEC_BATCH, M_HEADS, LANES), F32)],
        scratch_shapes=[pltpu.VMEM((SAMPLE_CHUNK, M_WIDTH), F32)] * 3,
        input_output_aliases={12: 0},
        compiler_params=_params("arbitrary"),
        name="mlstm_sample",
    )(qkv, qkv, qkv, gates, g_col, g_row, b_if.reshape(1, -1), b_if.reshape(-1, 1), gain.reshape(1, -1),
      st_c, st_n, jnp.broadcast_to(st_m[:, :, None], (DEC_BATCH, M_HEADS, LANES)), hm)


def _lambda(lam_ref, lam_init):
    p = lam_ref[...]
    s1 = jnp.sum(p[0:1, :] * p[1:2, :], axis=1, keepdims=True)
    s2 = jnp.sum(p[2:3, :] * p[3:4, :], axis=1, keepdims=True)
    return jnp.exp(s1) - jnp.exp(s2) + lam_init


def _diff_prompt_kernel(slope_ref, q_ref, k_ref, vt_ref, km_ref, vtm_ref, lam_ref, gain_ref, o_ref,
                        base_ref, m_ref, acc_ref, *, lam_init):
    b = pl.program_id(0)
    hg = pl.program_id(1)
    is_meta = pl.program_id(2) == 0
    t = pl.program_id(2) - 1
    tq = ROW_BLOCK
    heads = range(HEAD_GROUP)
    cols = [slice(i * LANES, (i + 1) * LANES) for i in heads]
    slopes = [slope_ref[hg * HEAD_GROUP + i] * LOG2E for i in heads]

    @pl.when(jnp.logical_not(is_meta & (b > 0)))
    def _():
        lo = lax.broadcasted_iota(jnp.int32, (tq, LANES), 1) < D_QK_DIM
        q2 = []
        for i in heads:
            q = q_ref[:, cols[i]].astype(BF16)
            zero = jnp.zeros_like(q)
            q2.append(jnp.concatenate([jnp.where(lo, q, zero), jnp.where(lo, zero, q)], axis=0))
            base_ref[i] = slopes[i] * lax.broadcasted_iota(jnp.int32, (tq, 2 * tq), 0).astype(F32)
        q_pos0 = jnp.where(is_meta, 0, N_META + t * tq)
        ones = jnp.ones((ONES_ROWS, tq), BF16)
        m_ref[...] = jnp.full_like(m_ref, NEG_INF)
        acc_ref[...] = jnp.zeros_like(acc_ref)

        def visible(n_keys, k_off):
            key = lax.broadcasted_iota(jnp.int32, (n_keys, 2 * tq), 0) + k_off
            qry = lax.broadcasted_iota(jnp.int32, (n_keys, 2 * tq), 1)
            return key <= jnp.where(qry >= tq, qry - tq, qry)

        def step(i, s, vt, k_off):
            c = slopes[i] * k_off.astype(F32)
            m_old = m_ref[i]
            m_new = jnp.maximum(m_old, jnp.max(s, axis=0, keepdims=True) + c)
            p = jnp.exp2(s - (m_new - c)).astype(BF16)
            lhs = jnp.concatenate([vt, ones[:, 0:vt.shape[1]]], axis=0)
            acc_ref[i] = jnp.exp2(m_old - m_new) * acc_ref[i] + _dot(lhs, p)
            m_ref[i] = m_new

        def scores(kb):
            ks = pl.ds(pl.multiple_of(kb * tq, tq), tq)
            return [_dot_nt(k_ref[ks, cols[i]], q2[i]) for i in heads]

        def full_blocks(kbs):
            ss = [scores(kb) for kb in kbs]
            for kb, s in zip(kbs, ss):
                for i in heads:
                    step(i, s[i] + base_ref[i], vt_ref[kb, cols[i], :], N_META + kb * tq - q_pos0)

        n_full = jnp.where(is_meta, 0, t)
        n_main = n_full // KV_UNROLL

        def unrolled(j, carry):
            full_blocks([j * KV_UNROLL + u for u in range(KV_UNROLL)])
            return carry

        def single(kb, carry):
            full_blocks([kb])
            return carry

        lax.fori_loop(0, n_main, unrolled, 0)
        lax.fori_loop(n_main * KV_UNROLL, n_full, single, 0)

        td = jnp.maximum(t, 0)
        hide = jnp.where(is_meta, 2 * tq, 0)
        s_meta = [_dot_nt(km_ref[:, cols[i]], q2[i]) for i in heads]
        s_diag = scores(td)
        for i in heads:
            s = s_meta[i] + base_ref[i, 0:N_META, :]
            step(i, jnp.where(visible(N_META, -q_pos0), s, NEG_INF), vtm_ref[0, cols[i], 0:N_META], -q_pos0)
        for i in heads:
            s = s_diag[i] + base_ref[i]
            step(i, jnp.where(visible(tq, hide), s, NEG_INF), vt_ref[td, cols[i], :], jnp.int32(0))

        lam = _lambda(lam_ref, lam_init)
        for i in heads:
            acc = acc_ref[i]
            o_t = acc[0:D_V_DIM, :] * (1.0 / acc[D_V_DIM:D_V_DIM + 1, :])
            o = (o_t[:, 0:tq] - lam * o_t[:, tq:2 * tq]).T
            o_ref[:, cols[i]] = _rms(o, gain_ref[:, cols[i]]) * (1.0 - lam_init)


def _diff_prompt(qn, kb, vt, lam_pack, gain, slopes, lam_init):
    r = qn.shape[0]
    meta_row = _rows()[2]
    nq = SEQ // ROW_BLOCK
    meta_blk = meta_row // ROW_BLOCK
    gw = HEAD_GROUP * LANES
    assert meta_row % N_META == 0 and D_V_DIM == LANES and D_HEADS % HEAD_GROUP == 0

    def qmap(b, h, qi, *_):
        return (jnp.where(qi == 0, jnp.where(b == 0, meta_blk, b * nq), b * nq + qi - 1), h)

    grid_spec = pltpu.PrefetchScalarGridSpec(
        num_scalar_prefetch=1,
        grid=(BATCH, D_HEADS // HEAD_GROUP, nq + 1),
        in_specs=[pl.BlockSpec((ROW_BLOCK, gw), qmap),
                  pl.BlockSpec((SEQ, gw), lambda b, h, qi, *_: (b, h)),
                  pl.BlockSpec((nq, gw, ROW_BLOCK), lambda b, h, qi, *_: (b, h, 0)),
                  pl.BlockSpec((N_META, gw), lambda b, h, qi, *_: (meta_row // N_META, h)),
                  pl.BlockSpec((1, gw, ROW_BLOCK), lambda b, h, qi, *_: (meta_blk, h, 0)),
                  pl.BlockSpec((SUBLANES, LANES), lambda b, h, qi, *_: (0, 0)),
                  pl.BlockSpec((1, gw), lambda b, h, qi, *_: (0, h))],
        out_specs=pl.BlockSpec((ROW_BLOCK, gw), qmap),
        scratch_shapes=[pltpu.VMEM((HEAD_GROUP, ROW_BLOCK, 2 * ROW_BLOCK), F32),
                        pltpu.VMEM((HEAD_GROUP, 1, 2 * ROW_BLOCK), F32),
                        pltpu.VMEM((HEAD_GROUP, D_V_DIM + ONES_ROWS, 2 * ROW_BLOCK), F32)])
    return pl.pallas_call(
        functools.partial(_diff_prompt_kernel, lam_init=lam_init),
        grid_spec=grid_spec,
        out_shape=jax.ShapeDtypeStruct((r, D_V_WIDTH), F32),
        compiler_params=_params("arbitrary", "arbitrary", "arbitrary"),
        name="diff_prompt",
    )(slopes, qn, kb, vt, kb, vt, lam_pack, gain.reshape(1, -1))


def _diff_sample_kernel(pt_ref, *refs, lam_init):
    del pt_ref
    pp = PAGES_PER_STEP
    k_pages, v_pages = refs[0:pp], refs[pp:2 * pp]
    (q_ref, kn_ref, vn_ref, lam_ref, gain_ref, hd_in, o_ref,
     qb_ref, bias_ref, kp_ref, vp_ref, m_ref, acc_ref) = refs[2 * pp:]
    del hd_in
    g = pl.program_id(1)
    rows = 2 * D_HEADS * DEC_SEQ
    half = D_HEADS * DEC_SEQ
    cols = PAGE_SIZE * D_HEADS

    row = lax.broadcasted_iota(jnp.int32, (rows, 1), 0)
    r_tok = row % DEC_SEQ
    r_head = (row // DEC_SEQ) % D_HEADS
    slope = jnp.exp2(-(r_head + 1).astype(F32) * (8.0 / D_HEADS)) * LOG2E

    def step(s, v, c):
        m_old = m_ref[...]
        m_new = jnp.maximum(m_old, jnp.max(s, axis=1, keepdims=True) + c)
        p = jnp.exp2(s - (m_new - c)).astype(BF16)
        v_ones = jnp.concatenate([v, jnp.ones_like(v)], axis=1)
        acc_ref[...] = jnp.exp2(m_old - m_new) * acc_ref[...] + _dot(p, v_ones)
        m_ref[...] = m_new

    @pl.when(g == 0)
    def _():
        lo = lax.broadcasted_iota(jnp.int32, (DEC_SEQ, LANES), 1) < D_QK_DIM
        q = q_ref[...]
        heads = [q[:, h * LANES:(h + 1) * LANES] for h in range(D_HEADS)]
        qb_ref[...] = jnp.concatenate([jnp.where(lo, x, 0.0) for x in heads]
                                      + [jnp.where(lo, 0.0, x) for x in heads], axis=0).astype(BF16)
        col = lax.broadcasted_iota(jnp.int32, (rows, cols), 1)
        bias_ref[...] = jnp.where(col % D_HEADS == r_head, slope * (col // D_HEADS).astype(F32), NEG_INF)
        m_ref[...] = jnp.full_like(m_ref, NEG_INF)
        acc_ref[...] = jnp.zeros_like(acc_ref)

    qb = qb_ref[...]
    for p in range(pp):
        page_pos0 = (g * pp + p) * PAGE_SIZE - PAST_LEN
        k2 = k_pages[p][...].reshape(cols, LANES).astype(BF16)
        v2 = v_pages[p][...].reshape(cols, LANES).astype(BF16)
        step(_dot_nt(qb, k2) + bias_ref[...], v2, slope * page_pos0.astype(F32))

    @pl.when(g == pl.num_programs(1) - 1)
    def _():
        for src, dst in ((kn_ref, kp_ref), (vn_ref, vp_ref)):
            dst[...] = jnp.zeros_like(dst)
            dst[0:half, :] = src[...].reshape(half, LANES)
        c = lax.broadcasted_iota(jnp.int32, (1, LANES), 1)
        c_tok = c // D_HEADS
        ok = (c % D_HEADS == r_head) & (c_tok <= r_tok)
        s = _dot_nt(qb, kp_ref[...].astype(BF16)) + slope * c_tok.astype(F32)
        step(jnp.where(ok, s, NEG_INF), vp_ref[...].astype(BF16), jnp.zeros((rows, 1), F32))

        lam = _lambda(lam_ref, lam_init)
        acc = acc_ref[...]
        o = acc[:, 0:LANES] * (1.0 / acc[:, LANES:2 * LANES])
        for h in range(D_HEADS):
            cs = slice(h * LANES, (h + 1) * LANES)
            oh = o[h * DEC_SEQ:(h + 1) * DEC_SEQ, :] - lam * o[half + h * DEC_SEQ:half + (h + 1) * DEC_SEQ, :]
            o_ref[:, cs] = _rms(oh, gain_ref[:, cs]) * (1.0 - lam_init)


def _diff_sample(layer, cache_k, cache_v, page_table, qn, kf, vf, lam_pack, gain, hd, lam_init):
    assert DEC_SEQ == SUBLANES and D_HEADS == SUBLANES and D_V_DIM == LANES
    assert D_HEADS * DEC_SEQ <= LANES and PAST_LEN % (PAGE_SIZE * PAGES_PER_STEP) == 0
    n_prompt = _rows()[0]
    blk0 = n_prompt // DEC_SEQ
    pp = PAGES_PER_STEP
    width = D_HEADS * LANES
    rows = 2 * D_HEADS * DEC_SEQ

    def page_spec(p):
        return pl.BlockSpec((None, None, PAGE_SIZE, D_HEADS, LANES),
                            lambda b, g, pt: (layer, pt[b, g * pp + p], 0, 0, 0))

    row_spec = pl.BlockSpec((DEC_SEQ, width), lambda b, g, pt: (blk0 + b, 0))
    new_spec = pl.BlockSpec((DEC_SEQ, D_HEADS, LANES), lambda b, g, pt: (blk0 + b, 0, 0))
    grid_spec = pltpu.PrefetchScalarGridSpec(
        num_scalar_prefetch=1,
        grid=(DEC_BATCH, PAST_LEN // (PAGE_SIZE * pp)),
        in_specs=[page_spec(p) for p in range(pp)] * 2
                 + [row_spec, new_spec, new_spec,
                    pl.BlockSpec((SUBLANES, LANES), lambda b, g, pt: (0, 0)),
                    pl.BlockSpec((1, width), lambda b, g, pt: (0, 0)),
                    pl.BlockSpec(memory_space=pl.ANY)],
        out_specs=row_spec,
        scratch_shapes=[pltpu.VMEM((rows, LANES), BF16), pltpu.VMEM((rows, PAGE_SIZE * D_HEADS), F32),
                        pltpu.VMEM((LANES, LANES), F32), pltpu.VMEM((LANES, LANES), F32),
                        pltpu.VMEM((rows, 1), F32), pltpu.VMEM((rows, 2 * LANES), F32)])
    n_in = 2 * pp + 6
    return pl.pallas_call(
        functools.partial(_diff_sample_kernel, lam_init=lam_init),
        grid_spec=grid_spec,
        out_shape=jax.ShapeDtypeStruct(hd.shape, F32),
        input_output_aliases={n_in: 0},
        compiler_params=_params("arbitrary", "arbitrary"),
        name="diff_sample",
    )(page_table, *([cache_k] * pp), *([cache_v] * pp), qn, kf, vf, lam_pack, gain.reshape(1, -1), hd)


def _split_w_in(l, w_in):
    g0 = 4 * M_WIDTH
    g1 = g0 + 2 * M_HEADS
    w_gif = jnp.pad(w_in[l, :, g0:g1], ((0, 0), (0, LANES - 2 * M_HEADS))).astype(BF16)
    return w_gif, w_in[l, :, g1:].astype(BF16)


def kernel(x_prompt, x_sample, cache_k, cache_v, state_C, state_n, state_m, page_table, meta_tokens, norm_mix, w_in, b_if, q_gain, k_gain, lambda_q1, lambda_k1, lambda_q2, lambda_k2, mlstm_norm, diff_norm, w_pm, w_pd, w_out, norm_ffn, w_gu, w_down):
    assert M_HEAD_DIM ** -0.5 == 2.0 ** round(math.log2(M_HEAD_DIM ** -0.5))
    n_prompt, n_sample, meta_row, r = _rows()
    x = jnp.concatenate([x_prompt.reshape(n_prompt, D_MODEL), x_sample.reshape(n_sample, D_MODEL),
                         meta_tokens.astype(F32), jnp.zeros((r - meta_row - N_META, D_MODEL), F32)], axis=0)
    slopes = 2.0 ** (-8.0 * jnp.arange(1, D_HEADS + 1, dtype=F32) / D_HEADS)

    def with_meta(rows_arr, tail):
        meta = jnp.broadcast_to(rows_arr[meta_row:meta_row + N_META][None], (BATCH, N_META) + tail)
        return jnp.concatenate([meta, rows_arr[:n_prompt].reshape((BATCH, SEQ) + tail)], axis=1)

    outs = [[] for _ in range(10)]
    for l in range(DEPTH):
        lam_init = 0.8 - 0.6 * math.exp(-0.3 * l)
        w_gif, w_tail = _split_w_in(l, w_in)
        xn = _rmsnorm(x, norm_mix[l], "norm_mix")
        proj_m = _matmul(xn, w_in, l, 4 * M_WIDTH, F32, "proj_mlstm")
        graw = _matmul(xn, w_gif, l, LANES, F32, "proj_gates")[:, :2 * M_HEADS]
        proj_d = _matmul(xn, w_tail, l, w_tail.shape[1], F32, "proj_diff",
                         tn=_col_tile(math.gcd(D_QK_WIDTH, D_MODEL)))

        hm, c_p, n_p, m_p = _mlstm_prompt(proj_m, proj_m, graw, graw.T, b_if[l], mlstm_norm[l])
        gs = graw[n_prompt:n_prompt + n_sample].reshape(DEC_BATCH, DEC_SEQ, 2 * M_HEADS)
        gs = jnp.pad(gs, ((0, 0), (0, SAMPLE_CHUNK - DEC_SEQ), (0, 0)))
        hm, c_s, n_s, m_s = _mlstm_sample(proj_m, proj_m, gs, gs.transpose(0, 2, 1), b_if[l], mlstm_norm[l],
                                          state_C[l].astype(F32), state_n[l].astype(F32), state_m[l].astype(F32), hm)

        qn, kf, vf, kb, vt = _qk_norm(proj_d, q_gain[l], k_gain[l])
        lam_pack = jnp.pad(jnp.stack([lambda_q1[l], lambda_k1[l], lambda_q2[l], lambda_k2[l]]).astype(F32),
                           ((0, SUBLANES - 4), (0, LANES - D_QK_DIM)))
        hd = _diff_prompt(qn, kb, vt, lam_pack, diff_norm[l], slopes, lam_init)
        hd = _diff_sample(l, cache_k, cache_v, page_table, qn, kf, vf, lam_pack, diff_norm[l], hd, lam_init)

        z = _merge(hm, hd, w_pm, w_pd, l, proj_d, 2 * D_QK_WIDTH + D_V_WIDTH)
        x = _matmul_res(z, w_out, l, x, "out_proj", MATMUL_ROWS)
        act = _swiglu(_rmsnorm(x, norm_ffn[l], "norm_ffn"), w_gu, l)
        x = _matmul_res(act, w_down, l, x, "ffn_down", MATMUL_ROWS // 2)

        head_shape = (D_HEADS, D_V_DIM)
        vals = (with_meta(kf, head_shape), with_meta(vf, head_shape), c_p, n_p, m_p[:, :, 0],
                kf[n_prompt:n_prompt + n_sample].reshape(DEC_BATCH, DEC_SEQ, *head_shape),
                vf[n_prompt:n_prompt + n_sample].reshape(DEC_BATCH, DEC_SEQ, *head_shape),
                c_s, n_s, m_s[:, :, 0])
        for o, v in zip(outs, vals):
            o.append(v)

    y_prompt = x[:n_prompt].reshape(BATCH, SEQ, D_MODEL)
    y_sample = x[n_prompt:n_prompt + n_sample].reshape(DEC_BATCH, DEC_SEQ, D_MODEL)
    return (y_prompt, y_sample) + tuple(jnp.stack(o) for o in outs)
```

```python
import functools
import math

import jax
import jax.numpy as jnp
from jax import lax
from jax.experimental import pallas as pl
from jax.experimental.pallas import tpu as pltpu

D_MODEL = 2048
BATCH = 2
SEQ = 4096
DEPTH = 2
DEC_BATCH = 32
DEC_SEQ = 8
PAST_LEN = 8192
PAGE_SIZE = 128
N_META = 16
M_HEADS = 4
M_HEAD_DIM = 256
M_WIDTH = M_HEADS * M_HEAD_DIM
D_HEADS = 8
D_QK_DIM = 64
D_V_DIM = 2 * D_QK_DIM
D_QK_WIDTH = D_HEADS * 2 * D_QK_DIM
D_V_WIDTH = D_HEADS * D_V_DIM
EPS = 1e-6

LANES = 128
SUBLANES = 8
ROW_BLOCK = 256
MATMUL_ROWS = 1100
K_SCALE = M_HEAD_DIM ** -0.5
SAMPLE_CHUNK = 128
KV_UNROLL = 1
HEAD_GROUP = 4
PAGES_PER_STEP = 8
VMEM_LIMIT = 56 * 1024 * 1024

F32 = jnp.float32
BF16 = jnp.bfloat16
NEG_INF = float("-inf")
LOG2E = math.log2(math.e)
Q_SCALE = D_QK_DIM ** -0.5 * LOG2E
ONES_ROWS = 16


def _ffn_dim():
    return ((8 * D_MODEL // 3 + 255) // 256) * 256


def _rows():
    n_prompt = BATCH * SEQ
    n_sample = DEC_BATCH * DEC_SEQ
    assert n_prompt % ROW_BLOCK == 0 and n_sample % ROW_BLOCK == 0 and N_META <= ROW_BLOCK
    return n_prompt, n_sample, n_prompt + n_sample, n_prompt + n_sample + ROW_BLOCK


def _row_tile(r):
    return 512 if r % 512 == 0 else ROW_BLOCK


def _tile(n, cap, mult=16):
    for t in range(min(cap, n) // mult * mult, 0, -mult):
        if n % t == 0:
            return t
    raise ValueError((n, cap, mult))


def _col_tile(n):
    for t in (1024, 512, 256, 128):
        if n % t == 0:
            return t
    raise ValueError(n)


def _params(*sem):
    return pltpu.CompilerParams(dimension_semantics=sem, vmem_limit_bytes=VMEM_LIMIT)


def _dot(a, b):
    return jnp.dot(a, b, preferred_element_type=F32)


def _dot_nt(a, b):
    return lax.dot_general(a, b, (((1,), (1,)), ((), ())), preferred_element_type=F32)


def _dot_tn(a, b):
    return lax.dot_general(a, b, (((0,), (0,)), ((), ())), preferred_element_type=F32)


def _sigmoid(x):
    return 1.0 / (1.0 + jnp.exp(-x))


def _rms(x, gain):
    ms = jnp.mean(x * x, axis=-1, keepdims=True)
    return x * lax.rsqrt(ms + EPS) * gain


def _rmsnorm_kernel(x_ref, g_ref, o_ref):
    o_ref[...] = _rms(x_ref[...], g_ref[...]).astype(o_ref.dtype)


def _rmsnorm(x, gain, name):
    r, d = x.shape
    tm = _tile(r, MATMUL_ROWS)
    return pl.pallas_call(
        _rmsnorm_kernel,
        grid=(r // tm,),
        in_specs=[pl.BlockSpec((tm, d), lambda i: (i, 0)), pl.BlockSpec((1, d), lambda i: (0, 0))],
        out_specs=pl.BlockSpec((tm, d), lambda i: (i, 0)),
        out_shape=jax.ShapeDtypeStruct((r, d), BF16),
        compiler_params=_params("parallel"),
        name=name,
    )(x, gain.reshape(1, d))


def _weight_spec(w, layer, tn, col0):
    if w.ndim == 3:
        return pl.BlockSpec((None, w.shape[1], tn), lambda j, i: (layer, 0, col0 + j))
    return pl.BlockSpec((w.shape[0], tn), lambda j, i: (0, col0 + j))


def _cast_weights(w_refs, wb_refs):
    @pl.when(pl.program_id(1) == 0)
    def _():
        for w_ref, wb_ref in zip(w_refs, wb_refs):
            wb_ref[...] = w_ref[...].astype(BF16)


def _matmul_kernel(a_ref, w_ref, o_ref, wb_ref):
    _cast_weights([w_ref], [wb_ref])
    o_ref[...] = _dot(a_ref[...], wb_ref[...]).astype(o_ref.dtype)


def _matmul(a, w, layer, n, out_dtype, name, tn=None):
    r, k = a.shape
    tm, tn = _tile(r, MATMUL_ROWS), tn or _col_tile(n)
    return pl.pallas_call(
        _matmul_kernel,
        grid=(n // tn, r // tm),
        in_specs=[pl.BlockSpec((tm, k), lambda j, i: (i, 0)), _weight_spec(w, layer, tn, 0)],
        out_specs=pl.BlockSpec((tm, tn), lambda j, i: (i, j)),
        out_shape=jax.ShapeDtypeStruct((r, n), out_dtype),
        scratch_shapes=[pltpu.VMEM((k, tn), BF16)],
        compiler_params=_params("parallel", "arbitrary"),
        name=name,
    )(a, w)


def _matmul_nt_kernel(a_ref, w_ref, o_ref, wb_ref):
    _cast_weights([w_ref.at[0]], [wb_ref])
    o_ref[...] = _dot_nt(a_ref[...], wb_ref[...]).astype(o_ref.dtype)


def _matmul_nt(a, w_t, layer, row0, n, name):
    r, k = a.shape
    tm, tn = _tile(r, MATMUL_ROWS), _tile(n, 1024, SUBLANES)
    assert row0 % SUBLANES == 0
    return pl.pallas_call(
        _matmul_nt_kernel,
        grid=(n // tn, r // tm),
        in_specs=[pl.BlockSpec((tm, k), lambda j, i: (i, 0)),
                  pl.BlockSpec((pl.Element(1), pl.Element(tn), pl.Element(k)),
                               lambda j, i: (layer, pl.multiple_of(row0 + j * tn, SUBLANES), 0))],
        out_specs=pl.BlockSpec((tm, tn), lambda j, i: (i, j)),
        out_shape=jax.ShapeDtypeStruct((r, n), F32),
        scratch_shapes=[pltpu.VMEM((tn, k), BF16)],
        compiler_params=_params("parallel", "arbitrary"),
        name=name,
    )(a, w_t)


def _swiglu_kernel(a_ref, wg_ref, wu_ref, o_ref, wgb_ref, wub_ref):
    _cast_weights([wg_ref, wu_ref], [wgb_ref, wub_ref])
    a = a_ref[...]
    g = _dot(a, wgb_ref[...])
    u = _dot(a, wub_ref[...])
    o_ref[...] = (g * _sigmoid(g) * u).astype(o_ref.dtype)


def _swiglu(a, w_gu, layer):
    r, k = a.shape
    f = w_gu.shape[2] // 2
    tm, tn = _tile(r, MATMUL_ROWS), min(512, _col_tile(f))
    nj = f // tn
    return pl.pallas_call(
        _swiglu_kernel,
        grid=(nj, r // tm),
        in_specs=[pl.BlockSpec((tm, k), lambda j, i: (i, 0)),
                  _weight_spec(w_gu, layer, tn, 0), _weight_spec(w_gu, layer, tn, nj)],
        out_specs=pl.BlockSpec((tm, tn), lambda j, i: (i, j)),
        out_shape=jax.ShapeDtypeStruct((r, f), BF16),
        scratch_shapes=[pltpu.VMEM((k, tn), BF16)] * 2,
        compiler_params=_params("parallel", "arbitrary"),
        name="ffn_up",
    )(a, w_gu, w_gu)


def _matmul_res_kernel(a_ref, w_ref, r_ref, o_ref, wb_ref):
    _cast_weights([w_ref], [wb_ref])
    o_ref[...] = r_ref[...] + _dot(a_ref[...], wb_ref[...])


def _matmul_res(a, w, layer, res, name, rows):
    r, k = a.shape
    n = w.shape[2]
    tm, tn = _tile(r, rows), min(512, _col_tile(n))
    return pl.pallas_call(
        _matmul_res_kernel,
        grid=(n // tn, r // tm),
        in_specs=[pl.BlockSpec((tm, k), lambda j, i: (i, 0)), _weight_spec(w, layer, tn, 0),
                  pl.BlockSpec((tm, tn), lambda j, i: (i, j))],
        out_specs=pl.BlockSpec((tm, tn), lambda j, i: (i, j)),
        out_shape=jax.ShapeDtypeStruct((r, n), F32),
        scratch_shapes=[pltpu.VMEM((k, tn), BF16)],
        compiler_params=_params("parallel", "arbitrary"),
        name=name,
    )(a, w, res)


def _merge_kernel(hm_ref, hd_ref, wpm_ref, wpd_ref, gm_ref, gd_ref, o_ref, wpmb_ref, wpdb_ref):
    _cast_weights([wpm_ref, wpd_ref], [wpmb_ref, wpdb_ref])
    zm = _dot(hm_ref[...].astype(BF16), wpmb_ref[...])
    zd = _dot(hd_ref[...].astype(BF16), wpdb_ref[...])
    o_ref[...] = (_sigmoid(gm_ref[...]) * zm + _sigmoid(gd_ref[...]) * zd).astype(o_ref.dtype)


def _merge(hm, hd, w_pm, w_pd, layer, gates, gate_col):
    r = hm.shape[0]
    n = w_pm.shape[2]
    tm, tn = _tile(r, MATMUL_ROWS // 2), _col_tile(math.gcd(n, gate_col))
    gm0, gd0 = gate_col // tn, (gate_col + n) // tn
    return pl.pallas_call(
        _merge_kernel,
        grid=(n // tn, r // tm),
        in_specs=[pl.BlockSpec((tm, M_WIDTH), lambda j, i: (i, 0)),
                  pl.BlockSpec((tm, D_V_WIDTH), lambda j, i: (i, 0)),
                  _weight_spec(w_pm, layer, tn, 0), _weight_spec(w_pd, layer, tn, 0),
                  pl.BlockSpec((tm, tn), lambda j, i: (i, j + gm0)),
                  pl.BlockSpec((tm, tn), lambda j, i: (i, j + gd0))],
        out_specs=pl.BlockSpec((tm, tn), lambda j, i: (i, j)),
        out_shape=jax.ShapeDtypeStruct((r, n), BF16),
        scratch_shapes=[pltpu.VMEM((M_WIDTH, tn), BF16), pltpu.VMEM((D_V_WIDTH, tn), BF16)],
        compiler_params=_params("parallel", "arbitrary"),
        name="merge",
    )(hm, hd, w_pm, w_pd, gates, gates)


def _half_norm(x, gain):
    lo = lax.broadcasted_iota(jnp.int32, x.shape, 1) < D_QK_DIM
    x2 = x * x
    s_lo = jnp.sum(jnp.where(lo, x2, 0.0), axis=1, keepdims=True)
    s_hi = jnp.sum(jnp.where(lo, 0.0, x2), axis=1, keepdims=True)
    ms = jnp.where(lo, s_lo, s_hi) * (1.0 / D_QK_DIM)
    return x * lax.rsqrt(ms + EPS) * gain


def _qk_norm_kernel(q_ref, k_ref, v_ref, qg_ref, kg_ref, *refs, n_prompt_tiles):
    qo_ref, kb_ref, vt_ref, kp_ref, vp_ref, kt_ref, vt8_ref = refs[-7:]
    i = pl.program_id(0)
    kn = []
    for h in range(D_HEADS):
        sl = slice(h * LANES, (h + 1) * LANES)
        qo_ref[:, sl] = _half_norm(q_ref[:, sl], qg_ref[...]) * Q_SCALE
        kn.append(_half_norm(k_ref[:, sl], kg_ref[...]))
        kb_ref[:, sl] = kn[h].astype(BF16)
    for j in range(vt_ref.shape[0]):
        vt_ref[j] = v_ref[j * ROW_BLOCK:(j + 1) * ROW_BLOCK, :].T.astype(BF16)

    @pl.when(i < n_prompt_tiles)
    def _():
        for h in range(D_HEADS):
            kp_ref[0, 0, :, h, :] = kn[h]
            vp_ref[0, 0, :, h, :] = v_ref[:, h * LANES:(h + 1) * LANES]

    @pl.when(i == n_prompt_tiles)
    def _():
        for h in range(D_HEADS):
            kt_ref[:, h, :] = kn[h]
            vt8_ref[:, h, :] = v_ref[:, h * LANES:(h + 1) * LANES]


def _qk_norm(qkv, q_gain, k_gain, layer, k_out, v_out):
    r = qkv.shape[0]
    w = D_QK_WIDTH
    n_prompt = _rows()[0]
    tm = r - n_prompt
    assert n_prompt % tm == 0 and SEQ % tm == 0 and tm % ROW_BLOCK == 0
    n_prompt_tiles, tiles_per_seq = n_prompt // tm, SEQ // tm
    spec = lambda c: pl.BlockSpec((tm, w), lambda i: (i, c))
    gspec = pl.BlockSpec((1, LANES), lambda i: (0, 0))

    def prompt_map(i):
        t = jnp.minimum(i, n_prompt_tiles - 1)
        return (layer, t // tiles_per_seq, N_META + (t % tiles_per_seq) * tm, 0, 0)

    pspec = pl.BlockSpec(tuple(pl.Element(s) for s in (1, 1, tm, D_HEADS, LANES)), prompt_map)
    tspec = pl.BlockSpec((tm, D_HEADS, LANES), lambda i: (0, 0, 0))
    out5 = jax.ShapeDtypeStruct((DEPTH, BATCH, N_META + SEQ, D_HEADS, LANES), F32)
    tail = jax.ShapeDtypeStruct((tm, D_HEADS, LANES), F32)
    carried = [] if k_out is None else [k_out, v_out]
    n_in = 5 + len(carried)
    return pl.pallas_call(
        functools.partial(_qk_norm_kernel, n_prompt_tiles=n_prompt_tiles),
        grid=(r // tm,),
        in_specs=[spec(0), spec(1), spec(2), gspec, gspec] + [pl.BlockSpec(memory_space=pl.ANY)] * len(carried),
        out_specs=[spec(0), spec(0), pl.BlockSpec((tm // ROW_BLOCK, D_V_WIDTH, ROW_BLOCK), lambda i: (i, 0, 0)),
                   pspec, pspec, tspec, tspec],
        out_shape=[jax.ShapeDtypeStruct((r, w), F32), jax.ShapeDtypeStruct((r, w), BF16),
                   jax.ShapeDtypeStruct((r // ROW_BLOCK, D_V_WIDTH, ROW_BLOCK), BF16), out5, out5, tail, tail],
        input_output_aliases={} if k_out is None else {n_in - 2: 3, n_in - 1: 4},
        compiler_params=_params("arbitrary"),
        name="qk_norm",
    )(qkv, qkv, qkv, jnp.tile(q_gain, 2).reshape(1, LANES), jnp.tile(k_gain, 2).reshape(1, LANES), *carried)


def _split3(x):
    hi = x.astype(BF16)
    r1 = x - hi.astype(F32)
    mid = r1.astype(BF16)
    lo = (r1 - mid.astype(F32)).astype(BF16)
    return hi, mid, lo


def _log_sigmoid(x):
    return jnp.minimum(x, 0.0) - jnp.log1p(jnp.exp(-jnp.abs(x)))


def _mlstm_gates(graw_col, graw_row, bias_col, bias_row, n_valid):
    l = graw_col.shape[0]
    g_col = graw_col + bias_col
    g_row = graw_row + bias_row
    is_f_col = lax.broadcasted_iota(jnp.int32, g_col.shape, 1) >= M_HEADS
    is_f_row = lax.broadcasted_iota(jnp.int32, g_row.shape, 0) >= M_HEADS
    ok_col = lax.broadcasted_iota(jnp.int32, g_col.shape, 0) < n_valid
    ok_row = lax.broadcasted_iota(jnp.int32, g_row.shape, 1) < n_valid
    lf_col = jnp.where(is_f_col & ok_col, _log_sigmoid(g_col), 0.0)
    lf_row = jnp.where(is_f_row & ok_row, _log_sigmoid(g_row), 0.0)
    ig_col = jnp.where(ok_col, g_col, NEG_INF)
    ig_row = jnp.where(ok_row, g_row, NEG_INF)
    rr = lax.broadcasted_iota(jnp.int32, (l, l), 0)
    cc = lax.broadcasted_iota(jnp.int32, (l, l), 1)
    tri_l = (cc <= rr).astype(BF16)
    tri_u = (rr <= cc).astype(BF16)
    b_col = sum(_dot(tri_l, p) for p in _split3(lf_col))
    b_row = sum(_dot(p, tri_u) for p in _split3(lf_row))
    return ig_col, b_col, ig_row, b_row, cc <= rr


def _mlstm_chunk(q, k, v, ig_col, b_col, ig_row, b_row, causal, c_st, n_st, m_st):
    l = q.shape[0]
    d = b_col - b_row + ig_row
    d = jnp.where(causal, d, NEG_INF)
    inter = b_col + m_st
    m_t = jnp.maximum(inter, jnp.max(d, axis=1, keepdims=True))
    w = jnp.exp(d - m_t)
    s = _dot_nt(q, k) * w
    a = jnp.exp(inter - m_t)
    num = _dot(s.astype(BF16), v) + a * _dot_nt(q, c_st.astype(BF16))
    den = jnp.sum(s, axis=1, keepdims=True) + a * jnp.sum(q.astype(F32) * n_st, axis=1, keepdims=True)
    h = num * (1.0 / jnp.maximum(jnp.abs(den), jnp.exp(-m_t)))

    b_last = b_col[l - 1:l, :]
    g_col = b_last - b_col + ig_col
    g_row = b_last - b_row + ig_row
    m_new = jnp.maximum(b_last + m_st, jnp.max(g_row, axis=1, keepdims=True))
    ws_col = jnp.exp(g_col - m_new)
    ws_row = jnp.exp(g_row - m_new)
    decay = jnp.exp(b_last + m_st - m_new)
    wv = (v.astype(F32) * ws_col).astype(BF16)
    c_new = decay * c_st + _dot_tn(wv, k)
    n_new = decay * n_st + _dot(ws_row.astype(BF16), k)
    return h, c_new, n_new, m_new


def _mlstm_heads(q_of, k_of, v_of, om_of, gates, gain_ref, c_ref, n_ref, m_ref, h_out):
    ig_col, b_col, ig_row, b_row, causal = gates
    for h in range(M_HEADS):
        f = M_HEADS + h
        hs = slice(h * M_HEAD_DIM, (h + 1) * M_HEAD_DIM)
        out, c_new, n_new, m_new = _mlstm_chunk(
            q_of(hs), k_of(hs), v_of(hs),
            ig_col[:, h:h + 1], b_col[:, f:f + 1], ig_row[h:h + 1, :], b_row[f:f + 1, :], causal,
            c_ref[0, h], n_ref[0, h:h + 1, :], m_ref[0, h:h + 1, 0:1])
        c_ref[0, h] = c_new
        n_ref[0, h:h + 1, :] = n_new
        m_ref[0, h:h + 1, :] = jnp.broadcast_to(m_new, (1, LANES))
        h_out(hs, _rms(out, gain_ref[:, hs]) * _sigmoid(om_of(hs)))


def _mlstm_prompt_kernel(q_ref, k_ref, v_ref, om_ref, gc_ref, gr_ref, bc_ref, br_ref, gain_ref,
                         h_ref, c_ref, n_ref, m_ref):
    c = pl.program_id(1)

    @pl.when(c == 0)
    def _():
        c_ref[...] = jnp.zeros_like(c_ref)
        n_ref[...] = jnp.zeros_like(n_ref)
        m_ref[...] = jnp.zeros_like(m_ref)

    n_valid = jnp.where(c == 0, N_META, ROW_BLOCK)
    gates = _mlstm_gates(gc_ref[...], gr_ref[...], bc_ref[...], br_ref[...], n_valid)

    def h_out(hs, val):
        h_ref[:, hs] = val

    _mlstm_heads(lambda hs: q_ref[:, hs].astype(BF16), lambda hs: (k_ref[:, hs] * K_SCALE).astype(BF16),
                 lambda hs: v_ref[:, hs].astype(BF16), lambda hs: om_ref[:, hs],
                 gates, gain_ref, c_ref, n_ref, m_ref, h_out)


def _mlstm_prompt(qkv, gates, g_col, g_row, b_if, gain):
    r = qkv.shape[0]
    n_prompt, _, meta_row, _ = _rows()
    nc = SEQ // ROW_BLOCK
    meta_blk = meta_row // ROW_BLOCK

    def rb(b, c):
        return jnp.where(c == 0, meta_blk, b * nc + c - 1)

    def rb_out(b, c):
        return jnp.where((c == 0) & (b > 0), b * nc, rb(b, c))

    spec = lambda col: pl.BlockSpec((ROW_BLOCK, M_WIDTH), lambda b, c: (rb(b, c), col))
    full = lambda shape: pl.BlockSpec(shape, lambda b, c: (0,) * len(shape))
    return pl.pallas_call(
        _mlstm_prompt_kernel,
        grid=(BATCH, nc + 1),
        in_specs=[spec(0), spec(1), spec(2), spec(3),
                  pl.BlockSpec((ROW_BLOCK, 2 * M_HEADS), lambda b, c: (rb(b, c), 0)),
                  pl.BlockSpec((2 * M_HEADS, ROW_BLOCK), lambda b, c: (0, rb(b, c))),
                  full((1, 2 * M_HEADS)), full((2 * M_HEADS, 1)), full((1, M_WIDTH))],
        out_specs=[pl.BlockSpec((ROW_BLOCK, M_WIDTH), lambda b, c: (rb_out(b, c), 0)),
                   pl.BlockSpec((1, M_HEADS, M_HEAD_DIM, M_HEAD_DIM), lambda b, c: (b, 0, 0, 0)),
                   pl.BlockSpec((1, M_HEADS, M_HEAD_DIM), lambda b, c: (b, 0, 0)),
                   pl.BlockSpec((1, M_HEADS, LANES), lambda b, c: (b, 0, 0))],
        out_shape=[jax.ShapeDtypeStruct((r, M_WIDTH), F32),
                   jax.ShapeDtypeStruct((BATCH, M_HEADS, M_HEAD_DIM, M_HEAD_DIM), F32),
                   jax.ShapeDtypeStruct((BATCH, M_HEADS, M_HEAD_DIM), F32),
                   jax.ShapeDtypeStruct((BATCH, M_HEADS, LANES), F32)],
        compiler_params=_params("arbitrary", "arbitrary"),
        name="mlstm_prompt",
    )(qkv, qkv, qkv, gates, g_col, g_row, b_if.reshape(1, -1), b_if.reshape(-1, 1), gain.reshape(1, -1))


def _mlstm_sample_kernel(q_ref, k_ref, v_ref, om_ref, gc_ref, gr_ref, bc_ref, br_ref, gain_ref,
                         c_in, n_in, m_in, hm_in, h_ref, c_ref, n_ref, m_ref, qp, kp, vp):
    del hm_in
    c_ref[...] = c_in[...]
    n_ref[...] = n_in[...]
    m_ref[...] = m_in[...]
    for src, dst in ((q_ref, qp), (k_ref, kp), (v_ref, vp)):
        dst[...] = jnp.zeros_like(dst)
        dst[0:DEC_SEQ, :] = src[...]
    gates = _mlstm_gates(gc_ref[0], gr_ref[0], bc_ref[...], br_ref[...], DEC_SEQ)
    om = om_ref[...]

    def h_out(hs, val):
        h_ref[:, hs] = val[0:DEC_SEQ, :]

    def om_of(hs):
        return jnp.concatenate([om[:, hs], jnp.zeros((SAMPLE_CHUNK - DEC_SEQ, M_HEAD_DIM), F32)], axis=0)

    _mlstm_heads(lambda hs: qp[:, hs].astype(BF16), lambda hs: (kp[:, hs] * K_SCALE).astype(BF16),
                 lambda hs: vp[:, hs].astype(BF16), om_of,
                 gates, gain_ref, c_ref, n_ref, m_ref, h_out)


def _mlstm_sample(qkv, gates, g_col, g_row, b_if, gain, st_c, st_n, st_m, hm):
    assert DEC_SEQ == SUBLANES
    n_prompt = _rows()[0]
    blk0 = n_prompt // DEC_SEQ
    spec = lambda col: pl.BlockSpec((DEC_SEQ, M_WIDTH), lambda b: (blk0 + b, col))
    full = lambda shape: pl.BlockSpec(shape, lambda b: (0,) * len(shape))
    st_spec = lambda shape: pl.BlockSpec((1,) + shape, lambda b: (b,) + (0,) * len(shape))
    st_specs = [st_spec((M_HEADS, M_HEAD_DIM, M_HEAD_DIM)), st_spec((M_HEADS, M_HEAD_DIM)),
                st_spec((M_HEADS, LANES))]
    return pl.pallas_call(
        _mlstm_sample_kernel,
        grid=(DEC_BATCH,),
        in_specs=[spec(0), spec(1), spec(2), spec(3),
                  st_spec((SAMPLE_CHUNK, 2 * M_HEADS)), st_spec((2 * M_HEADS, SAMPLE_CHUNK)),
                  full((1, 2 * M_HEADS)), full((2 * M_HEADS, 1)), full((1, M_WIDTH))] + st_specs
                 + [pl.BlockSpec(memory_space=pl.ANY)],
        out_specs=[spec(0)] + st_specs,
        out_shape=[jax.ShapeDtypeStruct(hm.shape, F32),
                   jax.ShapeDtypeStruct(st_c.shape, F32), jax.ShapeDtypeStruct(st_n.shape, F32),
                   jax.ShapeDtypeStruct((DEC_BATCH, M_HEADS, LANES), F32)],
        scratch_shapes=[pltpu.VMEM((SAMPLE_CHUNK, M_WIDTH), F32)] * 3,
        input_output_aliases={12: 0},
        compiler_params=_params("arbitrary"),
        name="mlstm_sample",
    )(qkv, qkv, qkv, gates, g_col, g_row, b_if.reshape(1, -1), b_if.reshape(-1, 1), gain.reshape(1, -1),
      st_c, st_n, jnp.broadcast_to(st_m[:, :, None], (DEC_BATCH, M_HEADS, LANES)), hm)


def _lambda(lam_ref, lam_init):
    p = lam_ref[...]
    s1 = jnp.sum(p[0:1, :] * p[1:2, :], axis=1, keepdims=True)
    s2 = jnp.sum(p[2:3, :] * p[3:4, :], axis=1, keepdims=True)
    return jnp.exp(s1) - jnp.exp(s2) + lam_init


def _diff_prompt_kernel(slope_ref, q_ref, k_ref, vt_ref, km_ref, vtm_ref, lam_ref, gain_ref, o_ref,
                        base_ref, m_ref, acc_ref, *, lam_init):
    b = pl.program_id(0)
    hg = pl.program_id(1)
    is_meta = pl.program_id(2) == 0
    t = pl.program_id(2) - 1
    tq = ROW_BLOCK
    heads = range(HEAD_GROUP)
    cols = [slice(i * LANES, (i + 1) * LANES) for i in heads]
    slopes = [slope_ref[hg * HEAD_GROUP + i] * LOG2E for i in heads]

    @pl.when(jnp.logical_not(is_meta & (b > 0)))
    def _():
        lo = lax.broadcasted_iota(jnp.int32, (tq, LANES), 1) < D_QK_DIM
        q2 = []
        for i in heads:
            q = q_ref[:, cols[i]].astype(BF16)
            zero = jnp.zeros_like(q)
            q2.append(jnp.concatenate([jnp.where(lo, q, zero), jnp.where(lo, zero, q)], axis=0))
            base_ref[i] = slopes[i] * lax.broadcasted_iota(jnp.int32, (tq, 2 * tq), 0).astype(F32)
        q_pos0 = jnp.where(is_meta, 0, N_META + t * tq)
        ones = jnp.ones((ONES_ROWS, tq), BF16)
        m_ref[...] = jnp.full_like(m_ref, NEG_INF)
        acc_ref[...] = jnp.zeros_like(acc_ref)

        def visible(n_keys, k_off):
            key = lax.broadcasted_iota(jnp.int32, (n_keys, 2 * tq), 0) + k_off
            qry = lax.broadcasted_iota(jnp.int32, (n_keys, 2 * tq), 1)
            return key <= jnp.where(qry >= tq, qry - tq, qry)

        def step(i, s, vt, k_off):
            c = slopes[i] * k_off.astype(F32)
            m_old = m_ref[i]
            m_new = jnp.maximum(m_old, jnp.max(s, axis=0, keepdims=True) + c)
            p = jnp.exp2(s - (m_new - c)).astype(BF16)
            lhs = jnp.concatenate([vt, ones[:, 0:vt.shape[1]]], axis=0)
            acc_ref[i] = jnp.exp2(m_old - m_new) * acc_ref[i] + _dot(lhs, p)
            m_ref[i] = m_new

        def scores(kb):
            ks = pl.ds(pl.multiple_of(kb * tq, tq), tq)
            return [_dot_nt(k_ref[ks, cols[i]], q2[i]) for i in heads]

        def full_blocks(kbs):
            ss = [scores(kb) for kb in kbs]
            for kb, s in zip(kbs, ss):
                for i in heads:
                    step(i, s[i] + base_ref[i], vt_ref[kb, cols[i], :], N_META + kb * tq - q_pos0)

        n_full = jnp.where(is_meta, 0, t)
        n_main = n_full // KV_UNROLL

        def unrolled(j, carry):
            full_blocks([j * KV_UNROLL + u for u in range(KV_UNROLL)])
            return carry

        def single(kb, carry):
            full_blocks([kb])
            return carry

        lax.fori_loop(0, n_main, unrolled, 0)
        lax.fori_loop(n_main * KV_UNROLL, n_full, single, 0)

        td = jnp.maximum(t, 0)
        hide = jnp.where(is_meta, 2 * tq, 0)
        s_meta = [_dot_nt(km_ref[:, cols[i]], q2[i]) for i in heads]
        s_diag = scores(td)
        for i in heads:
            s = s_meta[i] + base_ref[i, 0:N_META, :]
            step(i, jnp.where(visible(N_META, -q_pos0), s, NEG_INF), vtm_ref[0, cols[i], 0:N_META], -q_pos0)
        for i in heads:
            s = s_diag[i] + base_ref[i]
            step(i, jnp.where(visible(tq, hide), s, NEG_INF), vt_ref[td, cols[i], :], jnp.int32(0))

        lam = _lambda(lam_ref, lam_init)
        for i in heads:
            acc = acc_ref[i]
            o_t = acc[0:D_V_DIM, :] * (1.0 / acc[D_V_DIM:D_V_DIM + 1, :])
            o = (o_t[:, 0:tq] - lam * o_t[:, tq:2 * tq]).T
            o_ref[:, cols[i]] = _rms(o, gain_ref[:, cols[i]]) * (1.0 - lam_init)


def _diff_prompt(qn, kb, vt, lam_pack, gain, slopes, lam_init):
    r = qn.shape[0]
    meta_row = _rows()[2]
    nq = SEQ // ROW_BLOCK
    meta_blk = meta_row // ROW_BLOCK
    gw = HEAD_GROUP * LANES
    assert meta_row % N_META == 0 and D_V_DIM == LANES and D_HEADS % HEAD_GROUP == 0

    def qmap(b, h, qi, *_):
        return (jnp.where(qi == 0, jnp.where(b == 0, meta_blk, b * nq), b * nq + qi - 1), h)

    grid_spec = pltpu.PrefetchScalarGridSpec(
        num_scalar_prefetch=1,
        grid=(BATCH, D_HEADS // HEAD_GROUP, nq + 1),
        in_specs=[pl.BlockSpec((ROW_BLOCK, gw), qmap),
                  pl.BlockSpec((SEQ, gw), lambda b, h, qi, *_: (b, h)),
                  pl.BlockSpec((nq, gw, ROW_BLOCK), lambda b, h, qi, *_: (b, h, 0)),
                  pl.BlockSpec((N_META, gw), lambda b, h, qi, *_: (meta_row // N_META, h)),
                  pl.BlockSpec((1, gw, ROW_BLOCK), lambda b, h, qi, *_: (meta_blk, h, 0)),
                  pl.BlockSpec((SUBLANES, LANES), lambda b, h, qi, *_: (0, 0)),
                  pl.BlockSpec((1, gw), lambda b, h, qi, *_: (0, h))],
        out_specs=pl.BlockSpec((ROW_BLOCK, gw), qmap),
        scratch_shapes=[pltpu.VMEM((HEAD_GROUP, ROW_BLOCK, 2 * ROW_BLOCK), F32),
                        pltpu.VMEM((HEAD_GROUP, 1, 2 * ROW_BLOCK), F32),
                        pltpu.VMEM((HEAD_GROUP, D_V_DIM + ONES_ROWS, 2 * ROW_BLOCK), F32)])
    return pl.pallas_call(
        functools.partial(_diff_prompt_kernel, lam_init=lam_init),
        grid_spec=grid_spec,
        out_shape=jax.ShapeDtypeStruct((r, D_V_WIDTH), F32),
        compiler_params=_params("arbitrary", "arbitrary", "arbitrary"),
        name="diff_prompt",
    )(slopes, qn, kb, vt, kb, vt, lam_pack, gain.reshape(1, -1))


def _diff_sample_kernel(pt_ref, *refs, lam_init):
    del pt_ref
    pp = PAGES_PER_STEP
    k_pages, v_pages = refs[0:pp], refs[pp:2 * pp]
    (q_ref, kn_ref, vn_ref, lam_ref, gain_ref, hd_in, o_ref,
     qb_ref, bias_ref, kp_ref, vp_ref, m_ref, acc_ref) = refs[2 * pp:]
    del hd_in
    g = pl.program_id(1)
    rows = 2 * D_HEADS * DEC_SEQ
    half = D_HEADS * DEC_SEQ
    cols = PAGE_SIZE * D_HEADS

    row = lax.broadcasted_iota(jnp.int32, (rows, 1), 0)
    r_tok = row % DEC_SEQ
    r_head = (row // DEC_SEQ) % D_HEADS
    slope = jnp.exp2(-(r_head + 1).astype(F32) * (8.0 / D_HEADS)) * LOG2E

    def step(s, v, c):
        m_old = m_ref[...]
        m_new = jnp.maximum(m_old, jnp.max(s, axis=1, keepdims=True) + c)
        p = jnp.exp2(s - (m_new - c)).astype(BF16)
        v_ones = jnp.concatenate([v, jnp.ones_like(v)], axis=1)
        acc_ref[...] = jnp.exp2(m_old - m_new) * acc_ref[...] + _dot(p, v_ones)
        m_ref[...] = m_new

    @pl.when(g == 0)
    def _():
        lo = lax.broadcasted_iota(jnp.int32, (DEC_SEQ, LANES), 1) < D_QK_DIM
        q = q_ref[...]
        heads = [q[:, h * LANES:(h + 1) * LANES] for h in range(D_HEADS)]
        qb_ref[...] = jnp.concatenate([jnp.where(lo, x, 0.0) for x in heads]
                                      + [jnp.where(lo, 0.0, x) for x in heads], axis=0).astype(BF16)
        col = lax.broadcasted_iota(jnp.int32, (rows, cols), 1)
        bias_ref[...] = jnp.where(col % D_HEADS == r_head, slope * (col // D_HEADS).astype(F32), NEG_INF)
        m_ref[...] = jnp.full_like(m_ref, NEG_INF)
        acc_ref[...] = jnp.zeros_like(acc_ref)

    qb = qb_ref[...]
    for p in range(pp):
        page_pos0 = (g * pp + p) * PAGE_SIZE - PAST_LEN
        k2 = k_pages[p][...].reshape(cols, LANES).astype(BF16)
        v2 = v_pages[p][...].reshape(cols, LANES).astype(BF16)
        step(_dot_nt(qb, k2) + bias_ref[...], v2, slope * page_pos0.astype(F32))

    @pl.when(g == pl.num_programs(1) - 1)
    def _():
        for src, dst in ((kn_ref, kp_ref), (vn_ref, vp_ref)):
            dst[...] = jnp.zeros_like(dst)
            dst[0:half, :] = src[...].reshape(half, LANES)
        c = lax.broadcasted_iota(jnp.int32, (1, LANES), 1)
        c_tok = c // D_HEADS
        ok = (c % D_HEADS == r_head) & (c_tok <= r_tok)
        s = _dot_nt(qb, kp_ref[...].astype(BF16)) + slope * c_tok.astype(F32)
        step(jnp.where(ok, s, NEG_INF), vp_ref[...].astype(BF16), jnp.zeros((rows, 1), F32))

        lam = _lambda(lam_ref, lam_init)
        acc = acc_ref[...]
        o = acc[:, 0:LANES] * (1.0 / acc[:, LANES:2 * LANES])
        for h in range(D_HEADS):
            cs = slice(h * LANES, (h + 1) * LANES)
            oh = o[h * DEC_SEQ:(h + 1) * DEC_SEQ, :] - lam * o[half + h * DEC_SEQ:half + (h + 1) * DEC_SEQ, :]
            o_ref[:, cs] = _rms(oh, gain_ref[:, cs]) * (1.0 - lam_init)


def _diff_sample(layer, cache_k, cache_v, page_table, qn, kf, vf, lam_pack, gain, hd, lam_init):
    assert DEC_SEQ == SUBLANES and D_HEADS == SUBLANES and D_V_DIM == LANES
    assert D_HEADS * DEC_SEQ <= LANES and PAST_LEN % (PAGE_SIZE * PAGES_PER_STEP) == 0
    n_prompt = _rows()[0]
    blk0 = n_prompt // DEC_SEQ
    pp = PAGES_PER_STEP
    width = D_HEADS * LANES
    rows = 2 * D_HEADS * DEC_SEQ

    def page_spec(p):
        return pl.BlockSpec((None, None, PAGE_SIZE, D_HEADS, LANES),
                            lambda b, g, pt: (layer, pt[b, g * pp + p], 0, 0, 0))

    row_spec = pl.BlockSpec((DEC_SEQ, width), lambda b, g, pt: (blk0 + b, 0))
    new_spec = pl.BlockSpec((DEC_SEQ, D_HEADS, LANES), lambda b, g, pt: (b, 0, 0))
    grid_spec = pltpu.PrefetchScalarGridSpec(
        num_scalar_prefetch=1,
        grid=(DEC_BATCH, PAST_LEN // (PAGE_SIZE * pp)),
        in_specs=[page_spec(p) for p in range(pp)] * 2
                 + [row_spec, new_spec, new_spec,
                    pl.BlockSpec((SUBLANES, LANES), lambda b, g, pt: (0, 0)),
                    pl.BlockSpec((1, width), lambda b, g, pt: (0, 0)),
                    pl.BlockSpec(memory_space=pl.ANY)],
        out_specs=row_spec,
        scratch_shapes=[pltpu.VMEM((rows, LANES), BF16), pltpu.VMEM((rows, PAGE_SIZE * D_HEADS), F32),
                        pltpu.VMEM((LANES, LANES), F32), pltpu.VMEM((LANES, LANES), F32),
                        pltpu.VMEM((rows, 1), F32), pltpu.VMEM((rows, 2 * LANES), F32)])
    n_in = 2 * pp + 6
    return pl.pallas_call(
        functools.partial(_diff_sample_kernel, lam_init=lam_init),
        grid_spec=grid_spec,
        out_shape=jax.ShapeDtypeStruct(hd.shape, F32),
        input_output_aliases={n_in: 0},
        compiler_params=_params("arbitrary", "arbitrary"),
        name="diff_sample",
    )(page_table, *([cache_k] * pp), *([cache_v] * pp), qn, kf, vf, lam_pack, gain.reshape(1, -1), hd)


def kernel(x_prompt, x_sample, cache_k, cache_v, state_C, state_n, state_m, page_table, meta_tokens, norm_mix, w_in, b_if, q_gain, k_gain, lambda_q1, lambda_k1, lambda_q2, lambda_k2, mlstm_norm, diff_norm, w_pm, w_pd, w_out, norm_ffn, w_gu, w_down):
    assert M_HEAD_DIM ** -0.5 == 2.0 ** round(math.log2(M_HEAD_DIM ** -0.5))
    n_prompt, n_sample, meta_row, r = _rows()
    x = jnp.concatenate([x_prompt.reshape(n_prompt, D_MODEL), x_sample.reshape(n_sample, D_MODEL),
                         meta_tokens.astype(F32), jnp.zeros((r - meta_row - N_META, D_MODEL), F32)], axis=0)
    slopes = 2.0 ** (-8.0 * jnp.arange(1, D_HEADS + 1, dtype=F32) / D_HEADS)
    w_in_t = jnp.swapaxes(w_in, 1, 2)
    g0 = 4 * M_WIDTH
    g1 = g0 + 2 * M_HEADS

    k_prompt = v_prompt = None
    meta_kv = []
    outs = [[] for _ in range(8)]
    for l in range(DEPTH):
        lam_init = 0.8 - 0.6 * math.exp(-0.3 * l)
        xn = _rmsnorm(x, norm_mix[l], "norm_mix")
        proj_m = _matmul_nt(xn, w_in_t, l, 0, g0, "proj_mlstm")
        graw = _matmul_nt(xn, w_in_t, l, g0, 2 * M_HEADS, "proj_gates")
        proj_d = _matmul_nt(xn, w_in_t, l, g1, w_in_t.shape[1] - g1, "proj_diff")

        hm, c_p, n_p, m_p = _mlstm_prompt(proj_m, proj_m, graw, graw.T, b_if[l], mlstm_norm[l])
        gs = graw[n_prompt:n_prompt + n_sample].reshape(DEC_BATCH, DEC_SEQ, 2 * M_HEADS)
        gs = jnp.pad(gs, ((0, 0), (0, SAMPLE_CHUNK - DEC_SEQ), (0, 0)))
        hm, c_s, n_s, m_s = _mlstm_sample(proj_m, proj_m, gs, gs.transpose(0, 2, 1), b_if[l], mlstm_norm[l],
                                          state_C[l].astype(F32), state_n[l].astype(F32), state_m[l].astype(F32), hm)

        qn, kb, vt, k_prompt, v_prompt, kf, vf = _qk_norm(proj_d, q_gain[l], k_gain[l], l, k_prompt, v_prompt)
        lam_pack = jnp.pad(jnp.stack([lambda_q1[l], lambda_k1[l], lambda_q2[l], lambda_k2[l]]).astype(F32),
                           ((0, SUBLANES - 4), (0, LANES - D_QK_DIM)))
        hd = _diff_prompt(qn, kb, vt, lam_pack, diff_norm[l], slopes, lam_init)
        hd = _diff_sample(l, cache_k, cache_v, page_table, qn, kf, vf, lam_pack, diff_norm[l], hd, lam_init)

        z = _merge(hm, hd, w_pm, w_pd, l, proj_d, 2 * D_QK_WIDTH + D_V_WIDTH)
        x = _matmul_res(z, w_out, l, x, "out_proj", MATMUL_ROWS)
        act = _swiglu(_rmsnorm(x, norm_ffn[l], "norm_ffn"), w_gu, l)
        x = _matmul_res(act, w_down, l, x, "ffn_down", MATMUL_ROWS // 2)

        head_shape = (D_HEADS, D_V_DIM)
        meta_kv.append((kf[n_sample:n_sample + N_META], vf[n_sample:n_sample + N_META]))
        vals = (c_p, n_p, m_p[:, :, 0],
                kf[:n_sample].reshape(DEC_BATCH, DEC_SEQ, *head_shape),
                vf[:n_sample].reshape(DEC_BATCH, DEC_SEQ, *head_shape),
                c_s, n_s, m_s[:, :, 0])
        for o, v in zip(outs, vals):
            o.append(v)

    for l, (km, vm) in enumerate(meta_kv):
        k_prompt = k_prompt.at[l, :, :N_META].set(jnp.broadcast_to(km, (BATCH,) + km.shape))
        v_prompt = v_prompt.at[l, :, :N_META].set(jnp.broadcast_to(vm, (BATCH,) + vm.shape))
    y_prompt = x[:n_prompt].reshape(BATCH, SEQ, D_MODEL)
    y_sample = x[n_prompt:n_prompt + n_sample].reshape(DEC_BATCH, DEC_SEQ, D_MODEL)
    return (y_prompt, y_sample, k_prompt, v_prompt) + tuple(jnp.stack(o) for o in outs)
```

```python
import functools
import math

import jax
import jax.numpy as jnp
from jax import lax
from jax.experimental import pallas as pl
from jax.experimental.pallas import tpu as pltpu

D_MODEL = 2048
BATCH = 2
SEQ = 4096
DEPTH = 2
DEC_BATCH = 32
DEC_SEQ = 8
PAST_LEN = 8192
PAGE_SIZE = 128
N_META = 16
M_HEADS = 4
M_HEAD_DIM = 256
M_WIDTH = M_HEADS * M_HEAD_DIM
D_HEADS = 8
D_QK_DIM = 64
D_V_DIM = 2 * D_QK_DIM
D_QK_WIDTH = D_HEADS * 2 * D_QK_DIM
D_V_WIDTH = D_HEADS * D_V_DIM
EPS = 1e-6

LANES = 128
SUBLANES = 8
ROW_BLOCK = 256
MATMUL_ROWS = 1100
K_SCALE = M_HEAD_DIM ** -0.5
SAMPLE_CHUNK = 128
KV_UNROLL = 2
HEAD_GROUP = 4
PAGES_PER_STEP = 8
VMEM_LIMIT = 56 * 1024 * 1024

F32 = jnp.float32
BF16 = jnp.bfloat16
NEG_INF = float("-inf")
LOG2E = math.log2(math.e)
Q_SCALE = D_QK_DIM ** -0.5 * LOG2E
ONES_ROWS = 16


def _ffn_dim():
    return ((8 * D_MODEL // 3 + 255) // 256) * 256


def _rows():
    n_prompt = BATCH * SEQ
    n_sample = DEC_BATCH * DEC_SEQ
    assert n_prompt % ROW_BLOCK == 0 and n_sample % ROW_BLOCK == 0 and N_META <= ROW_BLOCK
    return n_prompt, n_sample, n_prompt + n_sample, n_prompt + n_sample + ROW_BLOCK


def _row_tile(r):
    return 512 if r % 512 == 0 else ROW_BLOCK


def _tile(n, cap, mult=16):
    for t in range(min(cap, n) // mult * mult, 0, -mult):
        if n % t == 0:
            return t
    raise ValueError((n, cap, mult))


def _col_tile(n):
    for t in (1024, 512, 256, 128):
        if n % t == 0:
            return t
    raise ValueError(n)


def _params(*sem):
    return pltpu.CompilerParams(dimension_semantics=sem, vmem_limit_bytes=VMEM_LIMIT)


def _dot(a, b):
    return jnp.dot(a, b, preferred_element_type=F32)


def _dot_nt(a, b):
    return lax.dot_general(a, b, (((1,), (1,)), ((), ())), preferred_element_type=F32)


def _dot_tn(a, b):
    return lax.dot_general(a, b, (((0,), (0,)), ((), ())), preferred_element_type=F32)


def _sigmoid(x):
    return 1.0 / (1.0 + jnp.exp(-x))


def _rms(x, gain):
    ms = jnp.mean(x * x, axis=-1, keepdims=True)
    return x * lax.rsqrt(ms + EPS) * gain


def _rmsnorm_kernel(x_ref, g_ref, o_ref):
    o_ref[...] = _rms(x_ref[...], g_ref[...]).astype(o_ref.dtype)


def _rmsnorm(x, gain, name):
    r, d = x.shape
    tm = _tile(r, MATMUL_ROWS)
    return pl.pallas_call(
        _rmsnorm_kernel,
        grid=(r // tm,),
        in_specs=[pl.BlockSpec((tm, d), lambda i: (i, 0)), pl.BlockSpec((1, d), lambda i: (0, 0))],
        out_specs=pl.BlockSpec((tm, d), lambda i: (i, 0)),
        out_shape=jax.ShapeDtypeStruct((r, d), BF16),
        compiler_params=_params("parallel"),
        name=name,
    )(x, gain.reshape(1, d))


def _weight_spec(w, layer, tn, col0):
    if w.ndim == 3:
        return pl.BlockSpec((None, w.shape[1], tn), lambda j, i: (layer, 0, col0 + j))
    return pl.BlockSpec((w.shape[0], tn), lambda j, i: (0, col0 + j))


def _cast_weights(w_refs, wb_refs):
    @pl.when(pl.program_id(1) == 0)
    def _():
        for w_ref, wb_ref in zip(w_refs, wb_refs):
            wb_ref[...] = w_ref[...].astype(BF16)


def _matmul_kernel(a_ref, w_ref, o_ref, wb_ref):
    _cast_weights([w_ref], [wb_ref])
    o_ref[...] = _dot(a_ref[...], wb_ref[...]).astype(o_ref.dtype)


def _matmul(a, w, layer, n, out_dtype, name, tn=None):
    r, k = a.shape
    tm, tn = _tile(r, MATMUL_ROWS), tn or _col_tile(n)
    return pl.pallas_call(
        _matmul_kernel,
        grid=(n // tn, r // tm),
        in_specs=[pl.BlockSpec((tm, k), lambda j, i: (i, 0)), _weight_spec(w, layer, tn, 0)],
        out_specs=pl.BlockSpec((tm, tn), lambda j, i: (i, j)),
        out_shape=jax.ShapeDtypeStruct((r, n), out_dtype),
        scratch_shapes=[pltpu.VMEM((k, tn), BF16)],
        compiler_params=_params("parallel", "arbitrary"),
        name=name,
    )(a, w)


def _matmul_nt_kernel(a_ref, w_ref, o_ref, wb_ref):
    _cast_weights([w_ref.at[0]], [wb_ref])
    o_ref[...] = _dot_nt(a_ref[...], wb_ref[...]).astype(o_ref.dtype)


def _matmul_nt(a, w_t, layer, row0, n, name):
    r, k = a.shape
    tm, tn = _tile(r, MATMUL_ROWS), _tile(n, 1024, SUBLANES)
    assert row0 % SUBLANES == 0
    return pl.pallas_call(
        _matmul_nt_kernel,
        grid=(n // tn, r // tm),
        in_specs=[pl.BlockSpec((tm, k), lambda j, i: (i, 0)),
                  pl.BlockSpec((pl.Element(1), pl.Element(tn), pl.Element(k)),
                               lambda j, i: (layer, pl.multiple_of(row0 + j * tn, SUBLANES), 0))],
        out_specs=pl.BlockSpec((tm, tn), lambda j, i: (i, j)),
        out_shape=jax.ShapeDtypeStruct((r, n), F32),
        scratch_shapes=[pltpu.VMEM((tn, k), BF16)],
        compiler_params=_params("parallel", "arbitrary"),
        name=name,
    )(a, w_t)


def _swiglu_kernel(a_ref, wg_ref, wu_ref, o_ref, wgb_ref, wub_ref):
    _cast_weights([wg_ref, wu_ref], [wgb_ref, wub_ref])
    a = a_ref[...]
    g = _dot(a, wgb_ref[...])
    u = _dot(a, wub_ref[...])
    o_ref[...] = (g * _sigmoid(g) * u).astype(o_ref.dtype)


def _swiglu(a, w_gu, layer):
    r, k = a.shape
    f = w_gu.shape[2] // 2
    tm, tn = _tile(r, MATMUL_ROWS), min(512, _col_tile(f))
    nj = f // tn
    return pl.pallas_call(
        _swiglu_kernel,
        grid=(nj, r // tm),
        in_specs=[pl.BlockSpec((tm, k), lambda j, i: (i, 0)),
                  _weight_spec(w_gu, layer, tn, 0), _weight_spec(w_gu, layer, tn, nj)],
        out_specs=pl.BlockSpec((tm, tn), lambda j, i: (i, j)),
        out_shape=jax.ShapeDtypeStruct((r, f), BF16),
        scratch_shapes=[pltpu.VMEM((k, tn), BF16)] * 2,
        compiler_params=_params("parallel", "arbitrary"),
        name="ffn_up",
    )(a, w_gu, w_gu)


def _matmul_res_kernel(a_ref, w_ref, r_ref, o_ref, wb_ref):
    _cast_weights([w_ref], [wb_ref])
    o_ref[...] = r_ref[...] + _dot(a_ref[...], wb_ref[...])


def _matmul_res(a, w, layer, res, name, rows):
    r, k = a.shape
    n = w.shape[2]
    tm, tn = _tile(r, rows), min(512, _col_tile(n))
    return pl.pallas_call(
        _matmul_res_kernel,
        grid=(n // tn, r // tm),
        in_specs=[pl.BlockSpec((tm, k), lambda j, i: (i, 0)), _weight_spec(w, layer, tn, 0),
                  pl.BlockSpec((tm, tn), lambda j, i: (i, j))],
        out_specs=pl.BlockSpec((tm, tn), lambda j, i: (i, j)),
        out_shape=jax.ShapeDtypeStruct((r, n), F32),
        scratch_shapes=[pltpu.VMEM((k, tn), BF16)],
        compiler_params=_params("parallel", "arbitrary"),
        name=name,
    )(a, w, res)


def _merge_kernel(hm_ref, hd_ref, wpm_ref, wpd_ref, gm_ref, gd_ref, o_ref, wpmb_ref, wpdb_ref):
    _cast_weights([wpm_ref, wpd_ref], [wpmb_ref, wpdb_ref])
    zm = _dot(hm_ref[...].astype(BF16), wpmb_ref[...])
    zd = _dot(hd_ref[...].astype(BF16), wpdb_ref[...])
    o_ref[...] = (_sigmoid(gm_ref[...]) * zm + _sigmoid(gd_ref[...]) * zd).astype(o_ref.dtype)


def _merge(hm, hd, w_pm, w_pd, layer, gates, gate_col):
    r = hm.shape[0]
    n = w_pm.shape[2]
    tm, tn = _tile(r, MATMUL_ROWS // 2), _col_tile(math.gcd(n, gate_col))
    gm0, gd0 = gate_col // tn, (gate_col + n) // tn
    return pl.pallas_call(
        _merge_kernel,
        grid=(n // tn, r // tm),
        in_specs=[pl.BlockSpec((tm, M_WIDTH), lambda j, i: (i, 0)),
                  pl.BlockSpec((tm, D_V_WIDTH), lambda j, i: (i, 0)),
                  _weight_spec(w_pm, layer, tn, 0), _weight_spec(w_pd, layer, tn, 0),
                  pl.BlockSpec((tm, tn), lambda j, i: (i, j + gm0)),
                  pl.BlockSpec((tm, tn), lambda j, i: (i, j + gd0))],
        out_specs=pl.BlockSpec((tm, tn), lambda j, i: (i, j)),
        out_shape=jax.ShapeDtypeStruct((r, n), BF16),
        scratch_shapes=[pltpu.VMEM((M_WIDTH, tn), BF16), pltpu.VMEM((D_V_WIDTH, tn), BF16)],
        compiler_params=_params("parallel", "arbitrary"),
        name="merge",
    )(hm, hd, w_pm, w_pd, gates, gates)


def _half_norm(x, gain):
    lo = lax.broadcasted_iota(jnp.int32, x.shape, 1) < D_QK_DIM
    x2 = x * x
    s_lo = jnp.sum(jnp.where(lo, x2, 0.0), axis=1, keepdims=True)
    s_hi = jnp.sum(jnp.where(lo, 0.0, x2), axis=1, keepdims=True)
    ms = jnp.where(lo, s_lo, s_hi) * (1.0 / D_QK_DIM)
    return x * lax.rsqrt(ms + EPS) * gain


def _qk_norm_kernel(q_ref, k_ref, v_ref, qg_ref, kg_ref, *refs, n_prompt_tiles):
    qo_ref, kb_ref, vt_ref, kp_ref, vp_ref, kt_ref, vt8_ref = refs[-7:]
    i = pl.program_id(0)
    kn = []
    for h in range(D_HEADS):
        sl = slice(h * LANES, (h + 1) * LANES)
        qo_ref[:, sl] = _half_norm(q_ref[:, sl], qg_ref[...]) * Q_SCALE
        kn.append(_half_norm(k_ref[:, sl], kg_ref[...]))
        kb_ref[:, sl] = kn[h].astype(BF16)
    for j in range(vt_ref.shape[0]):
        vt_ref[j] = v_ref[j * ROW_BLOCK:(j + 1) * ROW_BLOCK, :].T.astype(BF16)

    @pl.when(i < n_prompt_tiles)
    def _():
        for h in range(D_HEADS):
            kp_ref[0, 0, :, h, :] = kn[h]
            vp_ref[0, 0, :, h, :] = v_ref[:, h * LANES:(h + 1) * LANES]

    @pl.when(i == n_prompt_tiles)
    def _():
        for h in range(D_HEADS):
            kt_ref[:, h, :] = kn[h]
            vt8_ref[:, h, :] = v_ref[:, h * LANES:(h + 1) * LANES]


def _qk_norm(qkv, q_gain, k_gain, layer, k_out, v_out):
    r = qkv.shape[0]
    w = D_QK_WIDTH
    n_prompt = _rows()[0]
    tm = r - n_prompt
    assert n_prompt % tm == 0 and SEQ % tm == 0 and tm % ROW_BLOCK == 0
    n_prompt_tiles, tiles_per_seq = n_prompt // tm, SEQ // tm
    spec = lambda c: pl.BlockSpec((tm, w), lambda i: (i, c))
    gspec = pl.BlockSpec((1, LANES), lambda i: (0, 0))

    def prompt_map(i):
        t = jnp.minimum(i, n_prompt_tiles - 1)
        return (layer, t // tiles_per_seq, N_META + (t % tiles_per_seq) * tm, 0, 0)

    pspec = pl.BlockSpec(tuple(pl.Element(s) for s in (1, 1, tm, D_HEADS, LANES)), prompt_map)
    tspec = pl.BlockSpec((tm, D_HEADS, LANES), lambda i: (0, 0, 0))
    out5 = jax.ShapeDtypeStruct((DEPTH, BATCH, N_META + SEQ, D_HEADS, LANES), F32)
    tail = jax.ShapeDtypeStruct((tm, D_HEADS, LANES), F32)
    carried = [] if k_out is None else [k_out, v_out]
    n_in = 5 + len(carried)
    return pl.pallas_call(
        functools.partial(_qk_norm_kernel, n_prompt_tiles=n_prompt_tiles),
        grid=(r // tm,),
        in_specs=[spec(0), spec(1), spec(2), gspec, gspec] + [pl.BlockSpec(memory_space=pl.ANY)] * len(carried),
        out_specs=[spec(0), spec(0), pl.BlockSpec((tm // ROW_BLOCK, D_V_WIDTH, ROW_BLOCK), lambda i: (i, 0, 0)),
                   pspec, pspec, tspec, tspec],
        out_shape=[jax.ShapeDtypeStruct((r, w), F32), jax.ShapeDtypeStruct((r, w), BF16),
                   jax.ShapeDtypeStruct((r // ROW_BLOCK, D_V_WIDTH, ROW_BLOCK), BF16), out5, out5, tail, tail],
        input_output_aliases={} if k_out is None else {n_in - 2: 3, n_in - 1: 4},
        compiler_params=_params("arbitrary"),
        name="qk_norm",
    )(qkv, qkv, qkv, jnp.tile(q_gain, 2).reshape(1, LANES), jnp.tile(k_gain, 2).reshape(1, LANES), *carried)


def _split3(x):
    hi = x.astype(BF16)
    r1 = x - hi.astype(F32)
    mid = r1.astype(BF16)
    lo = (r1 - mid.astype(F32)).astype(BF16)
    return hi, mid, lo


def _log_sigmoid(x):
    return jnp.minimum(x, 0.0) - jnp.log1p(jnp.exp(-jnp.abs(x)))


def _mlstm_gates(graw_col, graw_row, bias_col, bias_row, n_valid):
    l = graw_col.shape[0]
    g_col = graw_col + bias_col
    g_row = graw_row + bias_row
    is_f_col = lax.broadcasted_iota(jnp.int32, g_col.shape, 1) >= M_HEADS
    is_f_row = lax.broadcasted_iota(jnp.int32, g_row.shape, 0) >= M_HEADS
    ok_col = lax.broadcasted_iota(jnp.int32, g_col.shape, 0) < n_valid
    ok_row = lax.broadcasted_iota(jnp.int32, g_row.shape, 1) < n_valid
    lf_col = jnp.where(is_f_col & ok_col, _log_sigmoid(g_col), 0.0)
    lf_row = jnp.where(is_f_row & ok_row, _log_sigmoid(g_row), 0.0)
    ig_col = jnp.where(ok_col, g_col, NEG_INF)
    ig_row = jnp.where(ok_row, g_row, NEG_INF)
    rr = lax.broadcasted_iota(jnp.int32, (l, l), 0)
    cc = lax.broadcasted_iota(jnp.int32, (l, l), 1)
    tri_l = (cc <= rr).astype(BF16)
    tri_u = (rr <= cc).astype(BF16)
    b_col = sum(_dot(tri_l, p) for p in _split3(lf_col))
    b_row = sum(_dot(p, tri_u) for p in _split3(lf_row))
    return ig_col, b_col, ig_row, b_row, cc <= rr


def _mlstm_chunk(q, k, v, ig_col, b_col, ig_row, b_row, causal, c_st, n_st, m_st):
    l = q.shape[0]
    d = b_col - b_row + ig_row
    d = jnp.where(causal, d, NEG_INF)
    inter = b_col + m_st
    m_t = jnp.maximum(inter, jnp.max(d, axis=1, keepdims=True))
    w = jnp.exp(d - m_t)
    s = _dot_nt(q, k) * w
    a = jnp.exp(inter - m_t)
    num = _dot(s.astype(BF16), v) + a * _dot_nt(q, c_st.astype(BF16))
    den = jnp.sum(s, axis=1, keepdims=True) + a * jnp.sum(q.astype(F32) * n_st, axis=1, keepdims=True)
    h = num * (1.0 / jnp.maximum(jnp.abs(den), jnp.exp(-m_t)))

    b_last = b_col[l - 1:l, :]
    g_col = b_last - b_col + ig_col
    g_row = b_last - b_row + ig_row
    m_new = jnp.maximum(b_last + m_st, jnp.max(g_row, axis=1, keepdims=True))
    ws_col = jnp.exp(g_col - m_new)
    ws_row = jnp.exp(g_row - m_new)
    decay = jnp.exp(b_last + m_st - m_new)
    wv = (v.astype(F32) * ws_col).astype(BF16)
    c_new = decay * c_st + _dot_tn(wv, k)
    n_new = decay * n_st + _dot(ws_row.astype(BF16), k)
    return h, c_new, n_new, m_new


def _mlstm_heads(q_of, k_of, v_of, om_of, gates, gain_ref, c_ref, n_ref, m_ref, h_out):
    ig_col, b_col, ig_row, b_row, causal = gates
    for h in range(M_HEADS):
        f = M_HEADS + h
        hs = slice(h * M_HEAD_DIM, (h + 1) * M_HEAD_DIM)
        out, c_new, n_new, m_new = _mlstm_chunk(
            q_of(hs), k_of(hs), v_of(hs),
            ig_col[:, h:h + 1], b_col[:, f:f + 1], ig_row[h:h + 1, :], b_row[f:f + 1, :], causal,
            c_ref[0, h], n_ref[0, h:h + 1, :], m_ref[0, h:h + 1, 0:1])
        c_ref[0, h] = c_new
        n_ref[0, h:h + 1, :] = n_new
        m_ref[0, h:h + 1, :] = jnp.broadcast_to(m_new, (1, LANES))
        h_out(hs, _rms(out, gain_ref[:, hs]) * _sigmoid(om_of(hs)))


def _mlstm_prompt_kernel(q_ref, k_ref, v_ref, om_ref, gc_ref, gr_ref, bc_ref, br_ref, gain_ref,
                         h_ref, c_ref, n_ref, m_ref):
    c = pl.program_id(1)

    @pl.when(c == 0)
    def _():
        c_ref[...] = jnp.zeros_like(c_ref)
        n_ref[...] = jnp.zeros_like(n_ref)
        m_ref[...] = jnp.zeros_like(m_ref)

    n_valid = jnp.where(c == 0, N_META, ROW_BLOCK)
    gates = _mlstm_gates(gc_ref[...], gr_ref[...], bc_ref[...], br_ref[...], n_valid)

    def h_out(hs, val):
        h_ref[:, hs] = val

    _mlstm_heads(lambda hs: q_ref[:, hs].astype(BF16), lambda hs: (k_ref[:, hs] * K_SCALE).astype(BF16),
                 lambda hs: v_ref[:, hs].astype(BF16), lambda hs: om_ref[:, hs],
                 gates, gain_ref, c_ref, n_ref, m_ref, h_out)


def _mlstm_prompt(qkv, gates, g_col, g_row, b_if, gain):
    r = qkv.shape[0]
    n_prompt, _, meta_row, _ = _rows()
    nc = SEQ // ROW_BLOCK
    meta_blk = meta_row // ROW_BLOCK

    def rb(b, c):
        return jnp.where(c == 0, meta_blk, b * nc + c - 1)

    def rb_out(b, c):
        return jnp.where((c == 0) & (b > 0), b * nc, rb(b, c))

    spec = lambda col: pl.BlockSpec((ROW_BLOCK, M_WIDTH), lambda b, c: (rb(b, c), col))
    full = lambda shape: pl.BlockSpec(shape, lambda b, c: (0,) * len(shape))
    return pl.pallas_call(
        _mlstm_prompt_kernel,
        grid=(BATCH, nc + 1),
        in_specs=[spec(0), spec(1), spec(2), spec(3),
                  pl.BlockSpec((ROW_BLOCK, 2 * M_HEADS), lambda b, c: (rb(b, c), 0)),
                  pl.BlockSpec((2 * M_HEADS, ROW_BLOCK), lambda b, c: (0, rb(b, c))),
                  full((1, 2 * M_HEADS)), full((2 * M_HEADS, 1)), full((1, M_WIDTH))],
        out_specs=[pl.BlockSpec((ROW_BLOCK, M_WIDTH), lambda b, c: (rb_out(b, c), 0)),
                   pl.BlockSpec((1, M_HEADS, M_HEAD_DIM, M_HEAD_DIM), lambda b, c: (b, 0, 0, 0)),
                   pl.BlockSpec((1, M_HEADS, M_HEAD_DIM), lambda b, c: (b, 0, 0)),
                   pl.BlockSpec((1, M_HEADS, LANES), lambda b, c: (b, 0, 0))],
        out_shape=[jax.ShapeDtypeStruct((r, M_WIDTH), F32),
                   jax.ShapeDtypeStruct((BATCH, M_HEADS, M_HEAD_DIM, M_HEAD_DIM), F32),
                   jax.ShapeDtypeStruct((BATCH, M_HEADS, M_HEAD_DIM), F32),
                   jax.ShapeDtypeStruct((BATCH, M_HEADS, LANES), F32)],
        compiler_params=_params("arbitrary", "arbitrary"),
        name="mlstm_prompt",
    )(qkv, qkv, qkv, gates, g_col, g_row, b_if.reshape(1, -1), b_if.reshape(-1, 1), gain.reshape(1, -1))


def _mlstm_sample_kernel(q_ref, k_ref, v_ref, om_ref, gc_ref, gr_ref, bc_ref, br_ref, gain_ref,
                         c_in, n_in, m_in, *rest):
    h_ref, c_ref, n_ref, m_ref, qp, kp, vp = rest[-7:]
    c_ref[...] = c_in[...]
    n_ref[...] = n_in[...]
    m_ref[...] = m_in[...]
    for src, dst in ((q_ref, qp), (k_ref, kp), (v_ref, vp)):
        dst[...] = jnp.zeros_like(dst)
        dst[0:DEC_SEQ, :] = src[...]
    gates = _mlstm_gates(gc_ref[0], gr_ref[0], bc_ref[...], br_ref[...], DEC_SEQ)
    om = om_ref[...]

    def h_out(hs, val):
        h_ref[:, hs] = val[0:DEC_SEQ, :]

    def om_of(hs):
        return jnp.concatenate([om[:, hs], jnp.zeros((SAMPLE_CHUNK - DEC_SEQ, M_HEAD_DIM), F32)], axis=0)

    _mlstm_heads(lambda hs: qp[:, hs].astype(BF16), lambda hs: (kp[:, hs] * K_SCALE).astype(BF16),
                 lambda hs: vp[:, hs].astype(BF16), om_of,
                 gates, gain_ref, c_ref, n_ref, m_ref, h_out)


def _mlstm_sample(qkv, gates, g_col, g_row, b_if, gain, layer, state_c, state_n, st_m, hm, c_out, n_out):
    assert DEC_SEQ == SUBLANES
    n_prompt = _rows()[0]
    blk0 = n_prompt // DEC_SEQ
    spec = lambda col: pl.BlockSpec((DEC_SEQ, M_WIDTH), lambda b: (blk0 + b, col))
    full = lambda shape: pl.BlockSpec(shape, lambda b: (0,) * len(shape))
    st_spec = lambda shape: pl.BlockSpec((1,) + shape, lambda b: (b,) + (0,) * len(shape))
    layer_spec = lambda shape: pl.BlockSpec((None, 1) + shape, lambda b: (layer, b) + (0,) * len(shape))
    st_specs = [layer_spec((M_HEADS, M_HEAD_DIM, M_HEAD_DIM)), layer_spec((M_HEADS, M_HEAD_DIM)),
                st_spec((M_HEADS, LANES))]
    carried = [] if c_out is None else [c_out, n_out]
    aliases = {12: 0}
    if carried:
        aliases.update({13: 1, 14: 2})
    return pl.pallas_call(
        _mlstm_sample_kernel,
        grid=(DEC_BATCH,),
        in_specs=[spec(0), spec(1), spec(2), spec(3),
                  st_spec((SAMPLE_CHUNK, 2 * M_HEADS)), st_spec((2 * M_HEADS, SAMPLE_CHUNK)),
                  full((1, 2 * M_HEADS)), full((2 * M_HEADS, 1)), full((1, M_WIDTH))] + st_specs
                 + [pl.BlockSpec(memory_space=pl.ANY)] * (1 + len(carried)),
        out_specs=[spec(0)] + st_specs,
        out_shape=[jax.ShapeDtypeStruct(hm.shape, F32),
                   jax.ShapeDtypeStruct(state_c.shape, F32), jax.ShapeDtypeStruct(state_n.shape, F32),
                   jax.ShapeDtypeStruct((DEC_BATCH, M_HEADS, LANES), F32)],
        scratch_shapes=[pltpu.VMEM((SAMPLE_CHUNK, M_WIDTH), F32)] * 3,
        input_output_aliases=aliases,
        compiler_params=_params("arbitrary"),
        name="mlstm_sample",
    )(qkv, qkv, qkv, gates, g_col, g_row, b_if.reshape(1, -1), b_if.reshape(-1, 1), gain.reshape(1, -1),
      state_c, state_n, jnp.broadcast_to(st_m[:, :, None], (DEC_BATCH, M_HEADS, LANES)), hm, *carried)


def _lambda(lam_ref, lam_init):
    p = lam_ref[...]
    s1 = jnp.sum(p[0:1, :] * p[1:2, :], axis=1, keepdims=True)
    s2 = jnp.sum(p[2:3, :] * p[3:4, :], axis=1, keepdims=True)
    return jnp.exp(s1) - jnp.exp(s2) + lam_init


def _diff_prompt_kernel(slope_ref, q_ref, k_ref, vt_ref, km_ref, vtm_ref, lam_ref, gain_ref, o_ref,
                        base_ref, m_ref, acc_ref, *, lam_init):
    b = pl.program_id(0)
    hg = pl.program_id(1)
    is_meta = pl.program_id(2) == 0
    t = pl.program_id(2) - 1
    tq = ROW_BLOCK
    heads = range(HEAD_GROUP)
    cols = [slice(i * LANES, (i + 1) * LANES) for i in heads]
    slopes = [slope_ref[hg * HEAD_GROUP + i] * LOG2E for i in heads]

    @pl.when(jnp.logical_not(is_meta & (b > 0)))
    def _():
        lo = lax.broadcasted_iota(jnp.int32, (tq, LANES), 1) < D_QK_DIM
        q2 = []
        for i in heads:
            q = q_ref[:, cols[i]].astype(BF16)
            zero = jnp.zeros_like(q)
            q2.append(jnp.concatenate([jnp.where(lo, q, zero), jnp.where(lo, zero, q)], axis=0))
            base_ref[i] = slopes[i] * lax.broadcasted_iota(jnp.int32, (tq, 2 * tq), 0).astype(F32)
        q_pos0 = jnp.where(is_meta, 0, N_META + t * tq)
        ones = jnp.ones((ONES_ROWS, tq), BF16)
        m_ref[...] = jnp.full_like(m_ref, NEG_INF)
        acc_ref[...] = jnp.zeros_like(acc_ref)

        def visible(n_keys, k_off):
            key = lax.broadcasted_iota(jnp.int32, (n_keys, 2 * tq), 0) + k_off
            qry = lax.broadcasted_iota(jnp.int32, (n_keys, 2 * tq), 1)
            return key <= jnp.where(qry >= tq, qry - tq, qry)

        def step(i, parts):
            cs = [slopes[i] * k_off.astype(F32) for _, _, k_off in parts]
            m_old = m_ref[i]
            m_new = m_old
            for (s, _, _), c in zip(parts, cs):
                m_new = jnp.maximum(m_new, jnp.max(s, axis=0, keepdims=True) + c)
            acc = jnp.exp2(m_old - m_new) * acc_ref[i]
            for (s, vt, _), c in zip(parts, cs):
                p = jnp.exp2(s - (m_new - c)).astype(BF16)
                acc = acc + _dot(jnp.concatenate([vt, ones[:, 0:vt.shape[1]]], axis=0), p)
            acc_ref[i] = acc
            m_ref[i] = m_new

        def scores(kb, i):
            ks = pl.ds(pl.multiple_of(kb * tq, tq), tq)
            return _dot_nt(k_ref[ks, cols[i]], q2[i])

        def full_blocks(kbs):
            for i in heads:
                step(i, [(scores(kb, i) + base_ref[i], vt_ref[kb, cols[i], :], N_META + kb * tq - q_pos0)
                         for kb in kbs])

        n_full = jnp.where(is_meta, 0, t)
        n_main = n_full // KV_UNROLL

        def unrolled(j, carry):
            full_blocks([j * KV_UNROLL + u for u in range(KV_UNROLL)])
            return carry

        def single(kb, carry):
            full_blocks([kb])
            return carry

        lax.fori_loop(0, n_main, unrolled, 0)
        lax.fori_loop(n_main * KV_UNROLL, n_full, single, 0)

        td = jnp.maximum(t, 0)
        hide = jnp.where(is_meta, 2 * tq, 0)
        for i in heads:
            s_meta = _dot_nt(km_ref[:, cols[i]], q2[i]) + base_ref[i, 0:N_META, :]
            s_diag = scores(td, i) + base_ref[i]
            step(i, [(jnp.where(visible(N_META, -q_pos0), s_meta, NEG_INF), vtm_ref[0, cols[i], 0:N_META], -q_pos0),
                     (jnp.where(visible(tq, hide), s_diag, NEG_INF), vt_ref[td, cols[i], :], jnp.int32(0))])

        lam = _lambda(lam_ref, lam_init)
        for i in heads:
            acc = acc_ref[i]
            o_t = acc[0:D_V_DIM, :] * (1.0 / acc[D_V_DIM:D_V_DIM + 1, :])
            o = (o_t[:, 0:tq] - lam * o_t[:, tq:2 * tq]).T
            o_ref[:, cols[i]] = _rms(o, gain_ref[:, cols[i]]) * (1.0 - lam_init)


def _diff_prompt(qn, kb, vt, lam_pack, gain, slopes, lam_init):
    r = qn.shape[0]
    meta_row = _rows()[2]
    nq = SEQ // ROW_BLOCK
    meta_blk = meta_row // ROW_BLOCK
    gw = HEAD_GROUP * LANES
    assert meta_row % N_META == 0 and D_V_DIM == LANES and D_HEADS % HEAD_GROUP == 0

    def qmap(b, h, qi, *_):
        return (jnp.where(qi == 0, jnp.where(b == 0, meta_blk, b * nq), b * nq + qi - 1), h)

    grid_spec = pltpu.PrefetchScalarGridSpec(
        num_scalar_prefetch=1,
        grid=(BATCH, D_HEADS // HEAD_GROUP, nq + 1),
        in_specs=[pl.BlockSpec((ROW_BLOCK, gw), qmap),
                  pl.BlockSpec((SEQ, gw), lambda b, h, qi, *_: (b, h)),
                  pl.BlockSpec((nq, gw, ROW_BLOCK), lambda b, h, qi, *_: (b, h, 0)),
                  pl.BlockSpec((N_META, gw), lambda b, h, qi, *_: (meta_row // N_META, h)),
                  pl.BlockSpec((1, gw, ROW_BLOCK), lambda b, h, qi, *_: (meta_blk, h, 0)),
                  pl.BlockSpec((SUBLANES, LANES), lambda b, h, qi, *_: (0, 0)),
                  pl.BlockSpec((1, gw), lambda b, h, qi, *_: (0, h))],
        out_specs=pl.BlockSpec((ROW_BLOCK, gw), qmap),
        scratch_shapes=[pltpu.VMEM((HEAD_GROUP, ROW_BLOCK, 2 * ROW_BLOCK), F32),
                        pltpu.VMEM((HEAD_GROUP, 1, 2 * ROW_BLOCK), F32),
                        pltpu.VMEM((HEAD_GROUP, D_V_DIM + ONES_ROWS, 2 * ROW_BLOCK), F32)])
    return pl.pallas_call(
        functools.partial(_diff_prompt_kernel, lam_init=lam_init),
        grid_spec=grid_spec,
        out_shape=jax.ShapeDtypeStruct((r, D_V_WIDTH), F32),
        compiler_params=_params("arbitrary", "arbitrary", "arbitrary"),
        name="diff_prompt",
    )(slopes, qn, kb, vt, kb, vt, lam_pack, gain.reshape(1, -1))


def _diff_sample_kernel(pt_ref, *refs, lam_init):
    del pt_ref
    pp = PAGES_PER_STEP
    k_pages, v_pages = refs[0:pp], refs[pp:2 * pp]
    (q_ref, kn_ref, vn_ref, lam_ref, gain_ref, hd_in, o_ref,
     qb_ref, kp_ref, vp_ref, m_ref, acc_ref) = refs[2 * pp:]
    del hd_in
    g = pl.program_id(1)
    qrows = 2 * DEC_SEQ
    half = D_HEADS * DEC_SEQ
    slopes = [2.0 ** (-8.0 * (h + 1) / D_HEADS) * LOG2E for h in range(D_HEADS)]

    def step(h, s, v):
        m_old = m_ref[h]
        m_new = jnp.maximum(m_old, jnp.max(s, axis=1, keepdims=True))
        p = jnp.exp2(s - m_new).astype(BF16)
        v_ones = jnp.concatenate([v, jnp.ones_like(v)], axis=1)
        acc_ref[h] = jnp.exp2(m_old - m_new) * acc_ref[h] + _dot(p, v_ones)
        m_ref[h] = m_new

    @pl.when(g == 0)
    def _():
        lo = lax.broadcasted_iota(jnp.int32, (DEC_SEQ, LANES), 1) < D_QK_DIM
        for h in range(D_HEADS):
            x = q_ref[:, h * LANES:(h + 1) * LANES]
            qb_ref[h] = jnp.concatenate([jnp.where(lo, x, 0.0), jnp.where(lo, 0.0, x)], axis=0).astype(BF16)
        m_ref[...] = jnp.full_like(m_ref, NEG_INF)
        acc_ref[...] = jnp.zeros_like(acc_ref)

    k_pos = (lax.broadcasted_iota(jnp.int32, (1, pp * PAGE_SIZE), 1) + (g * (pp * PAGE_SIZE) - PAST_LEN)).astype(F32)
    for h in range(D_HEADS):
        rows_h = pl.ds(h, PAGE_SIZE, stride=D_HEADS)
        kh = jnp.concatenate([k_pages[p][rows_h, :] for p in range(pp)], axis=0).astype(BF16)
        vh = jnp.concatenate([v_pages[p][rows_h, :] for p in range(pp)], axis=0).astype(BF16)
        step(h, _dot_nt(qb_ref[h], kh) + slopes[h] * k_pos, vh)

    @pl.when(g == pl.num_programs(1) - 1)
    def _():
        for src, dst in ((kn_ref, kp_ref), (vn_ref, vp_ref)):
            dst[...] = jnp.zeros_like(dst)
            dst[0:half, :] = src[...].reshape(half, LANES)
        kn = kp_ref[...].astype(BF16)
        vn = vp_ref[...].astype(BF16)
        c = lax.broadcasted_iota(jnp.int32, (qrows, LANES), 1)
        r_tok = lax.broadcasted_iota(jnp.int32, (qrows, LANES), 0) % DEC_SEQ
        c_tok = c // D_HEADS
        lam = _lambda(lam_ref, lam_init)
        for h in range(D_HEADS):
            ok = (c % D_HEADS == h) & (c_tok <= r_tok)
            s = _dot_nt(qb_ref[h], kn) + slopes[h] * c_tok.astype(F32)
            step(h, jnp.where(ok, s, NEG_INF), vn)
            acc = acc_ref[h]
            o = acc[:, 0:LANES] * (1.0 / acc[:, LANES:2 * LANES])
            oh = o[0:DEC_SEQ, :] - lam * o[DEC_SEQ:qrows, :]
            cs = slice(h * LANES, (h + 1) * LANES)
            o_ref[:, cs] = _rms(oh, gain_ref[:, cs]) * (1.0 - lam_init)


def _diff_sample(layer, cache_k, cache_v, page_table, qn, kf, vf, lam_pack, gain, hd, lam_init):
    assert DEC_SEQ == SUBLANES and D_HEADS == SUBLANES and D_V_DIM == LANES
    assert D_HEADS * DEC_SEQ <= LANES and PAST_LEN % (PAGE_SIZE * PAGES_PER_STEP) == 0
    n_prompt = _rows()[0]
    blk0 = n_prompt // DEC_SEQ
    pp = PAGES_PER_STEP
    width = D_HEADS * LANES
    rows = 2 * D_HEADS * DEC_SEQ

    def as_rows(cache):
        return cache.reshape(cache.shape[0], cache.shape[1], PAGE_SIZE * D_HEADS, LANES)

    def page_spec(p):
        return pl.BlockSpec((None, None, PAGE_SIZE * D_HEADS, LANES),
                            lambda b, g, pt: (layer, pt[b, g * pp + p], 0, 0))

    row_spec = pl.BlockSpec((DEC_SEQ, width), lambda b, g, pt: (blk0 + b, 0))
    new_spec = pl.BlockSpec((DEC_SEQ, D_HEADS, LANES), lambda b, g, pt: (b, 0, 0))
    grid_spec = pltpu.PrefetchScalarGridSpec(
        num_scalar_prefetch=1,
        grid=(DEC_BATCH, PAST_LEN // (PAGE_SIZE * pp)),
        in_specs=[page_spec(p) for p in range(pp)] * 2
                 + [row_spec, new_spec, new_spec,
                    pl.BlockSpec((SUBLANES, LANES), lambda b, g, pt: (0, 0)),
                    pl.BlockSpec((1, width), lambda b, g, pt: (0, 0)),
                    pl.BlockSpec(memory_space=pl.ANY)],
        out_specs=row_spec,
        scratch_shapes=[pltpu.VMEM((D_HEADS, 2 * DEC_SEQ, LANES), BF16),
                        pltpu.VMEM((LANES, LANES), F32), pltpu.VMEM((LANES, LANES), F32),
                        pltpu.VMEM((D_HEADS, 2 * DEC_SEQ, 1), F32), pltpu.VMEM((D_HEADS, 2 * DEC_SEQ, 2 * LANES), F32)])
    n_in = 2 * pp + 6
    return pl.pallas_call(
        functools.partial(_diff_sample_kernel, lam_init=lam_init),
        grid_spec=grid_spec,
        out_shape=jax.ShapeDtypeStruct(hd.shape, F32),
        input_output_aliases={n_in: 0},
        compiler_params=_params("arbitrary", "arbitrary"),
        name="diff_sample",
    )(page_table, *([as_rows(cache_k)] * pp), *([as_rows(cache_v)] * pp), qn, kf, vf, lam_pack,
      gain.reshape(1, -1), hd)


def kernel(x_prompt, x_sample, cache_k, cache_v, state_C, state_n, state_m, page_table, meta_tokens, norm_mix, w_in, b_if, q_gain, k_gain, lambda_q1, lambda_k1, lambda_q2, lambda_k2, mlstm_norm, diff_norm, w_pm, w_pd, w_out, norm_ffn, w_gu, w_down):
    assert M_HEAD_DIM ** -0.5 == 2.0 ** round(math.log2(M_HEAD_DIM ** -0.5))
    n_prompt, n_sample, meta_row, r = _rows()
    x = jnp.concatenate([x_prompt.reshape(n_prompt, D_MODEL), x_sample.reshape(n_sample, D_MODEL),
                         meta_tokens.astype(F32), jnp.zeros((r - meta_row - N_META, D_MODEL), F32)], axis=0)
    slopes = 2.0 ** (-8.0 * jnp.arange(1, D_HEADS + 1, dtype=F32) / D_HEADS)
    w_in_t = jnp.swapaxes(w_in, 1, 2)
    g0 = 4 * M_WIDTH
    g1 = g0 + 2 * M_HEADS

    k_prompt = v_prompt = c_sample = n_sample_st = None
    meta_kv = []
    outs = [[] for _ in range(6)]
    for l in range(DEPTH):
        lam_init = 0.8 - 0.6 * math.exp(-0.3 * l)
        xn = _rmsnorm(x, norm_mix[l], "norm_mix")
        proj_m = _matmul_nt(xn, w_in_t, l, 0, g0, "proj_mlstm")
        graw = _matmul_nt(xn, w_in_t, l, g0, 2 * M_HEADS, "proj_gates")
        proj_d = _matmul_nt(xn, w_in_t, l, g1, w_in_t.shape[1] - g1, "proj_diff")

        hm, c_p, n_p, m_p = _mlstm_prompt(proj_m, proj_m, graw, graw.T, b_if[l], mlstm_norm[l])
        gs = graw[n_prompt:n_prompt + n_sample].reshape(DEC_BATCH, DEC_SEQ, 2 * M_HEADS)
        gs = jnp.pad(gs, ((0, 0), (0, SAMPLE_CHUNK - DEC_SEQ), (0, 0)))
        hm, c_sample, n_sample_st, m_s = _mlstm_sample(
            proj_m, proj_m, gs, gs.transpose(0, 2, 1), b_if[l], mlstm_norm[l], l, state_C.astype(F32),
            state_n.astype(F32), state_m[l].astype(F32), hm, c_sample, n_sample_st)

        qn, kb, vt, k_prompt, v_prompt, kf, vf = _qk_norm(proj_d, q_gain[l], k_gain[l], l, k_prompt, v_prompt)
        lam_pack = jnp.pad(jnp.stack([lambda_q1[l], lambda_k1[l], lambda_q2[l], lambda_k2[l]]).astype(F32),
                           ((0, SUBLANES - 4), (0, LANES - D_QK_DIM)))
        hd = _diff_prompt(qn, kb, vt, lam_pack, diff_norm[l], slopes, lam_init)
        hd = _diff_sample(l, cache_k, cache_v, page_table, qn, kf, vf, lam_pack, diff_norm[l], hd, lam_init)

        z = _merge(hm, hd, w_pm, w_pd, l, proj_d, 2 * D_QK_WIDTH + D_V_WIDTH)
        x = _matmul_res(z, w_out, l, x, "out_proj", MATMUL_ROWS)
        act = _swiglu(_rmsnorm(x, norm_ffn[l], "norm_ffn"), w_gu, l)
        x = _matmul_res(act, w_down, l, x, "ffn_down", MATMUL_ROWS // 2)

        head_shape = (D_HEADS, D_V_DIM)
        meta_kv.append((kf[n_sample:n_sample + N_META], vf[n_sample:n_sample + N_META]))
        vals = (c_p, n_p, m_p[:, :, 0],
                kf[:n_sample].reshape(DEC_BATCH, DEC_SEQ, *head_shape),
                vf[:n_sample].reshape(DEC_BATCH, DEC_SEQ, *head_shape), m_s[:, :, 0])
        for o, v in zip(outs, vals):
            o.append(v)

    for l, (km, vm) in enumerate(meta_kv):
        k_prompt = k_prompt.at[l, :, :N_META].set(jnp.broadcast_to(km, (BATCH,) + km.shape))
        v_prompt = v_prompt.at[l, :, :N_META].set(jnp.broadcast_to(vm, (BATCH,) + vm.shape))
    y_prompt = x[:n_prompt].reshape(BATCH, SEQ, D_MODEL)
    y_sample = x[n_prompt:n_prompt + n_sample].reshape(DEC_BATCH, DEC_SEQ, D_MODEL)
    c_p, n_p, m_p, k_s, v_s, m_s = (jnp.stack(o) for o in outs)
    return (y_prompt, y_sample, k_prompt, v_prompt, c_p, n_p, m_p, k_s, v_s, c_sample, n_sample_st, m_s)
```

```python
import functools
import math

import jax
import jax.numpy as jnp
from jax import lax
from jax.experimental import pallas as pl
from jax.experimental.pallas import tpu as pltpu

D_MODEL = 2048
BATCH = 2
SEQ = 4096
DEPTH = 2
DEC_BATCH = 32
DEC_SEQ = 8
PAST_LEN = 8192
PAGE_SIZE = 128
N_META = 16
M_HEADS = 4
M_HEAD_DIM = 256
M_WIDTH = M_HEADS * M_HEAD_DIM
D_HEADS = 8
D_QK_DIM = 64
D_V_DIM = 2 * D_QK_DIM
D_QK_WIDTH = D_HEADS * 2 * D_QK_DIM
D_V_WIDTH = D_HEADS * D_V_DIM
EPS = 1e-6

LANES = 128
SUBLANES = 8
ROW_BLOCK = 256
MATMUL_ROWS = 1100
K_SCALE = M_HEAD_DIM ** -0.5
SAMPLE_CHUNK = 128
KV_UNROLL = (4, 2, 1)
HEAD_GROUP = 4
PAGES_PER_STEP = 16
VMEM_LIMIT = 56 * 1024 * 1024

F32 = jnp.float32
BF16 = jnp.bfloat16
NEG_INF = float("-inf")
LOG2E = math.log2(math.e)
Q_SCALE = D_QK_DIM ** -0.5 * LOG2E
ONES_ROWS = 16


def _ffn_dim():
    return ((8 * D_MODEL // 3 + 255) // 256) * 256


def _rows():
    n_prompt = BATCH * SEQ
    n_sample = DEC_BATCH * DEC_SEQ
    assert n_prompt % ROW_BLOCK == 0 and n_sample % ROW_BLOCK == 0 and N_META <= ROW_BLOCK
    return n_prompt, n_sample, n_prompt + n_sample, n_prompt + n_sample + ROW_BLOCK


def _row_tile(r):
    return 512 if r % 512 == 0 else ROW_BLOCK


def _tile(n, cap, mult=16):
    for t in range(min(cap, n) // mult * mult, 0, -mult):
        if n % t == 0:
            return t
    raise ValueError((n, cap, mult))


def _col_tile(n):
    for t in (1024, 512, 256, 128):
        if n % t == 0:
            return t
    raise ValueError(n)


def _params(*sem):
    return pltpu.CompilerParams(dimension_semantics=sem, vmem_limit_bytes=VMEM_LIMIT)


def _dot(a, b):
    return jnp.dot(a, b, preferred_element_type=F32)


def _dot_nt(a, b):
    return lax.dot_general(a, b, (((1,), (1,)), ((), ())), preferred_element_type=F32)


def _dot_tn(a, b):
    return lax.dot_general(a, b, (((0,), (0,)), ((), ())), preferred_element_type=F32)


def _sigmoid(x):
    return 1.0 / (1.0 + jnp.exp(-x))


def _rms(x, gain):
    ms = jnp.mean(x * x, axis=-1, keepdims=True)
    return x * lax.rsqrt(ms + EPS) * gain


def _rmsnorm_kernel(x_ref, g_ref, o_ref):
    o_ref[...] = _rms(x_ref[...], g_ref[...]).astype(o_ref.dtype)


def _rmsnorm(x, gain, name):
    r, d = x.shape
    tm = _tile(r, MATMUL_ROWS)
    return pl.pallas_call(
        _rmsnorm_kernel,
        grid=(r // tm,),
        in_specs=[pl.BlockSpec((tm, d), lambda i: (i, 0)), pl.BlockSpec((1, d), lambda i: (0, 0))],
        out_specs=pl.BlockSpec((tm, d), lambda i: (i, 0)),
        out_shape=jax.ShapeDtypeStruct((r, d), BF16),
        compiler_params=_params("parallel"),
        name=name,
    )(x, gain.reshape(1, d))


def _weight_spec(w, layer, tn, col0):
    if w.ndim == 3:
        return pl.BlockSpec((None, w.shape[1], tn), lambda j, i: (layer, 0, col0 + j))
    return pl.BlockSpec((w.shape[0], tn), lambda j, i: (0, col0 + j))


def _cast_weights(w_refs, wb_refs):
    @pl.when(pl.program_id(1) == 0)
    def _():
        for w_ref, wb_ref in zip(w_refs, wb_refs):
            wb_ref[...] = w_ref[...].astype(BF16)


def _matmul_kernel(a_ref, w_ref, o_ref, wb_ref):
    _cast_weights([w_ref], [wb_ref])
    o_ref[...] = _dot(a_ref[...], wb_ref[...]).astype(o_ref.dtype)


def _matmul(a, w, layer, n, out_dtype, name, tn=None):
    r, k = a.shape
    tm, tn = _tile(r, MATMUL_ROWS), tn or _col_tile(n)
    return pl.pallas_call(
        _matmul_kernel,
        grid=(n // tn, r // tm),
        in_specs=[pl.BlockSpec((tm, k), lambda j, i: (i, 0)), _weight_spec(w, layer, tn, 0)],
        out_specs=pl.BlockSpec((tm, tn), lambda j, i: (i, j)),
        out_shape=jax.ShapeDtypeStruct((r, n), out_dtype),
        scratch_shapes=[pltpu.VMEM((k, tn), BF16)],
        compiler_params=_params("parallel", "arbitrary"),
        name=name,
    )(a, w)


def _matmul_nt_kernel(a_ref, w_ref, o_ref, wb_ref):
    _cast_weights([w_ref.at[0]], [wb_ref])
    o_ref[...] = _dot_nt(a_ref[...], wb_ref[...]).astype(o_ref.dtype)


def _matmul_nt(a, w_t, layer, row0, n, name):
    r, k = a.shape
    tm, tn = _tile(r, MATMUL_ROWS), _tile(n, 1024, SUBLANES)
    assert row0 % SUBLANES == 0
    return pl.pallas_call(
        _matmul_nt_kernel,
        grid=(n // tn, r // tm),
        in_specs=[pl.BlockSpec((tm, k), lambda j, i: (i, 0)),
                  pl.BlockSpec((pl.Element(1), pl.Element(tn), pl.Element(k)),
                               lambda j, i: (layer, pl.multiple_of(row0 + j * tn, SUBLANES), 0))],
        out_specs=pl.BlockSpec((tm, tn), lambda j, i: (i, j)),
        out_shape=jax.ShapeDtypeStruct((r, n), F32),
        scratch_shapes=[pltpu.VMEM((tn, k), BF16)],
        compiler_params=_params("parallel", "arbitrary"),
        name=name,
    )(a, w_t)


def _swiglu_kernel(a_ref, wg_ref, wu_ref, o_ref, wgb_ref, wub_ref):
    _cast_weights([wg_ref, wu_ref], [wgb_ref, wub_ref])
    a = a_ref[...]
    g = _dot(a, wgb_ref[...])
    u = _dot(a, wub_ref[...])
    o_ref[...] = (g * _sigmoid(g) * u).astype(o_ref.dtype)


def _swiglu(a, w_gu, layer):
    r, k = a.shape
    f = w_gu.shape[2] // 2
    tm, tn = _tile(r, MATMUL_ROWS), min(512, _col_tile(f))
    nj = f // tn
    return pl.pallas_call(
        _swiglu_kernel,
        grid=(nj, r // tm),
        in_specs=[pl.BlockSpec((tm, k), lambda j, i: (i, 0)),
                  _weight_spec(w_gu, layer, tn, 0), _weight_spec(w_gu, layer, tn, nj)],
        out_specs=pl.BlockSpec((tm, tn), lambda j, i: (i, j)),
        out_shape=jax.ShapeDtypeStruct((r, f), BF16),
        scratch_shapes=[pltpu.VMEM((k, tn), BF16)] * 2,
        compiler_params=_params("parallel", "arbitrary"),
        name="ffn_up",
    )(a, w_gu, w_gu)


def _matmul_res_kernel(a_ref, w_ref, r_ref, o_ref, wb_ref):
    _cast_weights([w_ref], [wb_ref])
    o_ref[...] = r_ref[...] + _dot(a_ref[...], wb_ref[...])


def _matmul_res(a, w, layer, res, name, rows, tn_cap):
    r, k = a.shape
    n = w.shape[2]
    tm, tn = _tile(r, rows), min(tn_cap, _col_tile(n))
    return pl.pallas_call(
        _matmul_res_kernel,
        grid=(n // tn, r // tm),
        in_specs=[pl.BlockSpec((tm, k), lambda j, i: (i, 0)), _weight_spec(w, layer, tn, 0),
                  pl.BlockSpec((tm, tn), lambda j, i: (i, j))],
        out_specs=pl.BlockSpec((tm, tn), lambda j, i: (i, j)),
        out_shape=jax.ShapeDtypeStruct((r, n), F32),
        scratch_shapes=[pltpu.VMEM((k, tn), BF16)],
        compiler_params=_params("parallel", "arbitrary"),
        name=name,
    )(a, w, res)


def _merge_kernel(hm_ref, hd_ref, wpm_ref, wpd_ref, gm_ref, gd_ref, o_ref, wpmb_ref, wpdb_ref):
    _cast_weights([wpm_ref, wpd_ref], [wpmb_ref, wpdb_ref])
    zm = _dot(hm_ref[...].astype(BF16), wpmb_ref[...])
    zd = _dot(hd_ref[...].astype(BF16), wpdb_ref[...])
    o_ref[...] = (_sigmoid(gm_ref[...]) * zm + _sigmoid(gd_ref[...]) * zd).astype(o_ref.dtype)


def _merge(hm, hd, w_pm, w_pd, layer, gates, gate_col):
    r = hm.shape[0]
    n = w_pm.shape[2]
    tm = _tile(r, MATMUL_ROWS // 4)
    assert gate_col % LANES == 0

    def gate_spec(col0):
        return pl.BlockSpec((pl.Element(tm), pl.Element(n)), lambda j, i: (pl.multiple_of(i * tm, tm), col0))

    def resident(w):
        return pl.BlockSpec((None, w.shape[1], n), lambda j, i: (layer, 0, 0), pipeline_mode=pl.Buffered(1))

    return pl.pallas_call(
        _merge_kernel,
        grid=(1, r // tm),
        in_specs=[pl.BlockSpec((tm, M_WIDTH), lambda j, i: (i, 0)),
                  pl.BlockSpec((tm, D_V_WIDTH), lambda j, i: (i, 0)),
                  resident(w_pm), resident(w_pd), gate_spec(gate_col), gate_spec(gate_col + n)],
        out_specs=pl.BlockSpec((tm, n), lambda j, i: (i, 0)),
        out_shape=jax.ShapeDtypeStruct((r, n), BF16),
        scratch_shapes=[pltpu.VMEM((M_WIDTH, n), BF16), pltpu.VMEM((D_V_WIDTH, n), BF16)],
        compiler_params=_params("arbitrary", "arbitrary"),
        name="merge",
    )(hm, hd, w_pm, w_pd, gates, gates)


def _half_norm(x, gain):
    lo = lax.broadcasted_iota(jnp.int32, x.shape, 1) < D_QK_DIM
    x2 = x * x
    s_lo = jnp.sum(jnp.where(lo, x2, 0.0), axis=1, keepdims=True)
    s_hi = jnp.sum(jnp.where(lo, 0.0, x2), axis=1, keepdims=True)
    ms = jnp.where(lo, s_lo, s_hi) * (1.0 / D_QK_DIM)
    return x * lax.rsqrt(ms + EPS) * gain


def _qk_norm_kernel(q_ref, k_ref, v_ref, qg_ref, kg_ref, *refs, n_prompt_tiles):
    qo_ref, kb_ref, vt_ref, kp_ref, vp_ref, kt_ref, vt8_ref = refs[-7:]
    i = pl.program_id(0)
    kn = []
    for h in range(D_HEADS):
        sl = slice(h * LANES, (h + 1) * LANES)
        qo_ref[:, sl] = _half_norm(q_ref[:, sl], qg_ref[...]) * Q_SCALE
        kn.append(_half_norm(k_ref[:, sl], kg_ref[...]))
        kb_ref[:, sl] = kn[h].astype(BF16)
    for j in range(vt_ref.shape[0]):
        vt_ref[j] = v_ref[j * ROW_BLOCK:(j + 1) * ROW_BLOCK, :].T.astype(BF16)

    @pl.when(i < n_prompt_tiles)
    def _():
        for h in range(D_HEADS):
            kp_ref[0, 0, :, h, :] = kn[h]
            vp_ref[0, 0, :, h, :] = v_ref[:, h * LANES:(h + 1) * LANES]

    @pl.when(i == n_prompt_tiles)
    def _():
        for h in range(D_HEADS):
            kt_ref[:, h, :] = kn[h]
            vt8_ref[:, h, :] = v_ref[:, h * LANES:(h + 1) * LANES]


def _qk_norm(qkv, q_gain, k_gain, layer, k_out, v_out):
    r = qkv.shape[0]
    w = D_QK_WIDTH
    n_prompt = _rows()[0]
    tm = r - n_prompt
    assert n_prompt % tm == 0 and SEQ % tm == 0 and tm % ROW_BLOCK == 0
    n_prompt_tiles, tiles_per_seq = n_prompt // tm, SEQ // tm
    spec = lambda c: pl.BlockSpec((tm, w), lambda i: (i, c))
    gspec = pl.BlockSpec((1, LANES), lambda i: (0, 0))

    def prompt_map(i):
        t = jnp.minimum(i, n_prompt_tiles - 1)
        return (layer, t // tiles_per_seq, N_META + (t % tiles_per_seq) * tm, 0, 0)

    pspec = pl.BlockSpec(tuple(pl.Element(s) for s in (1, 1, tm, D_HEADS, LANES)), prompt_map)
    tspec = pl.BlockSpec((tm, D_HEADS, LANES), lambda i: (0, 0, 0))
    out5 = jax.ShapeDtypeStruct((DEPTH, BATCH, N_META + SEQ, D_HEADS, LANES), F32)
    tail = jax.ShapeDtypeStruct((tm, D_HEADS, LANES), F32)
    carried = [] if k_out is None else [k_out, v_out]
    n_in = 5 + len(carried)
    return pl.pallas_call(
        functools.partial(_qk_norm_kernel, n_prompt_tiles=n_prompt_tiles),
        grid=(r // tm,),
        in_specs=[spec(0), spec(1), spec(2), gspec, gspec] + [pl.BlockSpec(memory_space=pl.ANY)] * len(carried),
        out_specs=[spec(0), spec(0), pl.BlockSpec((tm // ROW_BLOCK, D_V_WIDTH, ROW_BLOCK), lambda i: (i, 0, 0)),
                   pspec, pspec, tspec, tspec],
        out_shape=[jax.ShapeDtypeStruct((r, w), F32), jax.ShapeDtypeStruct((r, w), BF16),
                   jax.ShapeDtypeStruct((r // ROW_BLOCK, D_V_WIDTH, ROW_BLOCK), BF16), out5, out5, tail, tail],
        input_output_aliases={} if k_out is None else {n_in - 2: 3, n_in - 1: 4},
        compiler_params=_params("arbitrary"),
        name="qk_norm",
    )(qkv, qkv, qkv, jnp.tile(q_gain, 2).reshape(1, LANES), jnp.tile(k_gain, 2).reshape(1, LANES), *carried)


def _split3(x):
    hi = x.astype(BF16)
    r1 = x - hi.astype(F32)
    mid = r1.astype(BF16)
    lo = (r1 - mid.astype(F32)).astype(BF16)
    return hi, mid, lo


def _log_sigmoid(x):
    return jnp.minimum(x, 0.0) - jnp.log1p(jnp.exp(-jnp.abs(x)))


def _mlstm_gates(graw_col, graw_row, bias_col, bias_row, n_valid):
    l = graw_col.shape[0]
    g_col = graw_col + bias_col
    g_row = graw_row + bias_row
    is_f_col = lax.broadcasted_iota(jnp.int32, g_col.shape, 1) >= M_HEADS
    is_f_row = lax.broadcasted_iota(jnp.int32, g_row.shape, 0) >= M_HEADS
    ok_col = lax.broadcasted_iota(jnp.int32, g_col.shape, 0) < n_valid
    ok_row = lax.broadcasted_iota(jnp.int32, g_row.shape, 1) < n_valid
    lf_col = jnp.where(is_f_col & ok_col, _log_sigmoid(g_col), 0.0)
    lf_row = jnp.where(is_f_row & ok_row, _log_sigmoid(g_row), 0.0)
    ig_col = jnp.where(ok_col, g_col, NEG_INF)
    ig_row = jnp.where(ok_row, g_row, NEG_INF)
    rr = lax.broadcasted_iota(jnp.int32, (l, l), 0)
    cc = lax.broadcasted_iota(jnp.int32, (l, l), 1)
    tri_l = (cc <= rr).astype(BF16)
    tri_u = (rr <= cc).astype(BF16)
    b_col = sum(_dot(tri_l, p) for p in _split3(lf_col))
    b_row = sum(_dot(p, tri_u) for p in _split3(lf_row))
    return ig_col, b_col, ig_row, b_row, cc <= rr


def _mlstm_chunk(q, k, v, ig_col, b_col, ig_row, b_row, causal, c_st, n_st, m_st):
    l = q.shape[0]
    d = b_col - b_row + ig_row
    d = jnp.where(causal, d, NEG_INF)
    inter = b_col + m_st
    m_t = jnp.maximum(inter, jnp.max(d, axis=1, keepdims=True))
    w = jnp.exp(d - m_t)
    s = _dot_nt(q, k) * w
    a = jnp.exp(inter - m_t)
    num = _dot(s.astype(BF16), v) + a * _dot_nt(q, c_st.astype(BF16))
    den = jnp.sum(s, axis=1, keepdims=True) + a * jnp.sum(q.astype(F32) * n_st, axis=1, keepdims=True)
    h = num * (1.0 / jnp.maximum(jnp.abs(den), jnp.exp(-m_t)))

    b_last = b_col[l - 1:l, :]
    g_col = b_last - b_col + ig_col
    g_row = b_last - b_row + ig_row
    m_new = jnp.maximum(b_last + m_st, jnp.max(g_row, axis=1, keepdims=True))
    ws_col = jnp.exp(g_col - m_new)
    ws_row = jnp.exp(g_row - m_new)
    decay = jnp.exp(b_last + m_st - m_new)
    wv = (v.astype(F32) * ws_col).astype(BF16)
    c_new = decay * c_st + _dot_tn(wv, k)
    n_new = decay * n_st + _dot(ws_row.astype(BF16), k)
    return h, c_new, n_new, m_new


def _mlstm_heads(q_of, k_of, v_of, om_of, gates, gain_ref, c_ref, n_ref, m_ref, h_out):
    ig_col, b_col, ig_row, b_row, causal = gates
    for h in range(M_HEADS):
        f = M_HEADS + h
        hs = slice(h * M_HEAD_DIM, (h + 1) * M_HEAD_DIM)
        out, c_new, n_new, m_new = _mlstm_chunk(
            q_of(hs), k_of(hs), v_of(hs),
            ig_col[:, h:h + 1], b_col[:, f:f + 1], ig_row[h:h + 1, :], b_row[f:f + 1, :], causal,
            c_ref[0, h], n_ref[0, h:h + 1, :], m_ref[0, h:h + 1, 0:1])
        c_ref[0, h] = c_new
        n_ref[0, h:h + 1, :] = n_new
        m_ref[0, h:h + 1, :] = jnp.broadcast_to(m_new, (1, LANES))
        h_out(hs, _rms(out, gain_ref[:, hs]) * _sigmoid(om_of(hs)))


def _mlstm_prompt_kernel(q_ref, k_ref, v_ref, om_ref, gc_ref, gr_ref, bc_ref, br_ref, gain_ref,
                         h_ref, c_ref, n_ref, m_ref):
    c = pl.program_id(1)

    @pl.when(c == 0)
    def _():
        c_ref[...] = jnp.zeros_like(c_ref)
        n_ref[...] = jnp.zeros_like(n_ref)
        m_ref[...] = jnp.zeros_like(m_ref)

    n_valid = jnp.where(c == 0, N_META, ROW_BLOCK)
    gates = _mlstm_gates(gc_ref[...], gr_ref[...], bc_ref[...], br_ref[...], n_valid)

    def h_out(hs, val):
        h_ref[:, hs] = val

    _mlstm_heads(lambda hs: q_ref[:, hs].astype(BF16), lambda hs: (k_ref[:, hs] * K_SCALE).astype(BF16),
                 lambda hs: v_ref[:, hs].astype(BF16), lambda hs: om_ref[:, hs],
                 gates, gain_ref, c_ref, n_ref, m_ref, h_out)


def _mlstm_prompt(qkv, gates, g_col, g_row, b_if, gain):
    r = qkv.shape[0]
    n_prompt, _, meta_row, _ = _rows()
    nc = SEQ // ROW_BLOCK
    meta_blk = meta_row // ROW_BLOCK

    def rb(b, c):
        return jnp.where(c == 0, meta_blk, b * nc + c - 1)

    def rb_out(b, c):
        return jnp.where((c == 0) & (b > 0), b * nc, rb(b, c))

    spec = lambda col: pl.BlockSpec((ROW_BLOCK, M_WIDTH), lambda b, c: (rb(b, c), col))
    full = lambda shape: pl.BlockSpec(shape, lambda b, c: (0,) * len(shape))
    return pl.pallas_call(
        _mlstm_prompt_kernel,
        grid=(BATCH, nc + 1),
        in_specs=[spec(0), spec(1), spec(2), spec(3),
                  pl.BlockSpec((ROW_BLOCK, 2 * M_HEADS), lambda b, c: (rb(b, c), 0)),
                  pl.BlockSpec((2 * M_HEADS, ROW_BLOCK), lambda b, c: (0, rb(b, c))),
                  full((1, 2 * M_HEADS)), full((2 * M_HEADS, 1)), full((1, M_WIDTH))],
        out_specs=[pl.BlockSpec((ROW_BLOCK, M_WIDTH), lambda b, c: (rb_out(b, c), 0)),
                   pl.BlockSpec((1, M_HEADS, M_HEAD_DIM, M_HEAD_DIM), lambda b, c: (b, 0, 0, 0)),
                   pl.BlockSpec((1, M_HEADS, M_HEAD_DIM), lambda b, c: (b, 0, 0)),
                   pl.BlockSpec((1, M_HEADS, LANES), lambda b, c: (b, 0, 0))],
        out_shape=[jax.ShapeDtypeStruct((r, M_WIDTH), F32),
                   jax.ShapeDtypeStruct((BATCH, M_HEADS, M_HEAD_DIM, M_HEAD_DIM), F32),
                   jax.ShapeDtypeStruct((BATCH, M_HEADS, M_HEAD_DIM), F32),
                   jax.ShapeDtypeStruct((BATCH, M_HEADS, LANES), F32)],
        compiler_params=_params("arbitrary", "arbitrary"),
        name="mlstm_prompt",
    )(qkv, qkv, qkv, gates, g_col, g_row, b_if.reshape(1, -1), b_if.reshape(-1, 1), gain.reshape(1, -1))


def _mlstm_sample_kernel(q_ref, k_ref, v_ref, om_ref, gc_ref, gr_ref, bc_ref, br_ref, gain_ref,
                         c_in, n_in, m_in, *rest):
    h_ref, c_ref, n_ref, m_ref, qp, kp, vp = rest[-7:]
    c_ref[...] = c_in[...]
    n_ref[...] = n_in[...]
    m_ref[...] = m_in[...]
    for src, dst in ((q_ref, qp), (k_ref, kp), (v_ref, vp)):
        dst[...] = jnp.zeros_like(dst)
        dst[0:DEC_SEQ, :] = src[...]
    gates = _mlstm_gates(gc_ref[0], gr_ref[0], bc_ref[...], br_ref[...], DEC_SEQ)
    om = om_ref[...]

    def h_out(hs, val):
        h_ref[:, hs] = val[0:DEC_SEQ, :]

    def om_of(hs):
        return jnp.concatenate([om[:, hs], jnp.zeros((SAMPLE_CHUNK - DEC_SEQ, M_HEAD_DIM), F32)], axis=0)

    _mlstm_heads(lambda hs: qp[:, hs].astype(BF16), lambda hs: (kp[:, hs] * K_SCALE).astype(BF16),
                 lambda hs: vp[:, hs].astype(BF16), om_of,
                 gates, gain_ref, c_ref, n_ref, m_ref, h_out)


def _mlstm_sample(qkv, gates, g_col, g_row, b_if, gain, layer, state_c, state_n, st_m, hm, c_out, n_out):
    assert DEC_SEQ == SUBLANES
    n_prompt = _rows()[0]
    blk0 = n_prompt // DEC_SEQ
    spec = lambda col: pl.BlockSpec((DEC_SEQ, M_WIDTH), lambda b: (blk0 + b, col))
    full = lambda shape: pl.BlockSpec(shape, lambda b: (0,) * len(shape))
    st_spec = lambda shape: pl.BlockSpec((1,) + shape, lambda b: (b,) + (0,) * len(shape))
    layer_spec = lambda shape: pl.BlockSpec((None, 1) + shape, lambda b: (layer, b) + (0,) * len(shape))
    st_specs = [layer_spec((M_HEADS, M_HEAD_DIM, M_HEAD_DIM)), layer_spec((M_HEADS, M_HEAD_DIM)),
                st_spec((M_HEADS, LANES))]
    carried = [] if c_out is None else [c_out, n_out]
    aliases = {12: 0}
    if carried:
        aliases.update({13: 1, 14: 2})
    return pl.pallas_call(
        _mlstm_sample_kernel,
        grid=(DEC_BATCH,),
        in_specs=[spec(0), spec(1), spec(2), spec(3),
                  st_spec((SAMPLE_CHUNK, 2 * M_HEADS)), st_spec((2 * M_HEADS, SAMPLE_CHUNK)),
                  full((1, 2 * M_HEADS)), full((2 * M_HEADS, 1)), full((1, M_WIDTH))] + st_specs
                 + [pl.BlockSpec(memory_space=pl.ANY)] * (1 + len(carried)),
        out_specs=[spec(0)] + st_specs,
        out_shape=[jax.ShapeDtypeStruct(hm.shape, F32),
                   jax.ShapeDtypeStruct(state_c.shape, F32), jax.ShapeDtypeStruct(state_n.shape, F32),
                   jax.ShapeDtypeStruct((DEC_BATCH, M_HEADS, LANES), F32)],
        scratch_shapes=[pltpu.VMEM((SAMPLE_CHUNK, M_WIDTH), F32)] * 3,
        input_output_aliases=aliases,
        compiler_params=_params("arbitrary"),
        name="mlstm_sample",
    )(qkv, qkv, qkv, gates, g_col, g_row, b_if.reshape(1, -1), b_if.reshape(-1, 1), gain.reshape(1, -1),
      state_c, state_n, jnp.broadcast_to(st_m[:, :, None], (DEC_BATCH, M_HEADS, LANES)), hm, *carried)


def _lambda(lam_ref, lam_init):
    p = lam_ref[...]
    s1 = jnp.sum(p[0:1, :] * p[1:2, :], axis=1, keepdims=True)
    s2 = jnp.sum(p[2:3, :] * p[3:4, :], axis=1, keepdims=True)
    return jnp.exp(s1) - jnp.exp(s2) + lam_init


def _diff_prompt_kernel(slope_ref, q_ref, k_ref, vt_ref, km_ref, vtm_ref, lam_ref, gain_ref, o_ref,
                        base_ref, m_ref, acc_ref, *, lam_init):
    b = pl.program_id(0)
    hg = pl.program_id(1)
    is_meta = pl.program_id(2) == 0
    t = pl.program_id(2) - 1
    tq = ROW_BLOCK
    heads = range(HEAD_GROUP)
    cols = [slice(i * LANES, (i + 1) * LANES) for i in heads]
    slopes = [slope_ref[hg * HEAD_GROUP + i] * LOG2E for i in heads]

    @pl.when(jnp.logical_not(is_meta & (b > 0)))
    def _():
        lo = lax.broadcasted_iota(jnp.int32, (tq, LANES), 1) < D_QK_DIM
        q2 = []
        for i in heads:
            q = q_ref[:, cols[i]].astype(BF16)
            zero = jnp.zeros_like(q)
            q2.append(jnp.concatenate([jnp.where(lo, q, zero), jnp.where(lo, zero, q)], axis=0))
            base_ref[i] = slopes[i] * lax.broadcasted_iota(jnp.int32, (tq, 2 * tq), 0).astype(F32)
        q_pos0 = jnp.where(is_meta, 0, N_META + t * tq)
        ones = jnp.ones((ONES_ROWS, tq), BF16)
        m_ref[...] = jnp.full_like(m_ref, NEG_INF)
        acc_ref[...] = jnp.zeros_like(acc_ref)

        def visible(n_keys, k_off):
            key = lax.broadcasted_iota(jnp.int32, (n_keys, 2 * tq), 0) + k_off
            qry = lax.broadcasted_iota(jnp.int32, (n_keys, 2 * tq), 1)
            return key <= jnp.where(qry >= tq, qry - tq, qry)

        def step(i, parts):
            cs = [slopes[i] * k_off.astype(F32) for _, _, k_off in parts]
            m_old = m_ref[i]
            m_new = m_old
            for (s, _, _), c in zip(parts, cs):
                m_new = jnp.maximum(m_new, jnp.max(s, axis=0, keepdims=True) + c)
            acc = jnp.exp2(m_old - m_new) * acc_ref[i]
            for (s, vt, _), c in zip(parts, cs):
                p = jnp.exp2(s - (m_new - c)).astype(BF16)
                acc = acc + _dot(jnp.concatenate([vt, ones[:, 0:vt.shape[1]]], axis=0), p)
            acc_ref[i] = acc
            m_ref[i] = m_new

        def scores(kb, i):
            ks = pl.ds(pl.multiple_of(kb * tq, tq), tq)
            return _dot_nt(k_ref[ks, cols[i]], q2[i])

        def full_blocks(kbs):
            for i in heads:
                step(i, [(scores(kb, i) + base_ref[i], vt_ref[kb, cols[i], :], N_META + kb * tq - q_pos0)
                         for kb in kbs])

        n_full = jnp.where(is_meta, 0, t)
        done = 0
        for width in KV_UNROLL:
            n_iter = (n_full - done) // width

            def body(j, carry, width=width, done=done):
                full_blocks([done + j * width + u for u in range(width)])
                return carry

            lax.fori_loop(0, n_iter, body, 0)
            done = done + n_iter * width

        td = jnp.maximum(t, 0)
        hide = jnp.where(is_meta, 2 * tq, 0)
        for i in heads:
            s_meta = _dot_nt(km_ref[:, cols[i]], q2[i]) + base_ref[i, 0:N_META, :]
            s_diag = scores(td, i) + base_ref[i]
            step(i, [(jnp.where(visible(N_META, -q_pos0), s_meta, NEG_INF), vtm_ref[0, cols[i], 0:N_META], -q_pos0),
                     (jnp.where(visible(tq, hide), s_diag, NEG_INF), vt_ref[td, cols[i], :], jnp.int32(0))])

        lam = _lambda(lam_ref, lam_init)
        for i in heads:
            acc = acc_ref[i]
            o_t = acc[0:D_V_DIM, :] * (1.0 / acc[D_V_DIM:D_V_DIM + 1, :])
            o = (o_t[:, 0:tq] - lam * o_t[:, tq:2 * tq]).T
            o_ref[:, cols[i]] = _rms(o, gain_ref[:, cols[i]]) * (1.0 - lam_init)


def _diff_prompt(qn, kb, vt, lam_pack, gain, slopes, lam_init):
    r = qn.shape[0]
    meta_row = _rows()[2]
    nq = SEQ // ROW_BLOCK
    meta_blk = meta_row // ROW_BLOCK
    gw = HEAD_GROUP * LANES
    assert meta_row % N_META == 0 and D_V_DIM == LANES and D_HEADS % HEAD_GROUP == 0

    def qmap(b, h, qi, *_):
        return (jnp.where(qi == 0, jnp.where(b == 0, meta_blk, b * nq), b * nq + qi - 1), h)

    grid_spec = pltpu.PrefetchScalarGridSpec(
        num_scalar_prefetch=1,
        grid=(BATCH, D_HEADS // HEAD_GROUP, nq + 1),
        in_specs=[pl.BlockSpec((ROW_BLOCK, gw), qmap),
                  pl.BlockSpec((SEQ, gw), lambda b, h, qi, *_: (b, h)),
                  pl.BlockSpec((nq, gw, ROW_BLOCK), lambda b, h, qi, *_: (b, h, 0)),
                  pl.BlockSpec((N_META, gw), lambda b, h, qi, *_: (meta_row // N_META, h)),
                  pl.BlockSpec((1, gw, ROW_BLOCK), lambda b, h, qi, *_: (meta_blk, h, 0)),
                  pl.BlockSpec((SUBLANES, LANES), lambda b, h, qi, *_: (0, 0)),
                  pl.BlockSpec((1, gw), lambda b, h, qi, *_: (0, h))],
        out_specs=pl.BlockSpec((ROW_BLOCK, gw), qmap),
        scratch_shapes=[pltpu.VMEM((HEAD_GROUP, ROW_BLOCK, 2 * ROW_BLOCK), F32),
                        pltpu.VMEM((HEAD_GROUP, 1, 2 * ROW_BLOCK), F32),
                        pltpu.VMEM((HEAD_GROUP, D_V_DIM + ONES_ROWS, 2 * ROW_BLOCK), F32)])
    return pl.pallas_call(
        functools.partial(_diff_prompt_kernel, lam_init=lam_init),
        grid_spec=grid_spec,
        out_shape=jax.ShapeDtypeStruct((r, D_V_WIDTH), F32),
        compiler_params=_params("arbitrary", "arbitrary", "arbitrary"),
        name="diff_prompt",
    )(slopes, qn, kb, vt, kb, vt, lam_pack, gain.reshape(1, -1))


def _diff_sample_kernel(pt_ref, *refs, lam_init):
    del pt_ref
    pp = PAGES_PER_STEP
    k_pages, v_pages = refs[0:pp], refs[pp:2 * pp]
    (q_ref, kn_ref, vn_ref, lam_ref, gain_ref, hd_in, o_ref,
     qb_ref, kp_ref, vp_ref, m_ref, acc_ref) = refs[2 * pp:]
    del hd_in
    g = pl.program_id(1)
    qrows = 2 * DEC_SEQ
    half = D_HEADS * DEC_SEQ
    slopes = [2.0 ** (-8.0 * (h + 1) / D_HEADS) * LOG2E for h in range(D_HEADS)]

    def step(h, s, v):
        m_old = m_ref[h]
        m_new = jnp.maximum(m_old, jnp.max(s, axis=1, keepdims=True))
        p = jnp.exp2(s - m_new).astype(BF16)
        v_ones = jnp.concatenate([v, jnp.ones_like(v)], axis=1)
        acc_ref[h] = jnp.exp2(m_old - m_new) * acc_ref[h] + _dot(p, v_ones)
        m_ref[h] = m_new

    @pl.when(g == 0)
    def _():
        lo = lax.broadcasted_iota(jnp.int32, (DEC_SEQ, LANES), 1) < D_QK_DIM
        for h in range(D_HEADS):
            x = q_ref[:, h * LANES:(h + 1) * LANES]
            qb_ref[h] = jnp.concatenate([jnp.where(lo, x, 0.0), jnp.where(lo, 0.0, x)], axis=0).astype(BF16)
        m_ref[...] = jnp.full_like(m_ref, NEG_INF)
        acc_ref[...] = jnp.zeros_like(acc_ref)

    k_pos = (lax.broadcasted_iota(jnp.int32, (1, pp * PAGE_SIZE), 1) + (g * (pp * PAGE_SIZE) - PAST_LEN)).astype(F32)
    for h in range(D_HEADS):
        rows_h = pl.ds(h, PAGE_SIZE, stride=D_HEADS)
        kh = jnp.concatenate([k_pages[p][rows_h, :] for p in range(pp)], axis=0).astype(BF16)
        vh = jnp.concatenate([v_pages[p][rows_h, :] for p in range(pp)], axis=0).astype(BF16)
        step(h, _dot_nt(qb_ref[h], kh) + slopes[h] * k_pos, vh)

    @pl.when(g == pl.num_programs(1) - 1)
    def _():
        for src, dst in ((kn_ref, kp_ref), (vn_ref, vp_ref)):
            dst[...] = jnp.zeros_like(dst)
            dst[0:half, :] = src[...].reshape(half, LANES)
        kn = kp_ref[...].astype(BF16)
        vn = vp_ref[...].astype(BF16)
        c = lax.broadcasted_iota(jnp.int32, (qrows, LANES), 1)
        r_tok = lax.broadcasted_iota(jnp.int32, (qrows, LANES), 0) % DEC_SEQ
        c_tok = c // D_HEADS
        lam = _lambda(lam_ref, lam_init)
        for h in range(D_HEADS):
            ok = (c % D_HEADS == h) & (c_tok <= r_tok)
            s = _dot_nt(qb_ref[h], kn) + slopes[h] * c_tok.astype(F32)
            step(h, jnp.where(ok, s, NEG_INF), vn)
            acc = acc_ref[h]
            o = acc[:, 0:LANES] * (1.0 / acc[:, LANES:2 * LANES])
            oh = o[0:DEC_SEQ, :] - lam * o[DEC_SEQ:qrows, :]
            cs = slice(h * LANES, (h + 1) * LANES)
            o_ref[:, cs] = _rms(oh, gain_ref[:, cs]) * (1.0 - lam_init)


def _diff_sample(layer, cache_k, cache_v, page_table, qn, kf, vf, lam_pack, gain, hd, lam_init):
    assert DEC_SEQ == SUBLANES and D_HEADS == SUBLANES and D_V_DIM == LANES
    assert D_HEADS * DEC_SEQ <= LANES and PAST_LEN % (PAGE_SIZE * PAGES_PER_STEP) == 0
    n_prompt = _rows()[0]
    blk0 = n_prompt // DEC_SEQ
    pp = PAGES_PER_STEP
    width = D_HEADS * LANES
    rows = 2 * D_HEADS * DEC_SEQ

    def as_rows(cache):
        return cache.reshape(cache.shape[0], cache.shape[1], PAGE_SIZE * D_HEADS, LANES)

    def page_spec(p):
        return pl.BlockSpec((None, None, PAGE_SIZE * D_HEADS, LANES),
                            lambda b, g, pt: (layer, pt[b, g * pp + p], 0, 0))

    row_spec = pl.BlockSpec((DEC_SEQ, width), lambda b, g, pt: (blk0 + b, 0))
    new_spec = pl.BlockSpec((DEC_SEQ, D_HEADS, LANES), lambda b, g, pt: (b, 0, 0))
    grid_spec = pltpu.PrefetchScalarGridSpec(
        num_scalar_prefetch=1,
        grid=(DEC_BATCH, PAST_LEN // (PAGE_SIZE * pp)),
        in_specs=[page_spec(p) for p in range(pp)] * 2
                 + [row_spec, new_spec, new_spec,
                    pl.BlockSpec((SUBLANES, LANES), lambda b, g, pt: (0, 0)),
                    pl.BlockSpec((1, width), lambda b, g, pt: (0, 0)),
                    pl.BlockSpec(memory_space=pl.ANY)],
        out_specs=row_spec,
        scratch_shapes=[pltpu.VMEM((D_HEADS, 2 * DEC_SEQ, LANES), BF16),
                        pltpu.VMEM((LANES, LANES), F32), pltpu.VMEM((LANES, LANES), F32),
                        pltpu.VMEM((D_HEADS, 2 * DEC_SEQ, 1), F32), pltpu.VMEM((D_HEADS, 2 * DEC_SEQ, 2 * LANES), F32)])
    n_in = 2 * pp + 6
    return pl.pallas_call(
        functools.partial(_diff_sample_kernel, lam_init=lam_init),
        grid_spec=grid_spec,
        out_shape=jax.ShapeDtypeStruct(hd.shape, F32),
        input_output_aliases={n_in: 0},
        compiler_params=_params("arbitrary", "arbitrary"),
        name="diff_sample",
    )(page_table, *([as_rows(cache_k)] * pp), *([as_rows(cache_v)] * pp), qn, kf, vf, lam_pack,
      gain.reshape(1, -1), hd)


def kernel(x_prompt, x_sample, cache_k, cache_v, state_C, state_n, state_m, page_table, meta_tokens, norm_mix, w_in, b_if, q_gain, k_gain, lambda_q1, lambda_k1, lambda_q2, lambda_k2, mlstm_norm, diff_norm, w_pm, w_pd, w_out, norm_ffn, w_gu, w_down):
    assert M_HEAD_DIM ** -0.5 == 2.0 ** round(math.log2(M_HEAD_DIM ** -0.5))
    n_prompt, n_sample, meta_row, r = _rows()
    x = jnp.concatenate([x_prompt.reshape(n_prompt, D_MODEL), x_sample.reshape(n_sample, D_MODEL),
                         meta_tokens.astype(F32), jnp.zeros((r - meta_row - N_META, D_MODEL), F32)], axis=0)
    slopes = 2.0 ** (-8.0 * jnp.arange(1, D_HEADS + 1, dtype=F32) / D_HEADS)
    w_in_t = jnp.swapaxes(w_in, 1, 2)
    g0 = 4 * M_WIDTH
    g1 = g0 + 2 * M_HEADS

    k_prompt = v_prompt = c_sample = n_sample_st = None
    meta_kv = []
    outs = [[] for _ in range(6)]
    for l in range(DEPTH):
        lam_init = 0.8 - 0.6 * math.exp(-0.3 * l)
        xn = _rmsnorm(x, norm_mix[l], "norm_mix")
        proj_m = _matmul_nt(xn, w_in_t, l, 0, g0, "proj_mlstm")
        graw = _matmul_nt(xn, w_in_t, l, g0, 2 * M_HEADS, "proj_gates")
        proj_d = _matmul_nt(xn, w_in_t, l, g1, w_in_t.shape[1] - g1, "proj_diff")

        hm, c_p, n_p, m_p = _mlstm_prompt(proj_m, proj_m, graw, graw.T, b_if[l], mlstm_norm[l])
        gs = graw[n_prompt:n_prompt + n_sample].reshape(DEC_BATCH, DEC_SEQ, 2 * M_HEADS)
        gs = jnp.pad(gs, ((0, 0), (0, SAMPLE_CHUNK - DEC_SEQ), (0, 0)))
        hm, c_sample, n_sample_st, m_s = _mlstm_sample(
            proj_m, proj_m, gs, gs.transpose(0, 2, 1), b_if[l], mlstm_norm[l], l, state_C.astype(F32),
            state_n.astype(F32), state_m[l].astype(F32), hm, c_sample, n_sample_st)

        qn, kb, vt, k_prompt, v_prompt, kf, vf = _qk_norm(proj_d, q_gain[l], k_gain[l], l, k_prompt, v_prompt)
        lam_pack = jnp.pad(jnp.stack([lambda_q1[l], lambda_k1[l], lambda_q2[l], lambda_k2[l]]).astype(F32),
                           ((0, SUBLANES - 4), (0, LANES - D_QK_DIM)))
        hd = _diff_prompt(qn, kb, vt, lam_pack, diff_norm[l], slopes, lam_init)
        hd = _diff_sample(l, cache_k, cache_v, page_table, qn, kf, vf, lam_pack, diff_norm[l], hd, lam_init)

        z = _merge(hm, hd, w_pm, w_pd, l, proj_d, 2 * D_QK_WIDTH + D_V_WIDTH)
        x = _matmul_res(z, w_out, l, x, "out_proj", MATMUL_ROWS, 1024)
        act = _swiglu(_rmsnorm(x, norm_ffn[l], "norm_ffn"), w_gu, l)
        x = _matmul_res(act, w_down, l, x, "ffn_down", MATMUL_ROWS // 2, 512)

        head_shape = (D_HEADS, D_V_DIM)
        meta_kv.append((kf[n_sample:n_sample + N_META], vf[n_sample:n_sample + N_META]))
        vals = (c_p, n_p, m_p[:, :, 0],
                kf[:n_sample].reshape(DEC_BATCH, DEC_SEQ, *head_shape),
                vf[:n_sample].reshape(DEC_BATCH, DEC_SEQ, *head_shape), m_s[:, :, 0])
        for o, v in zip(outs, vals):
            o.append(v)

    for l, (km, vm) in enumerate(meta_kv):
        k_prompt = k_prompt.at[l, :, :N_META].set(jnp.broadcast_to(km, (BATCH,) + km.shape))
        v_prompt = v_prompt.at[l, :, :N_META].set(jnp.broadcast_to(vm, (BATCH,) + vm.shape))
    y_prompt = x[:n_prompt].reshape(BATCH, SEQ, D_MODEL)
    y_sample = x[n_prompt:n_prompt + n_sample].reshape(DEC_BATCH, DEC_SEQ, D_MODEL)
    c_p, n_p, m_p, k_s, v_s, m_s = (jnp.stack(o) for o in outs)
    return (y_prompt, y_sample, k_prompt, v_prompt, c_p, n_p, m_p, k_s, v_s, c_sample, n_sample_st, m_s)
```

```python
import functools
import math

import jax
import jax.numpy as jnp
from jax import lax
from jax.experimental import pallas as pl
from jax.experimental.pallas import tpu as pltpu

D_MODEL = 2048
BATCH = 2
SEQ = 4096
DEPTH = 2
DEC_BATCH = 32
DEC_SEQ = 8
PAST_LEN = 8192
PAGE_SIZE = 128
N_META = 16
M_HEADS = 4
M_HEAD_DIM = 256
M_WIDTH = M_HEADS * M_HEAD_DIM
D_HEADS = 8
D_QK_DIM = 64
D_V_DIM = 2 * D_QK_DIM
D_QK_WIDTH = D_HEADS * 2 * D_QK_DIM
D_V_WIDTH = D_HEADS * D_V_DIM
EPS = 1e-6

LANES = 128
SUBLANES = 8
ROW_BLOCK = 256
MATMUL_ROWS = 1100
K_SCALE = M_HEAD_DIM ** -0.5
SAMPLE_CHUNK = 128
KV_UNROLL = (4, 2, 1)
HEAD_GROUP = 4
PAGES_PER_STEP = 16
VMEM_LIMIT = 56 * 1024 * 1024

F32 = jnp.float32
BF16 = jnp.bfloat16
NEG_INF = float("-inf")
LOG2E = math.log2(math.e)
Q_SCALE = D_QK_DIM ** -0.5 * LOG2E
ONES_ROWS = 16


def _ffn_dim():
    return ((8 * D_MODEL // 3 + 255) // 256) * 256


def _rows():
    n_prompt = BATCH * SEQ
    n_sample = DEC_BATCH * DEC_SEQ
    assert n_prompt % ROW_BLOCK == 0 and n_sample % ROW_BLOCK == 0 and N_META <= ROW_BLOCK
    return n_prompt, n_sample, n_prompt + n_sample, n_prompt + n_sample + ROW_BLOCK


def _row_tile(r):
    return 512 if r % 512 == 0 else ROW_BLOCK


def _tile(n, cap, mult=16):
    for t in range(min(cap, n) // mult * mult, 0, -mult):
        if n % t == 0:
            return t
    raise ValueError((n, cap, mult))


def _col_tile(n):
    for t in (1024, 512, 256, 128):
        if n % t == 0:
            return t
    raise ValueError(n)


def _params(*sem):
    return pltpu.CompilerParams(dimension_semantics=sem, vmem_limit_bytes=VMEM_LIMIT)


def _dot(a, b):
    return jnp.dot(a, b, preferred_element_type=F32)


def _dot_nt(a, b):
    return lax.dot_general(a, b, (((1,), (1,)), ((), ())), preferred_element_type=F32)


def _dot_tn(a, b):
    return lax.dot_general(a, b, (((0,), (0,)), ((), ())), preferred_element_type=F32)


def _sigmoid(x):
    return 1.0 / (1.0 + jnp.exp(-x))


def _rms(x, gain):
    ms = jnp.mean(x * x, axis=-1, keepdims=True)
    return x * lax.rsqrt(ms + EPS) * gain


def _rmsnorm_kernel(x_ref, g_ref, o_ref):
    o_ref[...] = _rms(x_ref[...], g_ref[...]).astype(o_ref.dtype)


def _rmsnorm(x, gain, name):
    r, d = x.shape
    tm = _tile(r, MATMUL_ROWS)
    return pl.pallas_call(
        _rmsnorm_kernel,
        grid=(r // tm,),
        in_specs=[pl.BlockSpec((tm, d), lambda i: (i, 0)), pl.BlockSpec((1, d), lambda i: (0, 0))],
        out_specs=pl.BlockSpec((tm, d), lambda i: (i, 0)),
        out_shape=jax.ShapeDtypeStruct((r, d), BF16),
        compiler_params=_params("parallel"),
        name=name,
    )(x, gain.reshape(1, d))


def _weight_spec(w, layer, tn, col0):
    if w.ndim == 3:
        return pl.BlockSpec((None, w.shape[1], tn), lambda j, i: (layer, 0, col0 + j))
    return pl.BlockSpec((w.shape[0], tn), lambda j, i: (0, col0 + j))


def _cast_weights(w_refs, wb_refs):
    @pl.when(pl.program_id(1) == 0)
    def _():
        for w_ref, wb_ref in zip(w_refs, wb_refs):
            wb_ref[...] = w_ref[...].astype(BF16)


def _matmul_kernel(a_ref, w_ref, o_ref, wb_ref):
    _cast_weights([w_ref], [wb_ref])
    o_ref[...] = _dot(a_ref[...], wb_ref[...]).astype(o_ref.dtype)


def _matmul(a, w, layer, n, out_dtype, name, tn=None):
    r, k = a.shape
    tm, tn = _tile(r, MATMUL_ROWS), tn or _col_tile(n)
    return pl.pallas_call(
        _matmul_kernel,
        grid=(n // tn, r // tm),
        in_specs=[pl.BlockSpec((tm, k), lambda j, i: (i, 0)), _weight_spec(w, layer, tn, 0)],
        out_specs=pl.BlockSpec((tm, tn), lambda j, i: (i, j)),
        out_shape=jax.ShapeDtypeStruct((r, n), out_dtype),
        scratch_shapes=[pltpu.VMEM((k, tn), BF16)],
        compiler_params=_params("parallel", "arbitrary"),
        name=name,
    )(a, w)


def _matmul_nt_kernel(a_ref, w_ref, o_ref, wb_ref):
    _cast_weights([w_ref.at[0]], [wb_ref])
    o_ref[...] = _dot_nt(a_ref[...], wb_ref[...]).astype(o_ref.dtype)


def _matmul_nt(a, w_t, layer, row0, n, name, out_dtype=F32):
    r, k = a.shape
    tm, tn = _tile(r, MATMUL_ROWS), _tile(n, 1024, SUBLANES)
    assert row0 % SUBLANES == 0
    return pl.pallas_call(
        _matmul_nt_kernel,
        grid=(n // tn, r // tm),
        in_specs=[pl.BlockSpec((tm, k), lambda j, i: (i, 0)),
                  pl.BlockSpec((pl.Element(1), pl.Element(tn), pl.Element(k)),
                               lambda j, i: (layer, pl.multiple_of(row0 + j * tn, SUBLANES), 0))],
        out_specs=pl.BlockSpec((tm, tn), lambda j, i: (i, j)),
        out_shape=jax.ShapeDtypeStruct((r, n), out_dtype),
        scratch_shapes=[pltpu.VMEM((tn, k), BF16)],
        compiler_params=_params("parallel", "arbitrary"),
        name=name,
    )(a, w_t)


def _swiglu_kernel(a_ref, wg_ref, wu_ref, o_ref, wgb_ref, wub_ref):
    _cast_weights([wg_ref, wu_ref], [wgb_ref, wub_ref])
    a = a_ref[...]
    g = _dot(a, wgb_ref[...])
    u = _dot(a, wub_ref[...])
    o_ref[...] = (g * _sigmoid(g) * u).astype(o_ref.dtype)


def _swiglu(a, w_gu, layer):
    r, k = a.shape
    f = w_gu.shape[2] // 2
    tm, tn = _tile(r, MATMUL_ROWS), min(512, _col_tile(f))
    nj = f // tn
    return pl.pallas_call(
        _swiglu_kernel,
        grid=(nj, r // tm),
        in_specs=[pl.BlockSpec((tm, k), lambda j, i: (i, 0)),
                  _weight_spec(w_gu, layer, tn, 0), _weight_spec(w_gu, layer, tn, nj)],
        out_specs=pl.BlockSpec((tm, tn), lambda j, i: (i, j)),
        out_shape=jax.ShapeDtypeStruct((r, f), BF16),
        scratch_shapes=[pltpu.VMEM((k, tn), BF16)] * 2,
        compiler_params=_params("parallel", "arbitrary"),
        name="ffn_up",
    )(a, w_gu, w_gu)


def _matmul_res_kernel(a_ref, w_ref, r_ref, o_ref, wb_ref):
    _cast_weights([w_ref], [wb_ref])
    o_ref[...] = r_ref[...] + _dot(a_ref[...], wb_ref[...])


def _matmul_res(a, w, layer, res, name, rows, tn_cap):
    r, k = a.shape
    n = w.shape[2]
    tm, tn = _tile(r, rows), min(tn_cap, _col_tile(n))
    return pl.pallas_call(
        _matmul_res_kernel,
        grid=(n // tn, r // tm),
        in_specs=[pl.BlockSpec((tm, k), lambda j, i: (i, 0)), _weight_spec(w, layer, tn, 0),
                  pl.BlockSpec((tm, tn), lambda j, i: (i, j))],
        out_specs=pl.BlockSpec((tm, tn), lambda j, i: (i, j)),
        out_shape=jax.ShapeDtypeStruct((r, n), F32),
        scratch_shapes=[pltpu.VMEM((k, tn), BF16)],
        compiler_params=_params("parallel", "arbitrary"),
        name=name,
    )(a, w, res)


def _merge_kernel(hm_ref, hd_ref, wpm_ref, wpd_ref, gm_ref, gd_ref, o_ref, wpmb_ref, wpdb_ref):
    _cast_weights([wpm_ref, wpd_ref], [wpmb_ref, wpdb_ref])
    zm = _dot(hm_ref[...].astype(BF16), wpmb_ref[...])
    zd = _dot(hd_ref[...].astype(BF16), wpdb_ref[...])
    gm = gm_ref[...].astype(F32)
    gd = gd_ref[...].astype(F32)
    o_ref[...] = (_sigmoid(gm) * zm + _sigmoid(gd) * zd).astype(o_ref.dtype)


def _merge(hm, hd, w_pm, w_pd, layer, gates, gate_col):
    r = hm.shape[0]
    n = w_pm.shape[2]
    tm = _tile(r, MATMUL_ROWS // 4)
    assert gate_col % LANES == 0

    def gate_spec(col0):
        return pl.BlockSpec((pl.Element(tm), pl.Element(n)), lambda j, i: (pl.multiple_of(i * tm, tm), col0))

    def resident(w):
        return pl.BlockSpec((None, w.shape[1], n), lambda j, i: (layer, 0, 0), pipeline_mode=pl.Buffered(1))

    return pl.pallas_call(
        _merge_kernel,
        grid=(1, r // tm),
        in_specs=[pl.BlockSpec((tm, M_WIDTH), lambda j, i: (i, 0)),
                  pl.BlockSpec((tm, D_V_WIDTH), lambda j, i: (i, 0)),
                  resident(w_pm), resident(w_pd), gate_spec(gate_col), gate_spec(gate_col + n)],
        out_specs=pl.BlockSpec((tm, n), lambda j, i: (i, 0)),
        out_shape=jax.ShapeDtypeStruct((r, n), BF16),
        scratch_shapes=[pltpu.VMEM((M_WIDTH, n), BF16), pltpu.VMEM((D_V_WIDTH, n), BF16)],
        compiler_params=_params("arbitrary", "arbitrary"),
        name="merge",
    )(hm, hd, w_pm, w_pd, gates, gates)


def _half_norm(x, gain):
    lo = lax.broadcasted_iota(jnp.int32, x.shape, 1) < D_QK_DIM
    x2 = x * x
    s_lo = jnp.sum(jnp.where(lo, x2, 0.0), axis=1, keepdims=True)
    s_hi = jnp.sum(jnp.where(lo, 0.0, x2), axis=1, keepdims=True)
    ms = jnp.where(lo, s_lo, s_hi) * (1.0 / D_QK_DIM)
    return x * lax.rsqrt(ms + EPS) * gain


def _qk_norm_kernel(q_ref, k_ref, v_ref, qg_ref, kg_ref, *refs, n_prompt_tiles):
    qo_ref, kb_ref, vt_ref, kp_ref, vp_ref, kt_ref, vt8_ref = refs[-7:]
    i = pl.program_id(0)
    kn = []
    for h in range(D_HEADS):
        sl = slice(h * LANES, (h + 1) * LANES)
        qo_ref[:, sl] = _half_norm(q_ref[:, sl], qg_ref[...]) * Q_SCALE
        kn.append(_half_norm(k_ref[:, sl], kg_ref[...]))
        kb_ref[:, sl] = kn[h].astype(BF16)
    for j in range(vt_ref.shape[0]):
        vt_ref[j] = v_ref[j * ROW_BLOCK:(j + 1) * ROW_BLOCK, :].T.astype(BF16)

    def scatter(k_dst, v_dst):
        tm = k_ref.shape[0]
        for h in range(D_HEADS):
            rows_h = pl.ds(h, tm, stride=D_HEADS)
            k_dst[rows_h, :] = kn[h]
            v_dst[rows_h, :] = v_ref[:, h * LANES:(h + 1) * LANES]

    @pl.when(i < n_prompt_tiles)
    def _():
        scatter(kp_ref.at[0, 0], vp_ref.at[0, 0])

    @pl.when(i == n_prompt_tiles)
    def _():
        scatter(kt_ref, vt8_ref)


def _qk_norm(qkv, q_gain, k_gain, layer, k_out, v_out):
    r = qkv.shape[0]
    w = D_QK_WIDTH
    n_prompt = _rows()[0]
    tm = r - n_prompt
    assert n_prompt % tm == 0 and SEQ % tm == 0 and tm % ROW_BLOCK == 0
    n_prompt_tiles, tiles_per_seq = n_prompt // tm, SEQ // tm
    spec = lambda c: pl.BlockSpec((tm, w), lambda i: (i, c))
    gspec = pl.BlockSpec((1, LANES), lambda i: (0, 0))

    def prompt_map(i):
        t = jnp.minimum(i, n_prompt_tiles - 1)
        row = pl.multiple_of((N_META + (t % tiles_per_seq) * tm) * D_HEADS, SUBLANES)
        return (layer, t // tiles_per_seq, row, 0)

    pspec = pl.BlockSpec(tuple(pl.Element(s) for s in (1, 1, tm * D_HEADS, LANES)), prompt_map)
    tspec = pl.BlockSpec((tm * D_HEADS, LANES), lambda i: (0, 0))
    out5 = jax.ShapeDtypeStruct((DEPTH, BATCH, (N_META + SEQ) * D_HEADS, LANES), F32)
    tail = jax.ShapeDtypeStruct((tm * D_HEADS, LANES), F32)
    carried = [] if k_out is None else [k_out, v_out]
    n_in = 5 + len(carried)
    return pl.pallas_call(
        functools.partial(_qk_norm_kernel, n_prompt_tiles=n_prompt_tiles),
        grid=(r // tm,),
        in_specs=[spec(0), spec(1), spec(2), gspec, gspec] + [pl.BlockSpec(memory_space=pl.ANY)] * len(carried),
        out_specs=[spec(0), spec(0), pl.BlockSpec((tm // ROW_BLOCK, D_V_WIDTH, ROW_BLOCK), lambda i: (i, 0, 0)),
                   pspec, pspec, tspec, tspec],
        out_shape=[jax.ShapeDtypeStruct((r, w), F32), jax.ShapeDtypeStruct((r, w), BF16),
                   jax.ShapeDtypeStruct((r // ROW_BLOCK, D_V_WIDTH, ROW_BLOCK), BF16), out5, out5, tail, tail],
        input_output_aliases={} if k_out is None else {n_in - 2: 3, n_in - 1: 4},
        compiler_params=_params("arbitrary"),
        name="qk_norm",
    )(qkv, qkv, qkv, jnp.tile(q_gain, 2).reshape(1, LANES), jnp.tile(k_gain, 2).reshape(1, LANES), *carried)


def _split3(x):
    hi = x.astype(BF16)
    r1 = x - hi.astype(F32)
    mid = r1.astype(BF16)
    lo = (r1 - mid.astype(F32)).astype(BF16)
    return hi, mid, lo


def _log_sigmoid(x):
    return jnp.minimum(x, 0.0) - jnp.log1p(jnp.exp(-jnp.abs(x)))


def _mlstm_gates(graw_col, graw_row, bias_col, bias_row, n_valid):
    l = graw_col.shape[0]
    g_col = graw_col + bias_col
    g_row = graw_row + bias_row
    is_f_col = lax.broadcasted_iota(jnp.int32, g_col.shape, 1) >= M_HEADS
    is_f_row = lax.broadcasted_iota(jnp.int32, g_row.shape, 0) >= M_HEADS
    ok_col = lax.broadcasted_iota(jnp.int32, g_col.shape, 0) < n_valid
    ok_row = lax.broadcasted_iota(jnp.int32, g_row.shape, 1) < n_valid
    lf_col = jnp.where(is_f_col & ok_col, _log_sigmoid(g_col), 0.0)
    lf_row = jnp.where(is_f_row & ok_row, _log_sigmoid(g_row), 0.0)
    ig_col = jnp.where(ok_col, g_col, NEG_INF)
    ig_row = jnp.where(ok_row, g_row, NEG_INF)
    rr = lax.broadcasted_iota(jnp.int32, (l, l), 0)
    cc = lax.broadcasted_iota(jnp.int32, (l, l), 1)
    tri_l = (cc <= rr).astype(BF16)
    tri_u = (rr <= cc).astype(BF16)
    b_col = sum(_dot(tri_l, p) for p in _split3(lf_col))
    b_row = sum(_dot(p, tri_u) for p in _split3(lf_row))
    return ig_col, b_col, ig_row, b_row, cc <= rr


def _mlstm_chunk(q, k, v, ig_col, b_col, ig_row, b_row, causal, c_st, n_st, m_st):
    l = q.shape[0]
    d = b_col - b_row + ig_row
    d = jnp.where(causal, d, NEG_INF)
    inter = b_col + m_st
    m_t = jnp.maximum(inter, jnp.max(d, axis=1, keepdims=True))
    w = jnp.exp(d - m_t)
    s = _dot_nt(q, k) * w
    a = jnp.exp(inter - m_t)
    num = _dot(s.astype(BF16), v) + a * _dot_nt(q, c_st.astype(BF16))
    den = jnp.sum(s, axis=1, keepdims=True) + a * jnp.sum(q.astype(F32) * n_st, axis=1, keepdims=True)
    h = num * (1.0 / jnp.maximum(jnp.abs(den), jnp.exp(-m_t)))

    b_last = b_col[l - 1:l, :]
    g_col = b_last - b_col + ig_col
    g_row = b_last - b_row + ig_row
    m_new = jnp.maximum(b_last + m_st, jnp.max(g_row, axis=1, keepdims=True))
    ws_col = jnp.exp(g_col - m_new)
    ws_row = jnp.exp(g_row - m_new)
    decay = jnp.exp(b_last + m_st - m_new)
    wv = (v.astype(F32) * ws_col).astype(BF16)
    c_new = decay * c_st + _dot_tn(wv, k)
    n_new = decay * n_st + _dot(ws_row.astype(BF16), k)
    return h, c_new, n_new, m_new


def _mlstm_heads(q_of, k_of, v_of, om_of, gates, gain_ref, c_ref, n_ref, m_ref, h_out):
    ig_col, b_col, ig_row, b_row, causal = gates
    for h in range(M_HEADS):
        f = M_HEADS + h
        hs = slice(h * M_HEAD_DIM, (h + 1) * M_HEAD_DIM)
        out, c_new, n_new, m_new = _mlstm_chunk(
            q_of(hs), k_of(hs), v_of(hs),
            ig_col[:, h:h + 1], b_col[:, f:f + 1], ig_row[h:h + 1, :], b_row[f:f + 1, :], causal,
            c_ref[0, h], n_ref[0, h:h + 1, :], m_ref[0, h:h + 1, 0:1])
        c_ref[0, h] = c_new
        n_ref[0, h:h + 1, :] = n_new
        m_ref[0, h:h + 1, :] = jnp.broadcast_to(m_new, (1, LANES))
        h_out(hs, _rms(out, gain_ref[:, hs]) * _sigmoid(om_of(hs)))


def _mlstm_prompt_kernel(q_ref, k_ref, v_ref, om_ref, gc_ref, gr_ref, bc_ref, br_ref, gain_ref,
                         h_ref, c_ref, n_ref, m_ref):
    c = pl.program_id(1)

    @pl.when(c == 0)
    def _():
        c_ref[...] = jnp.zeros_like(c_ref)
        n_ref[...] = jnp.zeros_like(n_ref)
        m_ref[...] = jnp.zeros_like(m_ref)

    n_valid = jnp.where(c == 0, N_META, ROW_BLOCK)
    gates = _mlstm_gates(gc_ref[...], gr_ref[...], bc_ref[...], br_ref[...], n_valid)

    def h_out(hs, val):
        h_ref[:, hs] = val

    _mlstm_heads(lambda hs: q_ref[:, hs].astype(BF16), lambda hs: (k_ref[:, hs] * K_SCALE).astype(BF16),
                 lambda hs: v_ref[:, hs].astype(BF16), lambda hs: om_ref[:, hs],
                 gates, gain_ref, c_ref, n_ref, m_ref, h_out)


def _mlstm_prompt(qkv, gates, g_col, g_row, b_if, gain):
    r = qkv.shape[0]
    n_prompt, _, meta_row, _ = _rows()
    nc = SEQ // ROW_BLOCK
    meta_blk = meta_row // ROW_BLOCK

    def rb(b, c):
        return jnp.where(c == 0, meta_blk, b * nc + c - 1)

    def rb_out(b, c):
        return jnp.where((c == 0) & (b > 0), b * nc, rb(b, c))

    spec = lambda col: pl.BlockSpec((ROW_BLOCK, M_WIDTH), lambda b, c: (rb(b, c), col))
    full = lambda shape: pl.BlockSpec(shape, lambda b, c: (0,) * len(shape))
    return pl.pallas_call(
        _mlstm_prompt_kernel,
        grid=(BATCH, nc + 1),
        in_specs=[spec(0), spec(1), spec(2), spec(3),
                  pl.BlockSpec((ROW_BLOCK, 2 * M_HEADS), lambda b, c: (rb(b, c), 0)),
                  pl.BlockSpec((2 * M_HEADS, ROW_BLOCK), lambda b, c: (0, rb(b, c))),
                  full((1, 2 * M_HEADS)), full((2 * M_HEADS, 1)), full((1, M_WIDTH))],
        out_specs=[pl.BlockSpec((ROW_BLOCK, M_WIDTH), lambda b, c: (rb_out(b, c), 0)),
                   pl.BlockSpec((1, M_HEADS, M_HEAD_DIM, M_HEAD_DIM), lambda b, c: (b, 0, 0, 0)),
                   pl.BlockSpec((1, M_HEADS, M_HEAD_DIM), lambda b, c: (b, 0, 0)),
                   pl.BlockSpec((1, M_HEADS, LANES), lambda b, c: (b, 0, 0))],
        out_shape=[jax.ShapeDtypeStruct((r, M_WIDTH), F32),
                   jax.ShapeDtypeStruct((BATCH, M_HEADS, M_HEAD_DIM, M_HEAD_DIM), F32),
                   jax.ShapeDtypeStruct((BATCH, M_HEADS, M_HEAD_DIM), F32),
                   jax.ShapeDtypeStruct((BATCH, M_HEADS, LANES), F32)],
        compiler_params=_params("arbitrary", "arbitrary"),
        name="mlstm_prompt",
    )(qkv, qkv, qkv, gates, g_col, g_row, b_if.reshape(1, -1), b_if.reshape(-1, 1), gain.reshape(1, -1))


def _mlstm_sample_kernel(q_ref, k_ref, v_ref, om_ref, gc_ref, gr_ref, bc_ref, br_ref, gain_ref,
                         c_in, n_in, m_in, *rest):
    h_ref, c_ref, n_ref, m_ref, qp, kp, vp = rest[-7:]
    c_ref[...] = c_in[...]
    n_ref[...] = n_in[...]
    m_ref[...] = m_in[...]
    for src, dst in ((q_ref, qp), (k_ref, kp), (v_ref, vp)):
        dst[...] = jnp.zeros_like(dst)
        dst[0:DEC_SEQ, :] = src[...]
    gates = _mlstm_gates(gc_ref[0], gr_ref[0], bc_ref[...], br_ref[...], DEC_SEQ)
    om = om_ref[...]

    def h_out(hs, val):
        h_ref[:, hs] = val[0:DEC_SEQ, :]

    def om_of(hs):
        return jnp.concatenate([om[:, hs], jnp.zeros((SAMPLE_CHUNK - DEC_SEQ, M_HEAD_DIM), F32)], axis=0)

    _mlstm_heads(lambda hs: qp[:, hs].astype(BF16), lambda hs: (kp[:, hs] * K_SCALE).astype(BF16),
                 lambda hs: vp[:, hs].astype(BF16), om_of,
                 gates, gain_ref, c_ref, n_ref, m_ref, h_out)


def _mlstm_sample(qkv, gates, g_col, g_row, b_if, gain, layer, state_c, state_n, st_m, hm, c_out, n_out):
    assert DEC_SEQ == SUBLANES
    n_prompt = _rows()[0]
    blk0 = n_prompt // DEC_SEQ
    spec = lambda col: pl.BlockSpec((DEC_SEQ, M_WIDTH), lambda b: (blk0 + b, col))
    full = lambda shape: pl.BlockSpec(shape, lambda b: (0,) * len(shape))
    st_spec = lambda shape: pl.BlockSpec((1,) + shape, lambda b: (b,) + (0,) * len(shape))
    layer_spec = lambda shape: pl.BlockSpec((None, 1) + shape, lambda b: (layer, b) + (0,) * len(shape))
    st_specs = [layer_spec((M_HEADS, M_HEAD_DIM, M_HEAD_DIM)), layer_spec((M_HEADS, M_HEAD_DIM)),
                st_spec((M_HEADS, LANES))]
    carried = [] if c_out is None else [c_out, n_out]
    aliases = {12: 0}
    if carried:
        aliases.update({13: 1, 14: 2})
    return pl.pallas_call(
        _mlstm_sample_kernel,
        grid=(DEC_BATCH,),
        in_specs=[spec(0), spec(1), spec(2), spec(3),
                  st_spec((SAMPLE_CHUNK, 2 * M_HEADS)), st_spec((2 * M_HEADS, SAMPLE_CHUNK)),
                  full((1, 2 * M_HEADS)), full((2 * M_HEADS, 1)), full((1, M_WIDTH))] + st_specs
                 + [pl.BlockSpec(memory_space=pl.ANY)] * (1 + len(carried)),
        out_specs=[spec(0)] + st_specs,
        out_shape=[jax.ShapeDtypeStruct(hm.shape, F32),
                   jax.ShapeDtypeStruct(state_c.shape, F32), jax.ShapeDtypeStruct(state_n.shape, F32),
                   jax.ShapeDtypeStruct((DEC_BATCH, M_HEADS, LANES), F32)],
        scratch_shapes=[pltpu.VMEM((SAMPLE_CHUNK, M_WIDTH), F32)] * 3,
        input_output_aliases=aliases,
        compiler_params=_params("arbitrary"),
        name="mlstm_sample",
    )(qkv, qkv, qkv, gates, g_col, g_row, b_if.reshape(1, -1), b_if.reshape(-1, 1), gain.reshape(1, -1),
      state_c, state_n, jnp.broadcast_to(st_m[:, :, None], (DEC_BATCH, M_HEADS, LANES)), hm, *carried)


def _lambda(lam_ref, lam_init):
    p = lam_ref[...]
    s1 = jnp.sum(p[0:1, :] * p[1:2, :], axis=1, keepdims=True)
    s2 = jnp.sum(p[2:3, :] * p[3:4, :], axis=1, keepdims=True)
    return jnp.exp(s1) - jnp.exp(s2) + lam_init


def _diff_prompt_kernel(slope_ref, q_ref, k_ref, vt_ref, km_ref, vtm_ref, lam_ref, gain_ref, o_ref,
                        base_ref, m_ref, acc_ref, *, lam_init):
    b = pl.program_id(0)
    hg = pl.program_id(1)
    is_meta = pl.program_id(2) == 0
    t = pl.program_id(2) - 1
    tq = ROW_BLOCK
    heads = range(HEAD_GROUP)
    cols = [slice(i * LANES, (i + 1) * LANES) for i in heads]
    slopes = [slope_ref[hg * HEAD_GROUP + i] * LOG2E for i in heads]

    @pl.when(jnp.logical_not(is_meta & (b > 0)))
    def _():
        lo = lax.broadcasted_iota(jnp.int32, (tq, LANES), 1) < D_QK_DIM
        q2 = []
        for i in heads:
            q = q_ref[:, cols[i]].astype(BF16)
            zero = jnp.zeros_like(q)
            q2.append(jnp.concatenate([jnp.where(lo, q, zero), jnp.where(lo, zero, q)], axis=0))
            base_ref[i] = slopes[i] * lax.broadcasted_iota(jnp.int32, (tq, 2 * tq), 0).astype(F32)
        q_pos0 = jnp.where(is_meta, 0, N_META + t * tq)
        ones = jnp.ones((ONES_ROWS, tq), BF16)
        m_ref[...] = jnp.full_like(m_ref, NEG_INF)
        acc_ref[...] = jnp.zeros_like(acc_ref)

        def visible(n_keys, k_off):
            key = lax.broadcasted_iota(jnp.int32, (n_keys, 2 * tq), 0) + k_off
            qry = lax.broadcasted_iota(jnp.int32, (n_keys, 2 * tq), 1)
            return key <= jnp.where(qry >= tq, qry - tq, qry)

        def step(i, parts):
            cs = [slopes[i] * k_off.astype(F32) for _, _, k_off in parts]
            m_old = m_ref[i]
            m_new = m_old
            for (s, _, _), c in zip(parts, cs):
                m_new = jnp.maximum(m_new, jnp.max(s, axis=0, keepdims=True) + c)
            acc = jnp.exp2(m_old - m_new) * acc_ref[i]
            for (s, vt, _), c in zip(parts, cs):
                p = jnp.exp2(s - (m_new - c)).astype(BF16)
                acc = acc + _dot(jnp.concatenate([vt, ones[:, 0:vt.shape[1]]], axis=0), p)
            acc_ref[i] = acc
            m_ref[i] = m_new

        def scores(kb, i):
            ks = pl.ds(pl.multiple_of(kb * tq, tq), tq)
            return _dot_nt(k_ref[ks, cols[i]], q2[i])

        def full_blocks(kbs):
            for i in heads:
                step(i, [(scores(kb, i) + base_ref[i], vt_ref[kb, cols[i], :], N_META + kb * tq - q_pos0)
                         for kb in kbs])

        n_full = jnp.where(is_meta, 0, t)
        done = 0
        for width in KV_UNROLL:
            n_iter = (n_full - done) // width

            def body(j, carry, width=width, done=done):
                full_blocks([done + j * width + u for u in range(width)])
                return carry

            lax.fori_loop(0, n_iter, body, 0)
            done = done + n_iter * width

        td = jnp.maximum(t, 0)
        hide = jnp.where(is_meta, 2 * tq, 0)
        for i in heads:
            s_meta = _dot_nt(km_ref[:, cols[i]], q2[i]) + base_ref[i, 0:N_META, :]
            s_diag = scores(td, i) + base_ref[i]
            step(i, [(jnp.where(visible(N_META, -q_pos0), s_meta, NEG_INF), vtm_ref[0, cols[i], 0:N_META], -q_pos0),
                     (jnp.where(visible(tq, hide), s_diag, NEG_INF), vt_ref[td, cols[i], :], jnp.int32(0))])

        lam = _lambda(lam_ref, lam_init)
        for i in heads:
            acc = acc_ref[i]
            o_t = acc[0:D_V_DIM, :] * (1.0 / acc[D_V_DIM:D_V_DIM + 1, :])
            o = (o_t[:, 0:tq] - lam * o_t[:, tq:2 * tq]).T
            o_ref[:, cols[i]] = _rms(o, gain_ref[:, cols[i]]) * (1.0 - lam_init)


def _diff_prompt(qn, kb, vt, lam_pack, gain, slopes, lam_init):
    r = qn.shape[0]
    meta_row = _rows()[2]
    nq = SEQ // ROW_BLOCK
    meta_blk = meta_row // ROW_BLOCK
    gw = HEAD_GROUP * LANES
    assert meta_row % N_META == 0 and D_V_DIM == LANES and D_HEADS % HEAD_GROUP == 0

    def qmap(b, h, qi, *_):
        return (jnp.where(qi == 0, jnp.where(b == 0, meta_blk, b * nq), b * nq + qi - 1), h)

    grid_spec = pltpu.PrefetchScalarGridSpec(
        num_scalar_prefetch=1,
        grid=(BATCH, D_HEADS // HEAD_GROUP, nq + 1),
        in_specs=[pl.BlockSpec((ROW_BLOCK, gw), qmap),
                  pl.BlockSpec((SEQ, gw), lambda b, h, qi, *_: (b, h)),
                  pl.BlockSpec((nq, gw, ROW_BLOCK), lambda b, h, qi, *_: (b, h, 0)),
                  pl.BlockSpec((N_META, gw), lambda b, h, qi, *_: (meta_row // N_META, h)),
                  pl.BlockSpec((1, gw, ROW_BLOCK), lambda b, h, qi, *_: (meta_blk, h, 0)),
                  pl.BlockSpec((SUBLANES, LANES), lambda b, h, qi, *_: (0, 0)),
                  pl.BlockSpec((1, gw), lambda b, h, qi, *_: (0, h))],
        out_specs=pl.BlockSpec((ROW_BLOCK, gw), qmap),
        scratch_shapes=[pltpu.VMEM((HEAD_GROUP, ROW_BLOCK, 2 * ROW_BLOCK), F32),
                        pltpu.VMEM((HEAD_GROUP, 1, 2 * ROW_BLOCK), F32),
                        pltpu.VMEM((HEAD_GROUP, D_V_DIM + ONES_ROWS, 2 * ROW_BLOCK), F32)])
    return pl.pallas_call(
        functools.partial(_diff_prompt_kernel, lam_init=lam_init),
        grid_spec=grid_spec,
        out_shape=jax.ShapeDtypeStruct((r, D_V_WIDTH), F32),
        compiler_params=_params("arbitrary", "arbitrary", "arbitrary"),
        name="diff_prompt",
    )(slopes, qn, kb, vt, kb, vt, lam_pack, gain.reshape(1, -1))


def _diff_sample_kernel(pt_ref, *refs, lam_init):
    del pt_ref
    pp = PAGES_PER_STEP
    k_pages, v_pages = refs[0:pp], refs[pp:2 * pp]
    (q_ref, kn_ref, vn_ref, lam_ref, gain_ref, hd_in, o_ref,
     qb_ref, kp_ref, vp_ref, m_ref, acc_ref) = refs[2 * pp:]
    del hd_in
    g = pl.program_id(1)
    qrows = 2 * DEC_SEQ
    half = D_HEADS * DEC_SEQ
    slopes = [2.0 ** (-8.0 * (h + 1) / D_HEADS) * LOG2E for h in range(D_HEADS)]

    def step(h, s, v):
        m_old = m_ref[h]
        m_new = jnp.maximum(m_old, jnp.max(s, axis=1, keepdims=True))
        p = jnp.exp2(s - m_new).astype(BF16)
        v_ones = jnp.concatenate([v, jnp.ones_like(v)], axis=1)
        acc_ref[h] = jnp.exp2(m_old - m_new) * acc_ref[h] + _dot(p, v_ones)
        m_ref[h] = m_new

    @pl.when(g == 0)
    def _():
        lo = lax.broadcasted_iota(jnp.int32, (DEC_SEQ, LANES), 1) < D_QK_DIM
        for h in range(D_HEADS):
            x = q_ref[:, h * LANES:(h + 1) * LANES]
            qb_ref[h] = jnp.concatenate([jnp.where(lo, x, 0.0), jnp.where(lo, 0.0, x)], axis=0).astype(BF16)
        m_ref[...] = jnp.full_like(m_ref, NEG_INF)
        acc_ref[...] = jnp.zeros_like(acc_ref)

    k_pos = (lax.broadcasted_iota(jnp.int32, (1, pp * PAGE_SIZE), 1) + (g * (pp * PAGE_SIZE) - PAST_LEN)).astype(F32)
    for h in range(D_HEADS):
        rows_h = pl.ds(h, PAGE_SIZE, stride=D_HEADS)
        kh = jnp.concatenate([k_pages[p][rows_h, :] for p in range(pp)], axis=0).astype(BF16)
        vh = jnp.concatenate([v_pages[p][rows_h, :] for p in range(pp)], axis=0).astype(BF16)
        step(h, _dot_nt(qb_ref[h], kh) + slopes[h] * k_pos, vh)

    @pl.when(g == pl.num_programs(1) - 1)
    def _():
        for src, dst in ((kn_ref, kp_ref), (vn_ref, vp_ref)):
            dst[...] = jnp.zeros_like(dst)
            dst[0:half, :] = src[...]
        kn = kp_ref[...].astype(BF16)
        vn = vp_ref[...].astype(BF16)
        c = lax.broadcasted_iota(jnp.int32, (qrows, LANES), 1)
        r_tok = lax.broadcasted_iota(jnp.int32, (qrows, LANES), 0) % DEC_SEQ
        c_tok = c // D_HEADS
        lam = _lambda(lam_ref, lam_init)
        for h in range(D_HEADS):
            ok = (c % D_HEADS == h) & (c_tok <= r_tok)
            s = _dot_nt(qb_ref[h], kn) + slopes[h] * c_tok.astype(F32)
            step(h, jnp.where(ok, s, NEG_INF), vn)
            acc = acc_ref[h]
            o = acc[:, 0:LANES] * (1.0 / acc[:, LANES:2 * LANES])
            oh = o[0:DEC_SEQ, :] - lam * o[DEC_SEQ:qrows, :]
            cs = slice(h * LANES, (h + 1) * LANES)
            o_ref[:, cs] = _rms(oh, gain_ref[:, cs]) * (1.0 - lam_init)


def _diff_sample(layer, cache_k, cache_v, page_table, qn, kf, vf, lam_pack, gain, hd, lam_init):
    assert DEC_SEQ == SUBLANES and D_HEADS == SUBLANES and D_V_DIM == LANES
    assert D_HEADS * DEC_SEQ <= LANES and PAST_LEN % (PAGE_SIZE * PAGES_PER_STEP) == 0
    n_prompt = _rows()[0]
    blk0 = n_prompt // DEC_SEQ
    pp = PAGES_PER_STEP
    width = D_HEADS * LANES
    rows = 2 * D_HEADS * DEC_SEQ

    def as_rows(cache):
        return cache.reshape(cache.shape[0], cache.shape[1], PAGE_SIZE * D_HEADS, LANES)

    def page_spec(p):
        return pl.BlockSpec((None, None, PAGE_SIZE * D_HEADS, LANES),
                            lambda b, g, pt: (layer, pt[b, g * pp + p], 0, 0))

    row_spec = pl.BlockSpec((DEC_SEQ, width), lambda b, g, pt: (blk0 + b, 0))
    new_spec = pl.BlockSpec((DEC_SEQ * D_HEADS, LANES), lambda b, g, pt: (b, 0))
    grid_spec = pltpu.PrefetchScalarGridSpec(
        num_scalar_prefetch=1,
        grid=(DEC_BATCH, PAST_LEN // (PAGE_SIZE * pp)),
        in_specs=[page_spec(p) for p in range(pp)] * 2
                 + [row_spec, new_spec, new_spec,
                    pl.BlockSpec((SUBLANES, LANES), lambda b, g, pt: (0, 0)),
                    pl.BlockSpec((1, width), lambda b, g, pt: (0, 0)),
                    pl.BlockSpec(memory_space=pl.ANY)],
        out_specs=row_spec,
        scratch_shapes=[pltpu.VMEM((D_HEADS, 2 * DEC_SEQ, LANES), BF16),
                        pltpu.VMEM((LANES, LANES), F32), pltpu.VMEM((LANES, LANES), F32),
                        pltpu.VMEM((D_HEADS, 2 * DEC_SEQ, 1), F32), pltpu.VMEM((D_HEADS, 2 * DEC_SEQ, 2 * LANES), F32)])
    n_in = 2 * pp + 6
    return pl.pallas_call(
        functools.partial(_diff_sample_kernel, lam_init=lam_init),
        grid_spec=grid_spec,
        out_shape=jax.ShapeDtypeStruct(hd.shape, F32),
        input_output_aliases={n_in: 0},
        compiler_params=_params("arbitrary", "arbitrary"),
        name="diff_sample",
    )(page_table, *([as_rows(cache_k)] * pp), *([as_rows(cache_v)] * pp), qn, kf, vf, lam_pack,
      gain.reshape(1, -1), hd)


def kernel(x_prompt, x_sample, cache_k, cache_v, state_C, state_n, state_m, page_table, meta_tokens, norm_mix, w_in, b_if, q_gain, k_gain, lambda_q1, lambda_k1, lambda_q2, lambda_k2, mlstm_norm, diff_norm, w_pm, w_pd, w_out, norm_ffn, w_gu, w_down):
    assert M_HEAD_DIM ** -0.5 == 2.0 ** round(math.log2(M_HEAD_DIM ** -0.5))
    n_prompt, n_sample, meta_row, r = _rows()
    x = jnp.concatenate([x_prompt.reshape(n_prompt, D_MODEL), x_sample.reshape(n_sample, D_MODEL),
                         meta_tokens.astype(F32), jnp.zeros((r - meta_row - N_META, D_MODEL), F32)], axis=0)
    slopes = 2.0 ** (-8.0 * jnp.arange(1, D_HEADS + 1, dtype=F32) / D_HEADS)
    w_in_t = jnp.swapaxes(w_in, 1, 2)
    g0 = 4 * M_WIDTH
    g1 = g0 + 2 * M_HEADS

    k_prompt = v_prompt = c_sample = n_sample_st = None
    meta_kv = []
    outs = [[] for _ in range(6)]
    for l in range(DEPTH):
        lam_init = 0.8 - 0.6 * math.exp(-0.3 * l)
        xn = _rmsnorm(x, norm_mix[l], "norm_mix")
        proj_m = _matmul_nt(xn, w_in_t, l, 0, g0, "proj_mlstm")
        graw = _matmul_nt(xn, w_in_t, l, g0, 2 * M_HEADS, "proj_gates")
        g2 = g1 + 2 * D_QK_WIDTH + D_V_WIDTH
        proj_d = _matmul_nt(xn, w_in_t, l, g1, g2 - g1, "proj_diff")
        proj_g = _matmul_nt(xn, w_in_t, l, g2, w_in_t.shape[1] - g2, "proj_merge_gates", BF16)

        hm, c_p, n_p, m_p = _mlstm_prompt(proj_m, proj_m, graw, graw.T, b_if[l], mlstm_norm[l])
        gs = graw[n_prompt:n_prompt + n_sample].reshape(DEC_BATCH, DEC_SEQ, 2 * M_HEADS)
        gs = jnp.pad(gs, ((0, 0), (0, SAMPLE_CHUNK - DEC_SEQ), (0, 0)))
        hm, c_sample, n_sample_st, m_s = _mlstm_sample(
            proj_m, proj_m, gs, gs.transpose(0, 2, 1), b_if[l], mlstm_norm[l], l, state_C.astype(F32),
            state_n.astype(F32), state_m[l].astype(F32), hm, c_sample, n_sample_st)

        qn, kb, vt, k_prompt, v_prompt, kf, vf = _qk_norm(proj_d, q_gain[l], k_gain[l], l, k_prompt, v_prompt)
        lam_pack = jnp.pad(jnp.stack([lambda_q1[l], lambda_k1[l], lambda_q2[l], lambda_k2[l]]).astype(F32),
                           ((0, SUBLANES - 4), (0, LANES - D_QK_DIM)))
        hd = _diff_prompt(qn, kb, vt, lam_pack, diff_norm[l], slopes, lam_init)
        hd = _diff_sample(l, cache_k, cache_v, page_table, qn, kf, vf, lam_pack, diff_norm[l], hd, lam_init)

        z = _merge(hm, hd, w_pm, w_pd, l, proj_g, 0)
        x = _matmul_res(z, w_out, l, x, "out_proj", MATMUL_ROWS, 1024)
        act = _swiglu(_rmsnorm(x, norm_ffn[l], "norm_ffn"), w_gu, l)
        x = _matmul_res(act, w_down, l, x, "ffn_down", MATMUL_ROWS // 2, 512)

        head_shape = (D_HEADS, D_V_DIM)
        s_rows, m_rows = n_sample * D_HEADS, N_META * D_HEADS
        meta_kv.append((kf[s_rows:s_rows + m_rows], vf[s_rows:s_rows + m_rows]))
        vals = (c_p, n_p, m_p[:, :, 0],
                kf[:s_rows].reshape(DEC_BATCH, DEC_SEQ, *head_shape),
                vf[:s_rows].reshape(DEC_BATCH, DEC_SEQ, *head_shape), m_s[:, :, 0])
        for o, v in zip(outs, vals):
            o.append(v)

    for l, (km, vm) in enumerate(meta_kv):
        k_prompt = k_prompt.at[l, :, :km.shape[0]].set(jnp.broadcast_to(km, (BATCH,) + km.shape))
        v_prompt = v_prompt.at[l, :, :vm.shape[0]].set(jnp.broadcast_to(vm, (BATCH,) + vm.shape))
    kv_shape = (DEPTH, BATCH, N_META + SEQ, D_HEADS, D_V_DIM)
    y_prompt = x[:n_prompt].reshape(BATCH, SEQ, D_MODEL)
    y_sample = x[n_prompt:n_prompt + n_sample].reshape(DEC_BATCH, DEC_SEQ, D_MODEL)
    c_p, n_p, m_p, k_s, v_s, m_s = (jnp.stack(o) for o in outs)
    return (y_prompt, y_sample, k_prompt.reshape(kv_shape), v_prompt.reshape(kv_shape), c_p, n_p, m_p, k_s, v_s,
            c_sample, n_sample_st, m_s)
```

```python
import functools
import math

import jax
import jax.numpy as jnp
from jax import lax
from jax.experimental import pallas as pl
from jax.experimental.pallas import tpu as pltpu

D_MODEL = 2048
BATCH = 2
SEQ = 4096
DEPTH = 2
DEC_BATCH = 32
DEC_SEQ = 8
PAST_LEN = 8192
PAGE_SIZE = 128
N_META = 16
M_HEADS = 4
M_HEAD_DIM = 256
M_WIDTH = M_HEADS * M_HEAD_DIM
D_HEADS = 8
D_QK_DIM = 64
D_V_DIM = 2 * D_QK_DIM
D_QK_WIDTH = D_HEADS * 2 * D_QK_DIM
D_V_WIDTH = D_HEADS * D_V_DIM
EPS = 1e-6

LANES = 128
SUBLANES = 8
ROW_BLOCK = 256
MATMUL_ROWS = 1100
K_SCALE = M_HEAD_DIM ** -0.5
SAMPLE_CHUNK = 128
KV_UNROLL = (4, 2, 1)
HEAD_GROUP = 8
PAGES_PER_STEP = 16
VMEM_LIMIT = 56 * 1024 * 1024

F32 = jnp.float32
BF16 = jnp.bfloat16
NEG_INF = float("-inf")
LOG2E = math.log2(math.e)
Q_SCALE = D_QK_DIM ** -0.5 * LOG2E
ONES_ROWS = 16


def _ffn_dim():
    return ((8 * D_MODEL // 3 + 255) // 256) * 256


def _rows():
    n_prompt = BATCH * SEQ
    n_sample = DEC_BATCH * DEC_SEQ
    assert n_prompt % ROW_BLOCK == 0 and n_sample % ROW_BLOCK == 0 and N_META <= ROW_BLOCK
    return n_prompt, n_sample, n_prompt + n_sample, n_prompt + n_sample + ROW_BLOCK


def _row_tile(r):
    return 512 if r % 512 == 0 else ROW_BLOCK


def _tile(n, cap, mult=16):
    for t in range(min(cap, n) // mult * mult, 0, -mult):
        if n % t == 0:
            return t
    raise ValueError((n, cap, mult))


def _col_tile(n):
    for t in (1024, 512, 256, 128):
        if n % t == 0:
            return t
    raise ValueError(n)


def _params(*sem):
    return pltpu.CompilerParams(dimension_semantics=sem, vmem_limit_bytes=VMEM_LIMIT)


def _dot(a, b):
    return jnp.dot(a, b, preferred_element_type=F32)


def _dot_nt(a, b):
    return lax.dot_general(a, b, (((1,), (1,)), ((), ())), preferred_element_type=F32)


def _dot_tn(a, b):
    return lax.dot_general(a, b, (((0,), (0,)), ((), ())), preferred_element_type=F32)


def _sigmoid(x):
    return 1.0 / (1.0 + jnp.exp(-x))


def _rms(x, gain):
    ms = jnp.mean(x * x, axis=-1, keepdims=True)
    return x * lax.rsqrt(ms + EPS) * gain


def _rmsnorm_kernel(x_ref, g_ref, o_ref):
    o_ref[...] = _rms(x_ref[...], g_ref[...]).astype(o_ref.dtype)


def _rmsnorm(x, gain, name):
    r, d = x.shape
    tm = _tile(r, MATMUL_ROWS)
    return pl.pallas_call(
        _rmsnorm_kernel,
        grid=(r // tm,),
        in_specs=[pl.BlockSpec((tm, d), lambda i: (i, 0)), pl.BlockSpec((1, d), lambda i: (0, 0))],
        out_specs=pl.BlockSpec((tm, d), lambda i: (i, 0)),
        out_shape=jax.ShapeDtypeStruct((r, d), BF16),
        compiler_params=_params("parallel"),
        name=name,
    )(x, gain.reshape(1, d))


def _rows_of(head_ref, tail_ref, i, n_head):
    return jnp.where(i < n_head, head_ref[...], tail_ref[...])


def _rmsnorm_split_kernel(head_ref, tail_ref, g_ref, o_ref, *, n_head):
    x = _rows_of(head_ref, tail_ref, pl.program_id(0), n_head)
    o_ref[...] = _rms(x, g_ref[...]).astype(o_ref.dtype)


def _rmsnorm_split(head, tail, gain, name):
    tm, d = tail.shape
    n_head = head.shape[0] // tm
    assert head.shape[0] % tm == 0
    return pl.pallas_call(
        functools.partial(_rmsnorm_split_kernel, n_head=n_head),
        grid=(n_head + 1,),
        in_specs=[pl.BlockSpec((tm, d), lambda i: (jnp.minimum(i, n_head - 1), 0)),
                  pl.BlockSpec((tm, d), lambda i: (0, 0)), pl.BlockSpec((1, d), lambda i: (0, 0))],
        out_specs=pl.BlockSpec((tm, d), lambda i: (i, 0)),
        out_shape=jax.ShapeDtypeStruct((head.shape[0] + tm, d), BF16),
        compiler_params=_params("parallel"),
        name=name,
    )(head, tail, gain.reshape(1, d))


def _weight_spec(w, layer, tn, col0):
    if w.ndim == 3:
        return pl.BlockSpec((None, w.shape[1], tn), lambda j, i: (layer, 0, col0 + j))
    return pl.BlockSpec((w.shape[0], tn), lambda j, i: (0, col0 + j))


def _cast_weights(w_refs, wb_refs):
    @pl.when(pl.program_id(1) == 0)
    def _():
        for w_ref, wb_ref in zip(w_refs, wb_refs):
            wb_ref[...] = w_ref[...].astype(BF16)


def _matmul_kernel(a_ref, w_ref, o_ref, wb_ref):
    _cast_weights([w_ref], [wb_ref])
    o_ref[...] = _dot(a_ref[...], wb_ref[...]).astype(o_ref.dtype)


def _matmul(a, w, layer, n, out_dtype, name, tn=None):
    r, k = a.shape
    tm, tn = _tile(r, MATMUL_ROWS), tn or _col_tile(n)
    return pl.pallas_call(
        _matmul_kernel,
        grid=(n // tn, r // tm),
        in_specs=[pl.BlockSpec((tm, k), lambda j, i: (i, 0)), _weight_spec(w, layer, tn, 0)],
        out_specs=pl.BlockSpec((tm, tn), lambda j, i: (i, j)),
        out_shape=jax.ShapeDtypeStruct((r, n), out_dtype),
        scratch_shapes=[pltpu.VMEM((k, tn), BF16)],
        compiler_params=_params("parallel", "arbitrary"),
        name=name,
    )(a, w)


def _matmul_nt_kernel(a_ref, w_ref, o_ref, wb_ref):
    _cast_weights([w_ref.at[0]], [wb_ref])
    o_ref[...] = _dot_nt(a_ref[...], wb_ref[...]).astype(o_ref.dtype)


def _matmul_nt(a, w_t, layer, row0, n, name, out_dtype=F32):
    r, k = a.shape
    tm, tn = _tile(r, MATMUL_ROWS), _tile(n, 1024, SUBLANES)
    assert row0 % SUBLANES == 0
    return pl.pallas_call(
        _matmul_nt_kernel,
        grid=(n // tn, r // tm),
        in_specs=[pl.BlockSpec((tm, k), lambda j, i: (i, 0)),
                  pl.BlockSpec((pl.Element(1), pl.Element(tn), pl.Element(k)),
                               lambda j, i: (layer, pl.multiple_of(row0 + j * tn, SUBLANES), 0))],
        out_specs=pl.BlockSpec((tm, tn), lambda j, i: (i, j)),
        out_shape=jax.ShapeDtypeStruct((r, n), out_dtype),
        scratch_shapes=[pltpu.VMEM((tn, k), BF16)],
        compiler_params=_params("parallel", "arbitrary"),
        name=name,
    )(a, w_t)


def _swiglu_kernel(a_ref, wg_ref, wu_ref, o_ref, wgb_ref, wub_ref):
    _cast_weights([wg_ref, wu_ref], [wgb_ref, wub_ref])
    a = a_ref[...]
    g = _dot(a, wgb_ref[...])
    u = _dot(a, wub_ref[...])
    o_ref[...] = (g * _sigmoid(g) * u).astype(o_ref.dtype)


def _swiglu(a, w_gu, layer):
    r, k = a.shape
    f = w_gu.shape[2] // 2
    tm, tn = _tile(r, MATMUL_ROWS), min(512, _col_tile(f))
    nj = f // tn
    return pl.pallas_call(
        _swiglu_kernel,
        grid=(nj, r // tm),
        in_specs=[pl.BlockSpec((tm, k), lambda j, i: (i, 0)),
                  _weight_spec(w_gu, layer, tn, 0), _weight_spec(w_gu, layer, tn, nj)],
        out_specs=pl.BlockSpec((tm, tn), lambda j, i: (i, j)),
        out_shape=jax.ShapeDtypeStruct((r, f), BF16),
        scratch_shapes=[pltpu.VMEM((k, tn), BF16)] * 2,
        compiler_params=_params("parallel", "arbitrary"),
        name="ffn_up",
    )(a, w_gu, w_gu)


def _matmul_res_kernel(a_ref, w_ref, r_ref, o_ref, wb_ref):
    _cast_weights([w_ref], [wb_ref])
    o_ref[...] = r_ref[...] + _dot(a_ref[...], wb_ref[...])


def _matmul_res(a, w, layer, res, name, rows, tn_cap):
    r, k = a.shape
    n = w.shape[2]
    tm, tn = _tile(r, rows), min(tn_cap, _col_tile(n))
    return pl.pallas_call(
        _matmul_res_kernel,
        grid=(n // tn, r // tm),
        in_specs=[pl.BlockSpec((tm, k), lambda j, i: (i, 0)), _weight_spec(w, layer, tn, 0),
                  pl.BlockSpec((tm, tn), lambda j, i: (i, j))],
        out_specs=pl.BlockSpec((tm, tn), lambda j, i: (i, j)),
        out_shape=jax.ShapeDtypeStruct((r, n), F32),
        scratch_shapes=[pltpu.VMEM((k, tn), BF16)],
        compiler_params=_params("parallel", "arbitrary"),
        name=name,
    )(a, w, res)


def _matmul_res_joined_kernel(a_ref, w_ref, rh_ref, rt_ref, o_ref, wb_ref, *, n_head):
    _cast_weights([w_ref], [wb_ref])
    o_ref[...] = _rows_of(rh_ref, rt_ref, pl.program_id(1), n_head) + _dot(a_ref[...], wb_ref[...])


def _matmul_res_joined(a, w, layer, res_head, res_tail, name):
    r, k = a.shape
    n = w.shape[2]
    tm, tn = res_tail.shape[0], min(1024, _col_tile(n))
    n_head = res_head.shape[0] // tm
    assert res_head.shape[0] % tm == 0 and r == res_head.shape[0] + tm
    return pl.pallas_call(
        functools.partial(_matmul_res_joined_kernel, n_head=n_head),
        grid=(n // tn, r // tm),
        in_specs=[pl.BlockSpec((tm, k), lambda j, i: (i, 0)), _weight_spec(w, layer, tn, 0),
                  pl.BlockSpec((tm, tn), lambda j, i: (jnp.minimum(i, n_head - 1), j)),
                  pl.BlockSpec((tm, tn), lambda j, i: (0, j))],
        out_specs=pl.BlockSpec((tm, tn), lambda j, i: (i, j)),
        out_shape=jax.ShapeDtypeStruct((r, n), F32),
        scratch_shapes=[pltpu.VMEM((k, tn), BF16)],
        compiler_params=_params("parallel", "arbitrary"),
        name=name,
    )(a, w, res_head, res_tail)


def _matmul_res_split_kernel(a_ref, w_ref, r_ref, head_ref, tail_ref, wb_ref, *, n_head):
    _cast_weights([w_ref], [wb_ref])
    i = pl.program_id(1)
    val = r_ref[...] + _dot(a_ref[...], wb_ref[...])

    @pl.when(i < n_head)
    def _():
        head_ref[...] = val

    @pl.when(i == n_head)
    def _():
        tail_ref[...] = val


def _matmul_res_split(a, w, layer, res, name, head_rows):
    r, k = a.shape
    n = w.shape[2]
    tm, tn = r - head_rows, min(512, _col_tile(n))
    assert head_rows % tm == 0
    n_head = head_rows // tm
    return pl.pallas_call(
        functools.partial(_matmul_res_split_kernel, n_head=n_head),
        grid=(n // tn, r // tm),
        in_specs=[pl.BlockSpec((tm, k), lambda j, i: (i, 0)), _weight_spec(w, layer, tn, 0),
                  pl.BlockSpec((tm, tn), lambda j, i: (i, j))],
        out_specs=[pl.BlockSpec((tm, tn), lambda j, i: (jnp.minimum(i, n_head - 1), j)),
                   pl.BlockSpec((tm, tn), lambda j, i: (0, j))],
        out_shape=[jax.ShapeDtypeStruct((head_rows, n), F32), jax.ShapeDtypeStruct((tm, n), F32)],
        scratch_shapes=[pltpu.VMEM((k, tn), BF16)],
        compiler_params=_params("arbitrary", "arbitrary"),
        name=name,
    )(a, w, res)


def _merge_kernel(hm_ref, hd_ref, wpm_ref, wpd_ref, gm_ref, gd_ref, o_ref, wpmb_ref, wpdb_ref):
    _cast_weights([wpm_ref, wpd_ref], [wpmb_ref, wpdb_ref])
    zm = _dot(hm_ref[...].astype(BF16), wpmb_ref[...])
    zd = _dot(hd_ref[...].astype(BF16), wpdb_ref[...])
    gm = gm_ref[...].astype(F32)
    gd = gd_ref[...].astype(F32)
    o_ref[...] = (_sigmoid(gm) * zm + _sigmoid(gd) * zd).astype(o_ref.dtype)


def _merge(hm, hd, w_pm, w_pd, layer, gates, gate_col):
    r = hm.shape[0]
    n = w_pm.shape[2]
    tm = _tile(r, MATMUL_ROWS // 4)
    assert gate_col % LANES == 0

    def gate_spec(col0):
        return pl.BlockSpec((pl.Element(tm), pl.Element(n)), lambda j, i: (pl.multiple_of(i * tm, tm), col0))

    def resident(w):
        return pl.BlockSpec((None, w.shape[1], n), lambda j, i: (layer, 0, 0), pipeline_mode=pl.Buffered(1))

    return pl.pallas_call(
        _merge_kernel,
        grid=(1, r // tm),
        in_specs=[pl.BlockSpec((tm, M_WIDTH), lambda j, i: (i, 0)),
                  pl.BlockSpec((tm, D_V_WIDTH), lambda j, i: (i, 0)),
                  resident(w_pm), resident(w_pd), gate_spec(gate_col), gate_spec(gate_col + n)],
        out_specs=pl.BlockSpec((tm, n), lambda j, i: (i, 0)),
        out_shape=jax.ShapeDtypeStruct((r, n), BF16),
        scratch_shapes=[pltpu.VMEM((M_WIDTH, n), BF16), pltpu.VMEM((D_V_WIDTH, n), BF16)],
        compiler_params=_params("arbitrary", "arbitrary"),
        name="merge",
    )(hm, hd, w_pm, w_pd, gates, gates)


def _half_norm(x, gain):
    lo = lax.broadcasted_iota(jnp.int32, x.shape, 1) < D_QK_DIM
    x2 = x * x
    s_lo = jnp.sum(jnp.where(lo, x2, 0.0), axis=1, keepdims=True)
    s_hi = jnp.sum(jnp.where(lo, 0.0, x2), axis=1, keepdims=True)
    ms = jnp.where(lo, s_lo, s_hi) * (1.0 / D_QK_DIM)
    return x * lax.rsqrt(ms + EPS) * gain


def _qk_norm_kernel(q_ref, k_ref, v_ref, qg_ref, kg_ref, *refs, n_prompt_tiles):
    qo_ref, kb_ref, vt_ref, kp_ref, vp_ref, kt_ref, vt8_ref = refs[-7:]
    i = pl.program_id(0)
    kn = []
    for h in range(D_HEADS):
        sl = slice(h * LANES, (h + 1) * LANES)
        qo_ref[:, sl] = _half_norm(q_ref[:, sl], qg_ref[...]) * Q_SCALE
        kn.append(_half_norm(k_ref[:, sl], kg_ref[...]))
        kb_ref[:, sl] = kn[h].astype(BF16)
    for j in range(vt_ref.shape[0]):
        vt_ref[j] = v_ref[j * ROW_BLOCK:(j + 1) * ROW_BLOCK, :].T.astype(BF16)

    def scatter(k_dst, v_dst):
        tm = k_ref.shape[0]
        for h in range(D_HEADS):
            rows_h = pl.ds(h, tm, stride=D_HEADS)
            k_dst[rows_h, :] = kn[h]
            v_dst[rows_h, :] = v_ref[:, h * LANES:(h + 1) * LANES]

    @pl.when(i < n_prompt_tiles)
    def _():
        scatter(kp_ref.at[0, 0], vp_ref.at[0, 0])

    @pl.when(i == n_prompt_tiles)
    def _():
        scatter(kt_ref, vt8_ref)


def _qk_norm(qkv, q_gain, k_gain, layer, k_out, v_out):
    r = qkv.shape[0]
    w = D_QK_WIDTH
    n_prompt = _rows()[0]
    tm = r - n_prompt
    assert n_prompt % tm == 0 and SEQ % tm == 0 and tm % ROW_BLOCK == 0
    n_prompt_tiles, tiles_per_seq = n_prompt // tm, SEQ // tm
    spec = lambda c: pl.BlockSpec((tm, w), lambda i: (i, c))
    gspec = pl.BlockSpec((1, LANES), lambda i: (0, 0))

    def prompt_map(i):
        t = jnp.minimum(i, n_prompt_tiles - 1)
        row = pl.multiple_of((N_META + (t % tiles_per_seq) * tm) * D_HEADS, SUBLANES)
        return (layer, t // tiles_per_seq, row, 0)

    pspec = pl.BlockSpec(tuple(pl.Element(s) for s in (1, 1, tm * D_HEADS, LANES)), prompt_map)
    tspec = pl.BlockSpec((tm * D_HEADS, LANES), lambda i: (0, 0))
    out5 = jax.ShapeDtypeStruct((DEPTH, BATCH, (N_META + SEQ) * D_HEADS, LANES), F32)
    tail = jax.ShapeDtypeStruct((tm * D_HEADS, LANES), F32)
    carried = [] if k_out is None else [k_out, v_out]
    n_in = 5 + len(carried)
    return pl.pallas_call(
        functools.partial(_qk_norm_kernel, n_prompt_tiles=n_prompt_tiles),
        grid=(r // tm,),
        in_specs=[spec(0), spec(1), spec(2), gspec, gspec] + [pl.BlockSpec(memory_space=pl.ANY)] * len(carried),
        out_specs=[spec(0), spec(0), pl.BlockSpec((tm // ROW_BLOCK, D_V_WIDTH, ROW_BLOCK), lambda i: (i, 0, 0)),
                   pspec, pspec, tspec, tspec],
        out_shape=[jax.ShapeDtypeStruct((r, w), F32), jax.ShapeDtypeStruct((r, w), BF16),
                   jax.ShapeDtypeStruct((r // ROW_BLOCK, D_V_WIDTH, ROW_BLOCK), BF16), out5, out5, tail, tail],
        input_output_aliases={} if k_out is None else {n_in - 2: 3, n_in - 1: 4},
        compiler_params=_params("arbitrary"),
        name="qk_norm",
    )(qkv, qkv, qkv, jnp.tile(q_gain, 2).reshape(1, LANES), jnp.tile(k_gain, 2).reshape(1, LANES), *carried)


def _split3(x):
    hi = x.astype(BF16)
    r1 = x - hi.astype(F32)
    mid = r1.astype(BF16)
    lo = (r1 - mid.astype(F32)).astype(BF16)
    return hi, mid, lo


def _log_sigmoid(x):
    return jnp.minimum(x, 0.0) - jnp.log1p(jnp.exp(-jnp.abs(x)))


def _mlstm_gates(graw_col, graw_row, bias_col, bias_row, n_valid):
    l = graw_col.shape[0]
    g_col = graw_col + bias_col
    g_row = graw_row + bias_row
    is_f_col = lax.broadcasted_iota(jnp.int32, g_col.shape, 1) >= M_HEADS
    is_f_row = lax.broadcasted_iota(jnp.int32, g_row.shape, 0) >= M_HEADS
    ok_col = lax.broadcasted_iota(jnp.int32, g_col.shape, 0) < n_valid
    ok_row = lax.broadcasted_iota(jnp.int32, g_row.shape, 1) < n_valid
    lf_col = jnp.where(is_f_col & ok_col, _log_sigmoid(g_col), 0.0)
    lf_row = jnp.where(is_f_row & ok_row, _log_sigmoid(g_row), 0.0)
    ig_col = jnp.where(ok_col, g_col, NEG_INF)
    ig_row = jnp.where(ok_row, g_row, NEG_INF)
    rr = lax.broadcasted_iota(jnp.int32, (l, l), 0)
    cc = lax.broadcasted_iota(jnp.int32, (l, l), 1)
    tri_l = (cc <= rr).astype(BF16)
    tri_u = (rr <= cc).astype(BF16)
    b_col = sum(_dot(tri_l, p) for p in _split3(lf_col))
    b_row = sum(_dot(p, tri_u) for p in _split3(lf_row))
    return ig_col, b_col, ig_row, b_row, cc <= rr


def _mlstm_chunk(q, k, v, ig_col, b_col, ig_row, b_row, causal, c_st, n_st, m_st):
    l = q.shape[0]
    d = b_col - b_row + ig_row
    d = jnp.where(causal, d, NEG_INF)
    inter = b_col + m_st
    m_t = jnp.maximum(inter, jnp.max(d, axis=1, keepdims=True))
    w = jnp.exp(d - m_t)
    s = _dot_nt(q, k) * w
    a = jnp.exp(inter - m_t)
    num = _dot(s.astype(BF16), v) + a * _dot_nt(q, c_st.astype(BF16))
    den = jnp.sum(s, axis=1, keepdims=True) + a * jnp.sum(q.astype(F32) * n_st, axis=1, keepdims=True)
    h = num * (1.0 / jnp.maximum(jnp.abs(den), jnp.exp(-m_t)))

    b_last = b_col[l - 1:l, :]
    g_col = b_last - b_col + ig_col
    g_row = b_last - b_row + ig_row
    m_new = jnp.maximum(b_last + m_st, jnp.max(g_row, axis=1, keepdims=True))
    ws_col = jnp.exp(g_col - m_new)
    ws_row = jnp.exp(g_row - m_new)
    decay = jnp.exp(b_last + m_st - m_new)
    wv = (v.astype(F32) * ws_col).astype(BF16)
    c_new = decay * c_st + _dot_tn(wv, k)
    n_new = decay * n_st + _dot(ws_row.astype(BF16), k)
    return h, c_new, n_new, m_new


def _mlstm_heads(q_of, k_of, v_of, om_of, gates, gain_ref, c_ref, n_ref, m_ref, h_out):
    ig_col, b_col, ig_row, b_row, causal = gates
    for h in range(M_HEADS):
        f = M_HEADS + h
        hs = slice(h * M_HEAD_DIM, (h + 1) * M_HEAD_DIM)
        out, c_new, n_new, m_new = _mlstm_chunk(
            q_of(hs), k_of(hs), v_of(hs),
            ig_col[:, h:h + 1], b_col[:, f:f + 1], ig_row[h:h + 1, :], b_row[f:f + 1, :], causal,
            c_ref[0, h], n_ref[0, h:h + 1, :], m_ref[0, h:h + 1, 0:1])
        c_ref[0, h] = c_new
        n_ref[0, h:h + 1, :] = n_new
        m_ref[0, h:h + 1, :] = jnp.broadcast_to(m_new, (1, LANES))
        h_out(hs, _rms(out, gain_ref[:, hs]) * _sigmoid(om_of(hs)))


def _mlstm_prompt_kernel(q_ref, k_ref, v_ref, om_ref, gc_ref, gr_ref, bc_ref, br_ref, gain_ref,
                         h_ref, c_ref, n_ref, m_ref):
    c = pl.program_id(1)

    @pl.when(c == 0)
    def _():
        c_ref[...] = jnp.zeros_like(c_ref)
        n_ref[...] = jnp.zeros_like(n_ref)
        m_ref[...] = jnp.zeros_like(m_ref)

    n_valid = jnp.where(c == 0, N_META, ROW_BLOCK)
    gates = _mlstm_gates(gc_ref[...], gr_ref[...], bc_ref[...], br_ref[...], n_valid)

    def h_out(hs, val):
        h_ref[:, hs] = val

    _mlstm_heads(lambda hs: q_ref[:, hs].astype(BF16), lambda hs: (k_ref[:, hs] * K_SCALE).astype(BF16),
                 lambda hs: v_ref[:, hs].astype(BF16), lambda hs: om_ref[:, hs],
                 gates, gain_ref, c_ref, n_ref, m_ref, h_out)


def _mlstm_prompt(qkv, gates, g_col, g_row, b_if, gain):
    r = qkv.shape[0]
    n_prompt, _, meta_row, _ = _rows()
    nc = SEQ // ROW_BLOCK
    meta_blk = meta_row // ROW_BLOCK

    def rb(b, c):
        return jnp.where(c == 0, meta_blk, b * nc + c - 1)

    def rb_out(b, c):
        return jnp.where((c == 0) & (b > 0), b * nc, rb(b, c))

    spec = lambda col: pl.BlockSpec((ROW_BLOCK, M_WIDTH), lambda b, c: (rb(b, c), col))
    full = lambda shape: pl.BlockSpec(shape, lambda b, c: (0,) * len(shape))
    return pl.pallas_call(
        _mlstm_prompt_kernel,
        grid=(BATCH, nc + 1),
        in_specs=[spec(0), spec(1), spec(2), spec(3),
                  pl.BlockSpec((ROW_BLOCK, 2 * M_HEADS), lambda b, c: (rb(b, c), 0)),
                  pl.BlockSpec((2 * M_HEADS, ROW_BLOCK), lambda b, c: (0, rb(b, c))),
                  full((1, 2 * M_HEADS)), full((2 * M_HEADS, 1)), full((1, M_WIDTH))],
        out_specs=[pl.BlockSpec((ROW_BLOCK, M_WIDTH), lambda b, c: (rb_out(b, c), 0)),
                   pl.BlockSpec((1, M_HEADS, M_HEAD_DIM, M_HEAD_DIM), lambda b, c: (b, 0, 0, 0)),
                   pl.BlockSpec((1, M_HEADS, M_HEAD_DIM), lambda b, c: (b, 0, 0)),
                   pl.BlockSpec((1, M_HEADS, LANES), lambda b, c: (b, 0, 0))],
        out_shape=[jax.ShapeDtypeStruct((r, M_WIDTH), F32),
                   jax.ShapeDtypeStruct((BATCH, M_HEADS, M_HEAD_DIM, M_HEAD_DIM), F32),
                   jax.ShapeDtypeStruct((BATCH, M_HEADS, M_HEAD_DIM), F32),
                   jax.ShapeDtypeStruct((BATCH, M_HEADS, LANES), F32)],
        compiler_params=_params("arbitrary", "arbitrary"),
        name="mlstm_prompt",
    )(qkv, qkv, qkv, gates, g_col, g_row, b_if.reshape(1, -1), b_if.reshape(-1, 1), gain.reshape(1, -1))


def _mlstm_sample_kernel(q_ref, k_ref, v_ref, om_ref, gc_ref, gr_ref, bc_ref, br_ref, gain_ref,
                         c_in, n_in, m_in, *rest):
    h_ref, c_ref, n_ref, m_ref, qp, kp, vp = rest[-7:]
    c_ref[...] = c_in[...]
    n_ref[...] = n_in[...]
    m_ref[...] = m_in[...]
    for src, dst in ((q_ref, qp), (k_ref, kp), (v_ref, vp)):
        dst[...] = jnp.zeros_like(dst)
        dst[0:DEC_SEQ, :] = src[...]
    gates = _mlstm_gates(gc_ref[0], gr_ref[0], bc_ref[...], br_ref[...], DEC_SEQ)
    om = om_ref[...]

    def h_out(hs, val):
        h_ref[:, hs] = val[0:DEC_SEQ, :]

    def om_of(hs):
        return jnp.concatenate([om[:, hs], jnp.zeros((SAMPLE_CHUNK - DEC_SEQ, M_HEAD_DIM), F32)], axis=0)

    _mlstm_heads(lambda hs: qp[:, hs].astype(BF16), lambda hs: (kp[:, hs] * K_SCALE).astype(BF16),
                 lambda hs: vp[:, hs].astype(BF16), om_of,
                 gates, gain_ref, c_ref, n_ref, m_ref, h_out)


def _mlstm_sample(qkv, gates, g_col, g_row, b_if, gain, layer, state_c, state_n, st_m, hm, c_out, n_out):
    assert DEC_SEQ == SUBLANES
    n_prompt = _rows()[0]
    blk0 = n_prompt // DEC_SEQ
    spec = lambda col: pl.BlockSpec((DEC_SEQ, M_WIDTH), lambda b: (blk0 + b, col))
    full = lambda shape: pl.BlockSpec(shape, lambda b: (0,) * len(shape))
    st_spec = lambda shape: pl.BlockSpec((1,) + shape, lambda b: (b,) + (0,) * len(shape))
    layer_spec = lambda shape: pl.BlockSpec((None, 1) + shape, lambda b: (layer, b) + (0,) * len(shape))
    st_specs = [layer_spec((M_HEADS, M_HEAD_DIM, M_HEAD_DIM)), layer_spec((M_HEADS, M_HEAD_DIM)),
                st_spec((M_HEADS, LANES))]
    carried = [] if c_out is None else [c_out, n_out]
    aliases = {12: 0}
    if carried:
        aliases.update({13: 1, 14: 2})
    return pl.pallas_call(
        _mlstm_sample_kernel,
        grid=(DEC_BATCH,),
        in_specs=[spec(0), spec(1), spec(2), spec(3),
                  st_spec((SAMPLE_CHUNK, 2 * M_HEADS)), st_spec((2 * M_HEADS, SAMPLE_CHUNK)),
                  full((1, 2 * M_HEADS)), full((2 * M_HEADS, 1)), full((1, M_WIDTH))] + st_specs
                 + [pl.BlockSpec(memory_space=pl.ANY)] * (1 + len(carried)),
        out_specs=[spec(0)] + st_specs,
        out_shape=[jax.ShapeDtypeStruct(hm.shape, F32),
                   jax.ShapeDtypeStruct(state_c.shape, F32), jax.ShapeDtypeStruct(state_n.shape, F32),
                   jax.ShapeDtypeStruct((DEC_BATCH, M_HEADS, LANES), F32)],
        scratch_shapes=[pltpu.VMEM((SAMPLE_CHUNK, M_WIDTH), F32)] * 3,
        input_output_aliases=aliases,
        compiler_params=_params("arbitrary"),
        name="mlstm_sample",
    )(qkv, qkv, qkv, gates, g_col, g_row, b_if.reshape(1, -1), b_if.reshape(-1, 1), gain.reshape(1, -1),
      state_c, state_n, jnp.broadcast_to(st_m[:, :, None], (DEC_BATCH, M_HEADS, LANES)), hm, *carried)


def _lambda(lam_ref, lam_init):
    p = lam_ref[...]
    s1 = jnp.sum(p[0:1, :] * p[1:2, :], axis=1, keepdims=True)
    s2 = jnp.sum(p[2:3, :] * p[3:4, :], axis=1, keepdims=True)
    return jnp.exp(s1) - jnp.exp(s2) + lam_init


def _diff_prompt_kernel(slope_ref, q_ref, k_ref, vt_ref, km_ref, vtm_ref, lam_ref, gain_ref, o_ref,
                        base_ref, m_ref, acc_ref, *, lam_init):
    b = pl.program_id(0)
    hg = pl.program_id(1)
    is_meta = pl.program_id(2) == 0
    t = pl.program_id(2) - 1
    tq = ROW_BLOCK
    heads = range(HEAD_GROUP)
    cols = [slice(i * LANES, (i + 1) * LANES) for i in heads]
    slopes = [slope_ref[hg * HEAD_GROUP + i] * LOG2E for i in heads]

    @pl.when(jnp.logical_not(is_meta & (b > 0)))
    def _():
        lo = lax.broadcasted_iota(jnp.int32, (tq, LANES), 1) < D_QK_DIM
        q2 = []
        for i in heads:
            q = q_ref[:, cols[i]].astype(BF16)
            zero = jnp.zeros_like(q)
            q2.append(jnp.concatenate([jnp.where(lo, q, zero), jnp.where(lo, zero, q)], axis=0))
            base_ref[i] = slopes[i] * lax.broadcasted_iota(jnp.int32, (tq, 2 * tq), 0).astype(F32)
        q_pos0 = jnp.where(is_meta, 0, N_META + t * tq)
        ones = jnp.ones((ONES_ROWS, tq), BF16)
        m_ref[...] = jnp.full_like(m_ref, NEG_INF)
        acc_ref[...] = jnp.zeros_like(acc_ref)

        def visible(n_keys, k_off):
            key = lax.broadcasted_iota(jnp.int32, (n_keys, 2 * tq), 0) + k_off
            qry = lax.broadcasted_iota(jnp.int32, (n_keys, 2 * tq), 1)
            return key <= jnp.where(qry >= tq, qry - tq, qry)

        def step(i, parts):
            cs = [slopes[i] * k_off.astype(F32) for _, _, k_off in parts]
            m_old = m_ref[i]
            m_new = m_old
            for (s, _, _), c in zip(parts, cs):
                m_new = jnp.maximum(m_new, jnp.max(s, axis=0, keepdims=True) + c)
            acc = jnp.exp2(m_old - m_new) * acc_ref[i]
            for (s, vt, _), c in zip(parts, cs):
                p = jnp.exp2(s - (m_new - c)).astype(BF16)
                acc = acc + _dot(jnp.concatenate([vt, ones[:, 0:vt.shape[1]]], axis=0), p)
            acc_ref[i] = acc
            m_ref[i] = m_new

        def scores(kb, i):
            ks = pl.ds(pl.multiple_of(kb * tq, tq), tq)
            return _dot_nt(k_ref[ks, cols[i]], q2[i])

        def full_blocks(kbs):
            for i in heads:
                step(i, [(scores(kb, i) + base_ref[i], vt_ref[kb, cols[i], :], N_META + kb * tq - q_pos0)
                         for kb in kbs])

        n_full = jnp.where(is_meta, 0, t)
        done = 0
        for width in KV_UNROLL:
            n_iter = (n_full - done) // width

            def body(j, carry, width=width, done=done):
                full_blocks([done + j * width + u for u in range(width)])
                return carry

            lax.fori_loop(0, n_iter, body, 0)
            done = done + n_iter * width

        td = jnp.maximum(t, 0)
        hide = jnp.where(is_meta, 2 * tq, 0)
        for i in heads:
            s_meta = _dot_nt(km_ref[:, cols[i]], q2[i]) + base_ref[i, 0:N_META, :]
            s_diag = scores(td, i) + base_ref[i]
            step(i, [(jnp.where(visible(N_META, -q_pos0), s_meta, NEG_INF), vtm_ref[0, cols[i], 0:N_META], -q_pos0),
                     (jnp.where(visible(tq, hide), s_diag, NEG_INF), vt_ref[td, cols[i], :], jnp.int32(0))])

        lam = _lambda(lam_ref, lam_init)
        for i in heads:
            acc = acc_ref[i]
            o_t = acc[0:D_V_DIM, :] * (1.0 / acc[D_V_DIM:D_V_DIM + 1, :])
            o = (o_t[:, 0:tq] - lam * o_t[:, tq:2 * tq]).T
            o_ref[:, cols[i]] = _rms(o, gain_ref[:, cols[i]]) * (1.0 - lam_init)


def _diff_prompt(qn, kb, vt, lam_pack, gain, slopes, lam_init):
    r = qn.shape[0]
    meta_row = _rows()[2]
    nq = SEQ // ROW_BLOCK
    meta_blk = meta_row // ROW_BLOCK
    gw = HEAD_GROUP * LANES
    assert meta_row % N_META == 0 and D_V_DIM == LANES and D_HEADS % HEAD_GROUP == 0

    def qmap(b, h, qi, *_):
        return (jnp.where(qi == 0, jnp.where(b == 0, meta_blk, b * nq), b * nq + qi - 1), h)

    grid_spec = pltpu.PrefetchScalarGridSpec(
        num_scalar_prefetch=1,
        grid=(BATCH, D_HEADS // HEAD_GROUP, nq + 1),
        in_specs=[pl.BlockSpec((ROW_BLOCK, gw), qmap),
                  pl.BlockSpec((SEQ, gw), lambda b, h, qi, *_: (b, h)),
                  pl.BlockSpec((nq, gw, ROW_BLOCK), lambda b, h, qi, *_: (b, h, 0)),
                  pl.BlockSpec((N_META, gw), lambda b, h, qi, *_: (meta_row // N_META, h)),
                  pl.BlockSpec((1, gw, ROW_BLOCK), lambda b, h, qi, *_: (meta_blk, h, 0)),
                  pl.BlockSpec((SUBLANES, LANES), lambda b, h, qi, *_: (0, 0)),
                  pl.BlockSpec((1, gw), lambda b, h, qi, *_: (0, h))],
        out_specs=pl.BlockSpec((ROW_BLOCK, gw), qmap),
        scratch_shapes=[pltpu.VMEM((HEAD_GROUP, ROW_BLOCK, 2 * ROW_BLOCK), F32),
                        pltpu.VMEM((HEAD_GROUP, 1, 2 * ROW_BLOCK), F32),
                        pltpu.VMEM((HEAD_GROUP, D_V_DIM + ONES_ROWS, 2 * ROW_BLOCK), F32)])
    return pl.pallas_call(
        functools.partial(_diff_prompt_kernel, lam_init=lam_init),
        grid_spec=grid_spec,
        out_shape=jax.ShapeDtypeStruct((r, D_V_WIDTH), F32),
        compiler_params=_params("arbitrary", "arbitrary", "arbitrary"),
        name="diff_prompt",
    )(slopes, qn, kb, vt, kb, vt, lam_pack, gain.reshape(1, -1))


def _diff_sample_kernel(pt_ref, *refs, lam_init):
    del pt_ref
    pp = PAGES_PER_STEP
    k_pages, v_pages = refs[0:pp], refs[pp:2 * pp]
    (q_ref, kn_ref, vn_ref, lam_ref, gain_ref, hd_in, o_ref,
     qb_ref, kp_ref, vp_ref, m_ref, acc_ref) = refs[2 * pp:]
    del hd_in
    g = pl.program_id(1)
    qrows = 2 * DEC_SEQ
    half = D_HEADS * DEC_SEQ
    slopes = [2.0 ** (-8.0 * (h + 1) / D_HEADS) * LOG2E for h in range(D_HEADS)]

    def step(h, s, v):
        m_old = m_ref[h]
        m_new = jnp.maximum(m_old, jnp.max(s, axis=1, keepdims=True))
        p = jnp.exp2(s - m_new).astype(BF16)
        v_ones = jnp.concatenate([v, jnp.ones_like(v)], axis=1)
        acc_ref[h] = jnp.exp2(m_old - m_new) * acc_ref[h] + _dot(p, v_ones)
        m_ref[h] = m_new

    @pl.when(g == 0)
    def _():
        lo = lax.broadcasted_iota(jnp.int32, (DEC_SEQ, LANES), 1) < D_QK_DIM
        for h in range(D_HEADS):
            x = q_ref[:, h * LANES:(h + 1) * LANES]
            qb_ref[h] = jnp.concatenate([jnp.where(lo, x, 0.0), jnp.where(lo, 0.0, x)], axis=0).astype(BF16)
        m_ref[...] = jnp.full_like(m_ref, NEG_INF)
        acc_ref[...] = jnp.zeros_like(acc_ref)

    k_pos = (lax.broadcasted_iota(jnp.int32, (1, pp * PAGE_SIZE), 1) + (g * (pp * PAGE_SIZE) - PAST_LEN)).astype(F32)
    for h in range(D_HEADS):
        rows_h = pl.ds(h, PAGE_SIZE, stride=D_HEADS)
        kh = jnp.concatenate([k_pages[p][rows_h, :] for p in range(pp)], axis=0).astype(BF16)
        vh = jnp.concatenate([v_pages[p][rows_h, :] for p in range(pp)], axis=0).astype(BF16)
        step(h, _dot_nt(qb_ref[h], kh) + slopes[h] * k_pos, vh)

    @pl.when(g == pl.num_programs(1) - 1)
    def _():
        for src, dst in ((kn_ref, kp_ref), (vn_ref, vp_ref)):
            dst[...] = jnp.zeros_like(dst)
            dst[0:half, :] = src[...]
        kn = kp_ref[...].astype(BF16)
        vn = vp_ref[...].astype(BF16)
        c = lax.broadcasted_iota(jnp.int32, (qrows, LANES), 1)
        r_tok = lax.broadcasted_iota(jnp.int32, (qrows, LANES), 0) % DEC_SEQ
        c_tok = c // D_HEADS
        lam = _lambda(lam_ref, lam_init)
        for h in range(D_HEADS):
            ok = (c % D_HEADS == h) & (c_tok <= r_tok)
            s = _dot_nt(qb_ref[h], kn) + slopes[h] * c_tok.astype(F32)
            step(h, jnp.where(ok, s, NEG_INF), vn)
            acc = acc_ref[h]
            o = acc[:, 0:LANES] * (1.0 / acc[:, LANES:2 * LANES])
            oh = o[0:DEC_SEQ, :] - lam * o[DEC_SEQ:qrows, :]
            cs = slice(h * LANES, (h + 1) * LANES)
            o_ref[:, cs] = _rms(oh, gain_ref[:, cs]) * (1.0 - lam_init)


def _diff_sample(layer, cache_k, cache_v, page_table, qn, kf, vf, lam_pack, gain, hd, lam_init):
    assert DEC_SEQ == SUBLANES and D_HEADS == SUBLANES and D_V_DIM == LANES
    assert D_HEADS * DEC_SEQ <= LANES and PAST_LEN % (PAGE_SIZE * PAGES_PER_STEP) == 0
    n_prompt = _rows()[0]
    blk0 = n_prompt // DEC_SEQ
    pp = PAGES_PER_STEP
    width = D_HEADS * LANES
    rows = 2 * D_HEADS * DEC_SEQ

    def as_rows(cache):
        return cache.reshape(cache.shape[0], cache.shape[1], PAGE_SIZE * D_HEADS, LANES)

    def page_spec(p):
        return pl.BlockSpec((None, None, PAGE_SIZE * D_HEADS, LANES),
                            lambda b, g, pt: (layer, pt[b, g * pp + p], 0, 0))

    row_spec = pl.BlockSpec((DEC_SEQ, width), lambda b, g, pt: (blk0 + b, 0))
    new_spec = pl.BlockSpec((DEC_SEQ * D_HEADS, LANES), lambda b, g, pt: (b, 0))
    grid_spec = pltpu.PrefetchScalarGridSpec(
        num_scalar_prefetch=1,
        grid=(DEC_BATCH, PAST_LEN // (PAGE_SIZE * pp)),
        in_specs=[page_spec(p) for p in range(pp)] * 2
                 + [row_spec, new_spec, new_spec,
                    pl.BlockSpec((SUBLANES, LANES), lambda b, g, pt: (0, 0)),
                    pl.BlockSpec((1, width), lambda b, g, pt: (0, 0)),
                    pl.BlockSpec(memory_space=pl.ANY)],
        out_specs=row_spec,
        scratch_shapes=[pltpu.VMEM((D_HEADS, 2 * DEC_SEQ, LANES), BF16),
                        pltpu.VMEM((LANES, LANES), F32), pltpu.VMEM((LANES, LANES), F32),
                        pltpu.VMEM((D_HEADS, 2 * DEC_SEQ, 1), F32), pltpu.VMEM((D_HEADS, 2 * DEC_SEQ, 2 * LANES), F32)])
    n_in = 2 * pp + 6
    return pl.pallas_call(
        functools.partial(_diff_sample_kernel, lam_init=lam_init),
        grid_spec=grid_spec,
        out_shape=jax.ShapeDtypeStruct(hd.shape, F32),
        input_output_aliases={n_in: 0},
        compiler_params=_params("arbitrary", "arbitrary"),
        name="diff_sample",
    )(page_table, *([as_rows(cache_k)] * pp), *([as_rows(cache_v)] * pp), qn, kf, vf, lam_pack,
      gain.reshape(1, -1), hd)


def kernel(x_prompt, x_sample, cache_k, cache_v, state_C, state_n, state_m, page_table, meta_tokens, norm_mix, w_in, b_if, q_gain, k_gain, lambda_q1, lambda_k1, lambda_q2, lambda_k2, mlstm_norm, diff_norm, w_pm, w_pd, w_out, norm_ffn, w_gu, w_down):
    assert M_HEAD_DIM ** -0.5 == 2.0 ** round(math.log2(M_HEAD_DIM ** -0.5))
    n_prompt, n_sample, meta_row, r = _rows()
    x_head = x_prompt.reshape(n_prompt, D_MODEL)
    x_tail = jnp.concatenate([x_sample.reshape(n_sample, D_MODEL), meta_tokens.astype(F32),
                              jnp.zeros((r - meta_row - N_META, D_MODEL), F32)], axis=0)
    x = None
    slopes = 2.0 ** (-8.0 * jnp.arange(1, D_HEADS + 1, dtype=F32) / D_HEADS)
    w_in_t = jnp.swapaxes(w_in, 1, 2)
    g0 = 4 * M_WIDTH
    g1 = g0 + 2 * M_HEADS

    k_prompt = v_prompt = c_sample = n_sample_st = None
    meta_kv = []
    outs = [[] for _ in range(6)]
    for l in range(DEPTH):
        lam_init = 0.8 - 0.6 * math.exp(-0.3 * l)
        if l == 0:
            xn = _rmsnorm_split(x_head, x_tail, norm_mix[l], "norm_mix_first")
        else:
            xn = _rmsnorm(x, norm_mix[l], "norm_mix")
        proj_m = _matmul_nt(xn, w_in_t, l, 0, g0, "proj_mlstm")
        graw = _matmul_nt(xn, w_in_t, l, g0, 2 * M_HEADS, "proj_gates")
        g2 = g1 + 2 * D_QK_WIDTH + D_V_WIDTH
        proj_d = _matmul_nt(xn, w_in_t, l, g1, g2 - g1, "proj_diff")
        proj_g = _matmul_nt(xn, w_in_t, l, g2, w_in_t.shape[1] - g2, "proj_merge_gates", BF16)

        hm, c_p, n_p, m_p = _mlstm_prompt(proj_m, proj_m, graw, graw.T, b_if[l], mlstm_norm[l])
        gs = graw[n_prompt:n_prompt + n_sample].reshape(DEC_BATCH, DEC_SEQ, 2 * M_HEADS)
        gs = jnp.pad(gs, ((0, 0), (0, SAMPLE_CHUNK - DEC_SEQ), (0, 0)))
        hm, c_sample, n_sample_st, m_s = _mlstm_sample(
            proj_m, proj_m, gs, gs.transpose(0, 2, 1), b_if[l], mlstm_norm[l], l, state_C.astype(F32),
            state_n.astype(F32), state_m[l].astype(F32), hm, c_sample, n_sample_st)

        qn, kb, vt, k_prompt, v_prompt, kf, vf = _qk_norm(proj_d, q_gain[l], k_gain[l], l, k_prompt, v_prompt)
        lam_pack = jnp.pad(jnp.stack([lambda_q1[l], lambda_k1[l], lambda_q2[l], lambda_k2[l]]).astype(F32),
                           ((0, SUBLANES - 4), (0, LANES - D_QK_DIM)))
        hd = _diff_prompt(qn, kb, vt, lam_pack, diff_norm[l], slopes, lam_init)
        hd = _diff_sample(l, cache_k, cache_v, page_table, qn, kf, vf, lam_pack, diff_norm[l], hd, lam_init)

        z = _merge(hm, hd, w_pm, w_pd, l, proj_g, 0)
        if l == 0:
            x = _matmul_res_joined(z, w_out, l, x_head, x_tail, "out_proj_first")
        else:
            x = _matmul_res(z, w_out, l, x, "out_proj", MATMUL_ROWS, 1024)
        act = _swiglu(_rmsnorm(x, norm_ffn[l], "norm_ffn"), w_gu, l)
        if l < DEPTH - 1:
            x = _matmul_res(act, w_down, l, x, "ffn_down", MATMUL_ROWS // 2, 512)
        else:
            y_head, y_tail = _matmul_res_split(act, w_down, l, x, "ffn_down_last", n_prompt)

        head_shape = (D_HEADS, D_V_DIM)
        s_rows, m_rows = n_sample * D_HEADS, N_META * D_HEADS
        meta_kv.append((kf[s_rows:s_rows + m_rows], vf[s_rows:s_rows + m_rows]))
        vals = (c_p, n_p, m_p[:, :, 0],
                kf[:s_rows].reshape(DEC_BATCH, DEC_SEQ, *head_shape),
                vf[:s_rows].reshape(DEC_BATCH, DEC_SEQ, *head_shape), m_s[:, :, 0])
        for o, v in zip(outs, vals):
            o.append(v)

    for l, (km, vm) in enumerate(meta_kv):
        k_prompt = k_prompt.at[l, :, :km.shape[0]].set(jnp.broadcast_to(km, (BATCH,) + km.shape))
        v_prompt = v_prompt.at[l, :, :vm.shape[0]].set(jnp.broadcast_to(vm, (BATCH,) + vm.shape))
    kv_shape = (DEPTH, BATCH, N_META + SEQ, D_HEADS, D_V_DIM)
    y_prompt = y_head.reshape(BATCH, SEQ, D_MODEL)
    y_sample = y_tail[:n_sample].reshape(DEC_BATCH, DEC_SEQ, D_MODEL)
    c_p, n_p, m_p, k_s, v_s, m_s = (jnp.stack(o) for o in outs)
    return (y_prompt, y_sample, k_prompt.reshape(kv_shape), v_prompt.reshape(kv_shape), c_p, n_p, m_p, k_s, v_s,
            c_sample, n_sample_st, m_s)
```

```python
import functools
import math

import jax
import jax.numpy as jnp
from jax import lax
from jax.experimental import pallas as pl
from jax.experimental.pallas import tpu as pltpu

D_MODEL = 2048
BATCH = 2
SEQ = 4096
DEPTH = 2
DEC_BATCH = 32
DEC_SEQ = 8
PAST_LEN = 8192
PAGE_SIZE = 128
N_META = 16
M_HEADS = 4
M_HEAD_DIM = 256
M_WIDTH = M_HEADS * M_HEAD_DIM
D_HEADS = 8
D_QK_DIM = 64
D_V_DIM = 2 * D_QK_DIM
D_QK_WIDTH = D_HEADS * 2 * D_QK_DIM
D_V_WIDTH = D_HEADS * D_V_DIM
EPS = 1e-6

LANES = 128
SUBLANES = 8
ROW_BLOCK = 256
MATMUL_ROWS = 1100
K_SCALE = M_HEAD_DIM ** -0.5
SAMPLE_CHUNK = 128
KV_UNROLL = (4, 2, 1)
HEAD_GROUP = 8
PAGES_PER_STEP = 16
VMEM_LIMIT = 56 * 1024 * 1024

F32 = jnp.float32
BF16 = jnp.bfloat16
NEG_INF = float("-inf")
LOG2E = math.log2(math.e)
Q_SCALE = D_QK_DIM ** -0.5 * LOG2E
ONES_ROWS = 16


def _ffn_dim():
    return ((8 * D_MODEL // 3 + 255) // 256) * 256


def _rows():
    n_prompt = BATCH * SEQ
    n_sample = DEC_BATCH * DEC_SEQ
    assert n_prompt % ROW_BLOCK == 0 and n_sample % ROW_BLOCK == 0 and N_META <= ROW_BLOCK
    return n_prompt, n_sample, n_prompt + n_sample, n_prompt + n_sample + ROW_BLOCK


def _tile(n, cap, mult=16):
    for t in range(min(cap, n) // mult * mult, 0, -mult):
        if n % t == 0:
            return t
    raise ValueError((n, cap, mult))


def _col_tile(n):
    for t in (1024, 512, 256, 128):
        if n % t == 0:
            return t
    raise ValueError(n)


def _params(*sem):
    return pltpu.CompilerParams(dimension_semantics=sem, vmem_limit_bytes=VMEM_LIMIT)


def _dot(a, b):
    return jnp.dot(a, b, preferred_element_type=F32)


def _dot_nt(a, b):
    return lax.dot_general(a, b, (((1,), (1,)), ((), ())), preferred_element_type=F32)


def _dot_tn(a, b):
    return lax.dot_general(a, b, (((0,), (0,)), ((), ())), preferred_element_type=F32)


def _sigmoid(x):
    return 1.0 / (1.0 + jnp.exp(-x))


def _rms(x, gain):
    ms = jnp.mean(x * x, axis=-1, keepdims=True)
    return x * lax.rsqrt(ms + EPS) * gain


def _rmsnorm_kernel(x_ref, g_ref, o_ref):
    o_ref[...] = _rms(x_ref[...], g_ref[...]).astype(o_ref.dtype)


def _rmsnorm(x, gain, name):
    r, d = x.shape
    tm = _tile(r, MATMUL_ROWS)
    return pl.pallas_call(
        _rmsnorm_kernel,
        grid=(r // tm,),
        in_specs=[pl.BlockSpec((tm, d), lambda i: (i, 0)), pl.BlockSpec((1, d), lambda i: (0, 0))],
        out_specs=pl.BlockSpec((tm, d), lambda i: (i, 0)),
        out_shape=jax.ShapeDtypeStruct((r, d), BF16),
        compiler_params=_params("parallel"),
        name=name,
    )(x, gain.reshape(1, d))


def _rows_of(head_ref, tail_ref, i, n_head):
    return jnp.where(i < n_head, head_ref[...], tail_ref[...])


def _rmsnorm_split_kernel(head_ref, tail_ref, g_ref, o_ref, *, n_head):
    x = _rows_of(head_ref, tail_ref, pl.program_id(0), n_head)
    o_ref[...] = _rms(x, g_ref[...]).astype(o_ref.dtype)


def _rmsnorm_split(head, tail, gain, name):
    tm, d = tail.shape
    n_head = head.shape[0] // tm
    assert head.shape[0] % tm == 0
    return pl.pallas_call(
        functools.partial(_rmsnorm_split_kernel, n_head=n_head),
        grid=(n_head + 1,),
        in_specs=[pl.BlockSpec((tm, d), lambda i: (jnp.minimum(i, n_head - 1), 0)),
                  pl.BlockSpec((tm, d), lambda i: (0, 0)), pl.BlockSpec((1, d), lambda i: (0, 0))],
        out_specs=pl.BlockSpec((tm, d), lambda i: (i, 0)),
        out_shape=jax.ShapeDtypeStruct((head.shape[0] + tm, d), BF16),
        compiler_params=_params("parallel"),
        name=name,
    )(head, tail, gain.reshape(1, d))


def _weight_spec(w, layer, tn, col0):
    if w.ndim == 3:
        return pl.BlockSpec((None, w.shape[1], tn), lambda j, i: (layer, 0, col0 + j))
    return pl.BlockSpec((w.shape[0], tn), lambda j, i: (0, col0 + j))


def _cast_weights(w_refs, wb_refs):
    @pl.when(pl.program_id(1) == 0)
    def _():
        for w_ref, wb_ref in zip(w_refs, wb_refs):
            wb_ref[...] = w_ref[...].astype(BF16)


def _matmul_nt_kernel(a_ref, w_ref, o_ref, wb_ref):
    _cast_weights([w_ref.at[0]], [wb_ref])
    o_ref[...] = _dot_nt(a_ref[...], wb_ref[...]).astype(o_ref.dtype)


def _matmul_nt(a, w_t, layer, row0, n, name, out_dtype=F32):
    r, k = a.shape
    tm, tn = _tile(r, MATMUL_ROWS), _tile(n, 1024, SUBLANES)
    assert row0 % SUBLANES == 0
    return pl.pallas_call(
        _matmul_nt_kernel,
        grid=(n // tn, r // tm),
        in_specs=[pl.BlockSpec((tm, k), lambda j, i: (i, 0)),
                  pl.BlockSpec((pl.Element(1), pl.Element(tn), pl.Element(k)),
                               lambda j, i: (layer, pl.multiple_of(row0 + j * tn, SUBLANES), 0))],
        out_specs=pl.BlockSpec((tm, tn), lambda j, i: (i, j)),
        out_shape=jax.ShapeDtypeStruct((r, n), out_dtype),
        scratch_shapes=[pltpu.VMEM((tn, k), BF16)],
        compiler_params=_params("parallel", "arbitrary"),
        name=name,
    )(a, w_t)


def _swiglu_kernel(a_ref, wg_ref, wu_ref, o_ref, wgb_ref, wub_ref):
    _cast_weights([wg_ref, wu_ref], [wgb_ref, wub_ref])
    a = a_ref[...]
    g = _dot(a, wgb_ref[...])
    u = _dot(a, wub_ref[...])
    o_ref[...] = (g * _sigmoid(g) * u).astype(o_ref.dtype)


def _swiglu(a, w_gu, layer):
    r, k = a.shape
    f = w_gu.shape[2] // 2
    tm, tn = _tile(r, MATMUL_ROWS), min(512, _col_tile(f))
    nj = f // tn
    return pl.pallas_call(
        _swiglu_kernel,
        grid=(nj, r // tm),
        in_specs=[pl.BlockSpec((tm, k), lambda j, i: (i, 0)),
                  _weight_spec(w_gu, layer, tn, 0), _weight_spec(w_gu, layer, tn, nj)],
        out_specs=pl.BlockSpec((tm, tn), lambda j, i: (i, j)),
        out_shape=jax.ShapeDtypeStruct((r, f), BF16),
        scratch_shapes=[pltpu.VMEM((k, tn), BF16)] * 2,
        compiler_params=_params("parallel", "arbitrary"),
        name="ffn_up",
    )(a, w_gu, w_gu)


def _matmul_res_kernel(a_ref, w_ref, r_ref, o_ref, wb_ref):
    _cast_weights([w_ref], [wb_ref])
    o_ref[...] = r_ref[...] + _dot(a_ref[...], wb_ref[...])


def _matmul_res(a, w, layer, res, name, rows, tn_cap):
    r, k = a.shape
    n = w.shape[2]
    tm, tn = _tile(r, rows), min(tn_cap, _col_tile(n))
    return pl.pallas_call(
        _matmul_res_kernel,
        grid=(n // tn, r // tm),
        in_specs=[pl.BlockSpec((tm, k), lambda j, i: (i, 0)), _weight_spec(w, layer, tn, 0),
                  pl.BlockSpec((tm, tn), lambda j, i: (i, j))],
        out_specs=pl.BlockSpec((tm, tn), lambda j, i: (i, j)),
        out_shape=jax.ShapeDtypeStruct((r, n), F32),
        scratch_shapes=[pltpu.VMEM((k, tn), BF16)],
        compiler_params=_params("parallel", "arbitrary"),
        name=name,
    )(a, w, res)


def _matmul_res_joined_kernel(a_ref, w_ref, rh_ref, rt_ref, o_ref, wb_ref, *, n_head):
    _cast_weights([w_ref], [wb_ref])
    o_ref[...] = _rows_of(rh_ref, rt_ref, pl.program_id(1), n_head) + _dot(a_ref[...], wb_ref[...])


def _matmul_res_joined(a, w, layer, res_head, res_tail, name):
    r, k = a.shape
    n = w.shape[2]
    tm, tn = res_tail.shape[0], min(1024, _col_tile(n))
    n_head = res_head.shape[0] // tm
    assert res_head.shape[0] % tm == 0 and r == res_head.shape[0] + tm
    return pl.pallas_call(
        functools.partial(_matmul_res_joined_kernel, n_head=n_head),
        grid=(n // tn, r // tm),
        in_specs=[pl.BlockSpec((tm, k), lambda j, i: (i, 0)), _weight_spec(w, layer, tn, 0),
                  pl.BlockSpec((tm, tn), lambda j, i: (jnp.minimum(i, n_head - 1), j)),
                  pl.BlockSpec((tm, tn), lambda j, i: (0, j))],
        out_specs=pl.BlockSpec((tm, tn), lambda j, i: (i, j)),
        out_shape=jax.ShapeDtypeStruct((r, n), F32),
        scratch_shapes=[pltpu.VMEM((k, tn), BF16)],
        compiler_params=_params("parallel", "arbitrary"),
        name=name,
    )(a, w, res_head, res_tail)


def _matmul_res_split_kernel(a_ref, w_ref, r_ref, head_ref, tail_ref, wb_ref, *, n_head):
    _cast_weights([w_ref], [wb_ref])
    i = pl.program_id(1)
    val = r_ref[...] + _dot(a_ref[...], wb_ref[...])

    @pl.when(i < n_head)
    def _():
        head_ref[...] = val

    @pl.when(i == n_head)
    def _():
        tail_ref[...] = val


def _matmul_res_split(a, w, layer, res, name, head_rows):
    r, k = a.shape
    n = w.shape[2]
    tm, tn = r - head_rows, min(512, _col_tile(n))
    assert head_rows % tm == 0
    n_head = head_rows // tm
    return pl.pallas_call(
        functools.partial(_matmul_res_split_kernel, n_head=n_head),
        grid=(n // tn, r // tm),
        in_specs=[pl.BlockSpec((tm, k), lambda j, i: (i, 0)), _weight_spec(w, layer, tn, 0),
                  pl.BlockSpec((tm, tn), lambda j, i: (i, j))],
        out_specs=[pl.BlockSpec((tm, tn), lambda j, i: (jnp.minimum(i, n_head - 1), j)),
                   pl.BlockSpec((tm, tn), lambda j, i: (0, j))],
        out_shape=[jax.ShapeDtypeStruct((head_rows, n), F32), jax.ShapeDtypeStruct((tm, n), F32)],
        scratch_shapes=[pltpu.VMEM((k, tn), BF16)],
        compiler_params=_params("arbitrary", "arbitrary"),
        name=name,
    )(a, w, res)


def _merge_kernel(hm_ref, hd_ref, wpm_ref, wpd_ref, gm_ref, gd_ref, o_ref, wpmb_ref, wpdb_ref):
    _cast_weights([wpm_ref, wpd_ref], [wpmb_ref, wpdb_ref])
    zm = _dot(hm_ref[...].astype(BF16), wpmb_ref[...])
    zd = _dot(hd_ref[...].astype(BF16), wpdb_ref[...])
    gm = gm_ref[...].astype(F32)
    gd = gd_ref[...].astype(F32)
    o_ref[...] = (_sigmoid(gm) * zm + _sigmoid(gd) * zd).astype(o_ref.dtype)


def _merge(hm, hd, w_pm, w_pd, layer, gates, gate_col):
    r = hm.shape[0]
    n = w_pm.shape[2]
    tm = _tile(r, MATMUL_ROWS // 4)
    assert gate_col % LANES == 0

    def gate_spec(col0):
        return pl.BlockSpec((pl.Element(tm), pl.Element(n)), lambda j, i: (pl.multiple_of(i * tm, tm), col0))

    def resident(w):
        return pl.BlockSpec((None, w.shape[1], n), lambda j, i: (layer, 0, 0), pipeline_mode=pl.Buffered(1))

    return pl.pallas_call(
        _merge_kernel,
        grid=(1, r // tm),
        in_specs=[pl.BlockSpec((tm, M_WIDTH), lambda j, i: (i, 0)),
                  pl.BlockSpec((tm, D_V_WIDTH), lambda j, i: (i, 0)),
                  resident(w_pm), resident(w_pd), gate_spec(gate_col), gate_spec(gate_col + n)],
        out_specs=pl.BlockSpec((tm, n), lambda j, i: (i, 0)),
        out_shape=jax.ShapeDtypeStruct((r, n), BF16),
        scratch_shapes=[pltpu.VMEM((M_WIDTH, n), BF16), pltpu.VMEM((D_V_WIDTH, n), BF16)],
        compiler_params=_params("arbitrary", "arbitrary"),
        name="merge",
    )(hm, hd, w_pm, w_pd, gates, gates)


def _half_norm(x, gain):
    lo = lax.broadcasted_iota(jnp.int32, x.shape, 1) < D_QK_DIM
    x2 = x * x
    s_lo = jnp.sum(jnp.where(lo, x2, 0.0), axis=1, keepdims=True)
    s_hi = jnp.sum(jnp.where(lo, 0.0, x2), axis=1, keepdims=True)
    ms = jnp.where(lo, s_lo, s_hi) * (1.0 / D_QK_DIM)
    return x * lax.rsqrt(ms + EPS) * gain


def _qk_norm_kernel(q_ref, k_ref, v_ref, qg_ref, kg_ref, *refs, n_prompt_tiles):
    qo_ref, kb_ref, vt_ref, kp_ref, vp_ref, kt_ref, vt8_ref = refs[-7:]
    i = pl.program_id(0)
    kn = []
    for h in range(D_HEADS):
        sl = slice(h * LANES, (h + 1) * LANES)
        qo_ref[:, sl] = _half_norm(q_ref[:, sl], qg_ref[...]) * Q_SCALE
        kn.append(_half_norm(k_ref[:, sl], kg_ref[...]))
        kb_ref[:, sl] = kn[h].astype(BF16)
    for j in range(vt_ref.shape[0]):
        vt_ref[j] = v_ref[j * ROW_BLOCK:(j + 1) * ROW_BLOCK, :].T.astype(BF16)

    def scatter(k_dst, v_dst):
        tm = k_ref.shape[0]
        for h in range(D_HEADS):
            rows_h = pl.ds(h, tm, stride=D_HEADS)
            k_dst[rows_h, :] = kn[h]
            v_dst[rows_h, :] = v_ref[:, h * LANES:(h + 1) * LANES]

    @pl.when(i < n_prompt_tiles)
    def _():
        scatter(kp_ref.at[0, 0], vp_ref.at[0, 0])

    @pl.when(i == n_prompt_tiles)
    def _():
        scatter(kt_ref, vt8_ref)


def _qk_norm(qkv, q_gain, k_gain, layer, k_out, v_out):
    r = qkv.shape[0]
    w = D_QK_WIDTH
    n_prompt = _rows()[0]
    tm = r - n_prompt
    assert n_prompt % tm == 0 and SEQ % tm == 0 and tm % ROW_BLOCK == 0
    n_prompt_tiles, tiles_per_seq = n_prompt // tm, SEQ // tm
    spec = lambda c: pl.BlockSpec((tm, w), lambda i: (i, c))
    gspec = pl.BlockSpec((1, LANES), lambda i: (0, 0))

    def prompt_map(i):
        t = jnp.minimum(i, n_prompt_tiles - 1)
        row = pl.multiple_of((N_META + (t % tiles_per_seq) * tm) * D_HEADS, SUBLANES)
        return (layer, t // tiles_per_seq, row, 0)

    pspec = pl.BlockSpec(tuple(pl.Element(s) for s in (1, 1, tm * D_HEADS, LANES)), prompt_map)
    tspec = pl.BlockSpec((tm * D_HEADS, LANES), lambda i: (0, 0))
    out5 = jax.ShapeDtypeStruct((DEPTH, BATCH, (N_META + SEQ) * D_HEADS, LANES), F32)
    tail = jax.ShapeDtypeStruct((tm * D_HEADS, LANES), F32)
    carried = [] if k_out is None else [k_out, v_out]
    n_in = 5 + len(carried)
    return pl.pallas_call(
        functools.partial(_qk_norm_kernel, n_prompt_tiles=n_prompt_tiles),
        grid=(r // tm,),
        in_specs=[spec(0), spec(1), spec(2), gspec, gspec] + [pl.BlockSpec(memory_space=pl.ANY)] * len(carried),
        out_specs=[spec(0), spec(0), pl.BlockSpec((tm // ROW_BLOCK, D_V_WIDTH, ROW_BLOCK), lambda i: (i, 0, 0)),
                   pspec, pspec, tspec, tspec],
        out_shape=[jax.ShapeDtypeStruct((r, w), F32), jax.ShapeDtypeStruct((r, w), BF16),
                   jax.ShapeDtypeStruct((r // ROW_BLOCK, D_V_WIDTH, ROW_BLOCK), BF16), out5, out5, tail, tail],
        input_output_aliases={} if k_out is None else {n_in - 2: 3, n_in - 1: 4},
        compiler_params=_params("arbitrary"),
        name="qk_norm",
    )(qkv, qkv, qkv, jnp.tile(q_gain, 2).reshape(1, LANES), jnp.tile(k_gain, 2).reshape(1, LANES), *carried)


def _split3(x):
    hi = x.astype(BF16)
    r1 = x - hi.astype(F32)
    mid = r1.astype(BF16)
    lo = (r1 - mid.astype(F32)).astype(BF16)
    return hi, mid, lo


def _log_sigmoid(x):
    return jnp.minimum(x, 0.0) - jnp.log1p(jnp.exp(-jnp.abs(x)))


def _mlstm_gates(graw_col, graw_row, bias_col, bias_row, n_valid):
    l = graw_col.shape[0]
    g_col = graw_col + bias_col
    g_row = graw_row + bias_row
    is_f_col = lax.broadcasted_iota(jnp.int32, g_col.shape, 1) >= M_HEADS
    is_f_row = lax.broadcasted_iota(jnp.int32, g_row.shape, 0) >= M_HEADS
    ok_col = lax.broadcasted_iota(jnp.int32, g_col.shape, 0) < n_valid
    ok_row = lax.broadcasted_iota(jnp.int32, g_row.shape, 1) < n_valid
    lf_col = jnp.where(is_f_col & ok_col, _log_sigmoid(g_col), 0.0)
    lf_row = jnp.where(is_f_row & ok_row, _log_sigmoid(g_row), 0.0)
    ig_col = jnp.where(ok_col, g_col, NEG_INF)
    ig_row = jnp.where(ok_row, g_row, NEG_INF)
    rr = lax.broadcasted_iota(jnp.int32, (l, l), 0)
    cc = lax.broadcasted_iota(jnp.int32, (l, l), 1)
    tri_l = (cc <= rr).astype(BF16)
    tri_u = (rr <= cc).astype(BF16)
    b_col = sum(_dot(tri_l, p) for p in _split3(lf_col))
    b_row = sum(_dot(p, tri_u) for p in _split3(lf_row))
    return ig_col, b_col, ig_row, b_row, cc <= rr


def _mlstm_chunk(q, k, v, ig_col, b_col, ig_row, b_row, causal, c_st, n_st, m_st):
    l = q.shape[0]
    d = b_col - b_row + ig_row
    d = jnp.where(causal, d, NEG_INF)
    inter = b_col + m_st
    m_t = jnp.maximum(inter, jnp.max(d, axis=1, keepdims=True))
    w = jnp.exp(d - m_t)
    s = _dot_nt(q, k) * w
    a = jnp.exp(inter - m_t)
    num = _dot(s.astype(BF16), v) + a * _dot_nt(q, c_st.astype(BF16))
    den = jnp.sum(s, axis=1, keepdims=True) + a * jnp.sum(q.astype(F32) * n_st, axis=1, keepdims=True)
    h = num * (1.0 / jnp.maximum(jnp.abs(den), jnp.exp(-m_t)))

    b_last = b_col[l - 1:l, :]
    g_col = b_last - b_col + ig_col
    g_row = b_last - b_row + ig_row
    m_new = jnp.maximum(b_last + m_st, jnp.max(g_row, axis=1, keepdims=True))
    ws_col = jnp.exp(g_col - m_new)
    ws_row = jnp.exp(g_row - m_new)
    decay = jnp.exp(b_last + m_st - m_new)
    wv = (v.astype(F32) * ws_col).astype(BF16)
    c_new = decay * c_st + _dot_tn(wv, k)
    n_new = decay * n_st + _dot(ws_row.astype(BF16), k)
    return h, c_new, n_new, m_new


def _mlstm_heads(q_of, k_of, v_of, om_of, gates, gain_ref, c_ref, n_ref, m_ref, h_out):
    ig_col, b_col, ig_row, b_row, causal = gates
    for h in range(M_HEADS):
        f = M_HEADS + h
        hs = slice(h * M_HEAD_DIM, (h + 1) * M_HEAD_DIM)
        out, c_new, n_new, m_new = _mlstm_chunk(
            q_of(hs), k_of(hs), v_of(hs),
            ig_col[:, h:h + 1], b_col[:, f:f + 1], ig_row[h:h + 1, :], b_row[f:f + 1, :], causal,
            c_ref[0, h], n_ref[0, h:h + 1, :], m_ref[0, h:h + 1, 0:1])
        c_ref[0, h] = c_new
        n_ref[0, h:h + 1, :] = n_new
        m_ref[0, h:h + 1, :] = jnp.broadcast_to(m_new, (1, LANES))
        h_out(hs, _rms(out, gain_ref[:, hs]) * _sigmoid(om_of(hs)))


def _mlstm_prompt_kernel(q_ref, k_ref, v_ref, om_ref, gc_ref, gr_ref, bc_ref, br_ref, gain_ref,
                         h_ref, c_ref, n_ref, m_ref):
    c = pl.program_id(1)

    @pl.when(c == 0)
    def _():
        c_ref[...] = jnp.zeros_like(c_ref)
        n_ref[...] = jnp.zeros_like(n_ref)
        m_ref[...] = jnp.zeros_like(m_ref)

    n_valid = jnp.where(c == 0, N_META, ROW_BLOCK)
    gates = _mlstm_gates(gc_ref[...], gr_ref[...], bc_ref[...], br_ref[...], n_valid)

    def h_out(hs, val):
        h_ref[:, hs] = val

    _mlstm_heads(lambda hs: q_ref[:, hs].astype(BF16), lambda hs: (k_ref[:, hs] * K_SCALE).astype(BF16),
                 lambda hs: v_ref[:, hs].astype(BF16), lambda hs: om_ref[:, hs],
                 gates, gain_ref, c_ref, n_ref, m_ref, h_out)


def _mlstm_prompt(qkv, gates, g_col, g_row, b_if, gain):
    r = qkv.shape[0]
    n_prompt, _, meta_row, _ = _rows()
    nc = SEQ // ROW_BLOCK
    meta_blk = meta_row // ROW_BLOCK

    def rb(b, c):
        return jnp.where(c == 0, meta_blk, b * nc + c - 1)

    def rb_out(b, c):
        return jnp.where((c == 0) & (b > 0), b * nc, rb(b, c))

    spec = lambda col: pl.BlockSpec((ROW_BLOCK, M_WIDTH), lambda b, c: (rb(b, c), col))
    full = lambda shape: pl.BlockSpec(shape, lambda b, c: (0,) * len(shape))
    return pl.pallas_call(
        _mlstm_prompt_kernel,
        grid=(BATCH, nc + 1),
        in_specs=[spec(0), spec(1), spec(2), spec(3),
                  pl.BlockSpec((ROW_BLOCK, 2 * M_HEADS), lambda b, c: (rb(b, c), 0)),
                  pl.BlockSpec((2 * M_HEADS, ROW_BLOCK), lambda b, c: (0, rb(b, c))),
                  full((1, 2 * M_HEADS)), full((2 * M_HEADS, 1)), full((1, M_WIDTH))],
        out_specs=[pl.BlockSpec((ROW_BLOCK, M_WIDTH), lambda b, c: (rb_out(b, c), 0)),
                   pl.BlockSpec((1, M_HEADS, M_HEAD_DIM, M_HEAD_DIM), lambda b, c: (b, 0, 0, 0)),
                   pl.BlockSpec((1, M_HEADS, M_HEAD_DIM), lambda b, c: (b, 0, 0)),
                   pl.BlockSpec((1, M_HEADS, LANES), lambda b, c: (b, 0, 0))],
        out_shape=[jax.ShapeDtypeStruct((r, M_WIDTH), F32),
                   jax.ShapeDtypeStruct((BATCH, M_HEADS, M_HEAD_DIM, M_HEAD_DIM), F32),
                   jax.ShapeDtypeStruct((BATCH, M_HEADS, M_HEAD_DIM), F32),
                   jax.ShapeDtypeStruct((BATCH, M_HEADS, LANES), F32)],
        compiler_params=_params("arbitrary", "arbitrary"),
        name="mlstm_prompt",
    )(qkv, qkv, qkv, gates, g_col, g_row, b_if.reshape(1, -1), b_if.reshape(-1, 1), gain.reshape(1, -1))


def _mlstm_sample_kernel(q_ref, k_ref, v_ref, om_ref, gc_ref, gr_ref, bc_ref, br_ref, gain_ref,
                         c_in, n_in, m_in, *rest):
    h_ref, c_ref, n_ref, m_ref, qp, kp, vp = rest[-7:]
    c_ref[...] = c_in[...]
    n_ref[...] = n_in[...]
    m_ref[...] = m_in[...]
    for src, dst in ((q_ref, qp), (k_ref, kp), (v_ref, vp)):
        dst[...] = jnp.zeros_like(dst)
        dst[0:DEC_SEQ, :] = src[...]
    gates = _mlstm_gates(gc_ref[0], gr_ref[0], bc_ref[...], br_ref[...], DEC_SEQ)
    om = om_ref[...]

    def h_out(hs, val):
        h_ref[:, hs] = val[0:DEC_SEQ, :]

    def om_of(hs):
        return jnp.concatenate([om[:, hs], jnp.zeros((SAMPLE_CHUNK - DEC_SEQ, M_HEAD_DIM), F32)], axis=0)

    _mlstm_heads(lambda hs: qp[:, hs].astype(BF16), lambda hs: (kp[:, hs] * K_SCALE).astype(BF16),
                 lambda hs: vp[:, hs].astype(BF16), om_of,
                 gates, gain_ref, c_ref, n_ref, m_ref, h_out)


def _mlstm_sample(qkv, gates, g_col, g_row, b_if, gain, layer, state_c, state_n, st_m, hm, c_out, n_out):
    assert DEC_SEQ == SUBLANES
    n_prompt = _rows()[0]
    blk0 = n_prompt // DEC_SEQ
    spec = lambda col: pl.BlockSpec((DEC_SEQ, M_WIDTH), lambda b: (blk0 + b, col))
    full = lambda shape: pl.BlockSpec(shape, lambda b: (0,) * len(shape))
    st_spec = lambda shape: pl.BlockSpec((1,) + shape, lambda b: (b,) + (0,) * len(shape))
    layer_spec = lambda shape: pl.BlockSpec((None, 1) + shape, lambda b: (layer, b) + (0,) * len(shape))
    st_specs = [layer_spec((M_HEADS, M_HEAD_DIM, M_HEAD_DIM)), layer_spec((M_HEADS, M_HEAD_DIM)),
                st_spec((M_HEADS, LANES))]
    carried = [] if c_out is None else [c_out, n_out]
    aliases = {12: 0}
    if carried:
        aliases.update({13: 1, 14: 2})
    return pl.pallas_call(
        _mlstm_sample_kernel,
        grid=(DEC_BATCH,),
        in_specs=[spec(0), spec(1), spec(2), spec(3),
                  st_spec((SAMPLE_CHUNK, 2 * M_HEADS)), st_spec((2 * M_HEADS, SAMPLE_CHUNK)),
                  full((1, 2 * M_HEADS)), full((2 * M_HEADS, 1)), full((1, M_WIDTH))] + st_specs
                 + [pl.BlockSpec(memory_space=pl.ANY)] * (1 + len(carried)),
        out_specs=[spec(0)] + st_specs,
        out_shape=[jax.ShapeDtypeStruct(hm.shape, F32),
                   jax.ShapeDtypeStruct(state_c.shape, F32), jax.ShapeDtypeStruct(state_n.shape, F32),
                   jax.ShapeDtypeStruct((DEC_BATCH, M_HEADS, LANES), F32)],
        scratch_shapes=[pltpu.VMEM((SAMPLE_CHUNK, M_WIDTH), F32)] * 3,
        input_output_aliases=aliases,
        compiler_params=_params("arbitrary"),
        name="mlstm_sample",
    )(qkv, qkv, qkv, gates, g_col, g_row, b_if.reshape(1, -1), b_if.reshape(-1, 1), gain.reshape(1, -1),
      state_c, state_n, jnp.broadcast_to(st_m[:, :, None], (DEC_BATCH, M_HEADS, LANES)), hm, *carried)


def _lambda(lam_ref, lam_init):
    p = lam_ref[...]
    s1 = jnp.sum(p[0:1, :] * p[1:2, :], axis=1, keepdims=True)
    s2 = jnp.sum(p[2:3, :] * p[3:4, :], axis=1, keepdims=True)
    return jnp.exp(s1) - jnp.exp(s2) + lam_init


def _diff_prompt_kernel(slope_ref, q_ref, k_ref, vt_ref, km_ref, vtm_ref, lam_ref, gain_ref, o_ref,
                        base_ref, m_ref, acc_ref, *, lam_init):
    b = pl.program_id(0)
    hg = pl.program_id(1)
    is_meta = pl.program_id(2) == 0
    t = pl.program_id(2) - 1
    tq = ROW_BLOCK
    heads = range(HEAD_GROUP)
    cols = [slice(i * LANES, (i + 1) * LANES) for i in heads]
    slopes = [slope_ref[hg * HEAD_GROUP + i] * LOG2E for i in heads]

    @pl.when(jnp.logical_not(is_meta & (b > 0)))
    def _():
        lo = lax.broadcasted_iota(jnp.int32, (tq, LANES), 1) < D_QK_DIM
        q2 = []
        for i in heads:
            q = q_ref[:, cols[i]].astype(BF16)
            zero = jnp.zeros_like(q)
            q2.append(jnp.concatenate([jnp.where(lo, q, zero), jnp.where(lo, zero, q)], axis=0))
            base_ref[i] = slopes[i] * lax.broadcasted_iota(jnp.int32, (tq, 2 * tq), 0).astype(F32)
        q_pos0 = jnp.where(is_meta, 0, N_META + t * tq)
        ones = jnp.ones((ONES_ROWS, tq), BF16)
        m_ref[...] = jnp.full_like(m_ref, NEG_INF)
        acc_ref[...] = jnp.zeros_like(acc_ref)

        def visible(n_keys, k_off):
            key = lax.broadcasted_iota(jnp.int32, (n_keys, 2 * tq), 0) + k_off
            qry = lax.broadcasted_iota(jnp.int32, (n_keys, 2 * tq), 1)
            return key <= jnp.where(qry >= tq, qry - tq, qry)

        def step(i, parts):
            cs = [slopes[i] * k_off.astype(F32) for _, _, k_off in parts]
            m_old = m_ref[i]
            m_new = m_old
            for (s, _, _), c in zip(parts, cs):
                m_new = jnp.maximum(m_new, jnp.max(s, axis=0, keepdims=True) + c)
            acc = jnp.exp2(m_old - m_new) * acc_ref[i]
            for (s, vt, _), c in zip(parts, cs):
                p = jnp.exp2(s - (m_new - c)).astype(BF16)
                acc = acc + _dot(jnp.concatenate([vt, ones[:, 0:vt.shape[1]]], axis=0), p)
            acc_ref[i] = acc
            m_ref[i] = m_new

        def scores(kb, i):
            ks = pl.ds(pl.multiple_of(kb * tq, tq), tq)
            return _dot_nt(k_ref[ks, cols[i]], q2[i])

        def full_blocks(kbs):
            for i in heads:
                step(i, [(scores(kb, i) + base_ref[i], vt_ref[kb, cols[i], :], N_META + kb * tq - q_pos0)
                         for kb in kbs])

        n_full = jnp.where(is_meta, 0, t)
        done = 0
        for width in KV_UNROLL:
            n_iter = (n_full - done) // width

            def body(j, carry, width=width, done=done):
                full_blocks([done + j * width + u for u in range(width)])
                return carry

            lax.fori_loop(0, n_iter, body, 0)
            done = done + n_iter * width

        td = jnp.maximum(t, 0)
        hide = jnp.where(is_meta, 2 * tq, 0)
        for i in heads:
            s_meta = _dot_nt(km_ref[:, cols[i]], q2[i]) + base_ref[i, 0:N_META, :]
            s_diag = scores(td, i) + base_ref[i]
            step(i, [(jnp.where(visible(N_META, -q_pos0), s_meta, NEG_INF), vtm_ref[0, cols[i], 0:N_META], -q_pos0),
                     (jnp.where(visible(tq, hide), s_diag, NEG_INF), vt_ref[td, cols[i], :], jnp.int32(0))])

        lam = _lambda(lam_ref, lam_init)
        for i in heads:
            acc = acc_ref[i]
            o_t = acc[0:D_V_DIM, :] * (1.0 / acc[D_V_DIM:D_V_DIM + 1, :])
            o = (o_t[:, 0:tq] - lam * o_t[:, tq:2 * tq]).T
            o_ref[:, cols[i]] = _rms(o, gain_ref[:, cols[i]]) * (1.0 - lam_init)


def _diff_prompt(qn, kb, vt, lam_pack, gain, slopes, lam_init):
    r = qn.shape[0]
    meta_row = _rows()[2]
    nq = SEQ // ROW_BLOCK
    meta_blk = meta_row // ROW_BLOCK
    gw = HEAD_GROUP * LANES
    assert meta_row % N_META == 0 and D_V_DIM == LANES and D_HEADS % HEAD_GROUP == 0

    def qmap(b, h, qi, *_):
        return (jnp.where(qi == 0, jnp.where(b == 0, meta_blk, b * nq), b * nq + qi - 1), h)

    grid_spec = pltpu.PrefetchScalarGridSpec(
        num_scalar_prefetch=1,
        grid=(BATCH, D_HEADS // HEAD_GROUP, nq + 1),
        in_specs=[pl.BlockSpec((ROW_BLOCK, gw), qmap),
                  pl.BlockSpec((SEQ, gw), lambda b, h, qi, *_: (b, h)),
                  pl.BlockSpec((nq, gw, ROW_BLOCK), lambda b, h, qi, *_: (b, h, 0)),
                  pl.BlockSpec((N_META, gw), lambda b, h, qi, *_: (meta_row // N_META, h)),
                  pl.BlockSpec((1, gw, ROW_BLOCK), lambda b, h, qi, *_: (meta_blk, h, 0)),
                  pl.BlockSpec((SUBLANES, LANES), lambda b, h, qi, *_: (0, 0)),
                  pl.BlockSpec((1, gw), lambda b, h, qi, *_: (0, h))],
        out_specs=pl.BlockSpec((ROW_BLOCK, gw), qmap),
        scratch_shapes=[pltpu.VMEM((HEAD_GROUP, ROW_BLOCK, 2 * ROW_BLOCK), F32),
                        pltpu.VMEM((HEAD_GROUP, 1, 2 * ROW_BLOCK), F32),
                        pltpu.VMEM((HEAD_GROUP, D_V_DIM + ONES_ROWS, 2 * ROW_BLOCK), F32)])
    return pl.pallas_call(
        functools.partial(_diff_prompt_kernel, lam_init=lam_init),
        grid_spec=grid_spec,
        out_shape=jax.ShapeDtypeStruct((r, D_V_WIDTH), F32),
        compiler_params=_params("arbitrary", "arbitrary", "arbitrary"),
        name="diff_prompt",
    )(slopes, qn, kb, vt, kb, vt, lam_pack, gain.reshape(1, -1))


def _diff_sample_kernel(pt_ref, *refs, lam_init):
    del pt_ref
    pp = PAGES_PER_STEP
    k_pages, v_pages = refs[0:pp], refs[pp:2 * pp]
    (q_ref, kn_ref, vn_ref, lam_ref, gain_ref, hd_in, o_ref,
     qb_ref, kp_ref, vp_ref, m_ref, acc_ref) = refs[2 * pp:]
    del hd_in
    g = pl.program_id(1)
    qrows = 2 * DEC_SEQ
    half = D_HEADS * DEC_SEQ
    slopes = [2.0 ** (-8.0 * (h + 1) / D_HEADS) * LOG2E for h in range(D_HEADS)]

    def step(h, s, v):
        m_old = m_ref[h]
        m_new = jnp.maximum(m_old, jnp.max(s, axis=1, keepdims=True))
        p = jnp.exp2(s - m_new).astype(BF16)
        v_ones = jnp.concatenate([v, jnp.ones_like(v)], axis=1)
        acc_ref[h] = jnp.exp2(m_old - m_new) * acc_ref[h] + _dot(p, v_ones)
        m_ref[h] = m_new

    @pl.when(g == 0)
    def _():
        lo = lax.broadcasted_iota(jnp.int32, (DEC_SEQ, LANES), 1) < D_QK_DIM
        for h in range(D_HEADS):
            x = q_ref[:, h * LANES:(h + 1) * LANES]
            qb_ref[h] = jnp.concatenate([jnp.where(lo, x, 0.0), jnp.where(lo, 0.0, x)], axis=0).astype(BF16)
        m_ref[...] = jnp.full_like(m_ref, NEG_INF)
        acc_ref[...] = jnp.zeros_like(acc_ref)

    k_pos = (lax.broadcasted_iota(jnp.int32, (1, pp * PAGE_SIZE), 1) + (g * (pp * PAGE_SIZE) - PAST_LEN)).astype(F32)
    for h in range(D_HEADS):
        rows_h = pl.ds(h, PAGE_SIZE, stride=D_HEADS)
        kh = jnp.concatenate([k_pages[p][rows_h, :] for p in range(pp)], axis=0).astype(BF16)
        vh = jnp.concatenate([v_pages[p][rows_h, :] for p in range(pp)], axis=0).astype(BF16)
        step(h, _dot_nt(qb_ref[h], kh) + slopes[h] * k_pos, vh)

    @pl.when(g == pl.num_programs(1) - 1)
    def _():
        for src, dst in ((kn_ref, kp_ref), (vn_ref, vp_ref)):
            dst[...] = jnp.zeros_like(dst)
            dst[0:half, :] = src[...]
        kn = kp_ref[...].astype(BF16)
        vn = vp_ref[...].astype(BF16)
        c = lax.broadcasted_iota(jnp.int32, (qrows, LANES), 1)
        r_tok = lax.broadcasted_iota(jnp.int32, (qrows, LANES), 0) % DEC_SEQ
        c_tok = c // D_HEADS
        lam = _lambda(lam_ref, lam_init)
        for h in range(D_HEADS):
            ok = (c % D_HEADS == h) & (c_tok <= r_tok)
            s = _dot_nt(qb_ref[h], kn) + slopes[h] * c_tok.astype(F32)
            step(h, jnp.where(ok, s, NEG_INF), vn)
            acc = acc_ref[h]
            o = acc[:, 0:LANES] * (1.0 / acc[:, LANES:2 * LANES])
            oh = o[0:DEC_SEQ, :] - lam * o[DEC_SEQ:qrows, :]
            cs = slice(h * LANES, (h + 1) * LANES)
            o_ref[:, cs] = _rms(oh, gain_ref[:, cs]) * (1.0 - lam_init)


def _diff_sample(layer, cache_k, cache_v, page_table, qn, kf, vf, lam_pack, gain, hd, lam_init):
    assert DEC_SEQ == SUBLANES and D_HEADS == SUBLANES and D_V_DIM == LANES
    assert D_HEADS * DEC_SEQ <= LANES and PAST_LEN % (PAGE_SIZE * PAGES_PER_STEP) == 0
    n_prompt = _rows()[0]
    blk0 = n_prompt // DEC_SEQ
    pp = PAGES_PER_STEP
    width = D_HEADS * LANES
    rows = 2 * D_HEADS * DEC_SEQ

    def as_rows(cache):
        return cache.reshape(cache.shape[0], cache.shape[1], PAGE_SIZE * D_HEADS, LANES)

    def page_spec(p):
        return pl.BlockSpec((None, None, PAGE_SIZE * D_HEADS, LANES),
                            lambda b, g, pt: (layer, pt[b, g * pp + p], 0, 0))

    row_spec = pl.BlockSpec((DEC_SEQ, width), lambda b, g, pt: (blk0 + b, 0))
    new_spec = pl.BlockSpec((DEC_SEQ * D_HEADS, LANES), lambda b, g, pt: (b, 0))
    grid_spec = pltpu.PrefetchScalarGridSpec(
        num_scalar_prefetch=1,
        grid=(DEC_BATCH, PAST_LEN // (PAGE_SIZE * pp)),
        in_specs=[page_spec(p) for p in range(pp)] * 2
                 + [row_spec, new_spec, new_spec,
                    pl.BlockSpec((SUBLANES, LANES), lambda b, g, pt: (0, 0)),
                    pl.BlockSpec((1, width), lambda b, g, pt: (0, 0)),
                    pl.BlockSpec(memory_space=pl.ANY)],
        out_specs=row_spec,
        scratch_shapes=[pltpu.VMEM((D_HEADS, 2 * DEC_SEQ, LANES), BF16),
                        pltpu.VMEM((LANES, LANES), F32), pltpu.VMEM((LANES, LANES), F32),
                        pltpu.VMEM((D_HEADS, 2 * DEC_SEQ, 1), F32), pltpu.VMEM((D_HEADS, 2 * DEC_SEQ, 2 * LANES), F32)])
    n_in = 2 * pp + 6
    return pl.pallas_call(
        functools.partial(_diff_sample_kernel, lam_init=lam_init),
        grid_spec=grid_spec,
        out_shape=jax.ShapeDtypeStruct(hd.shape, F32),
        input_output_aliases={n_in: 0},
        compiler_params=_params("arbitrary", "arbitrary"),
        name="diff_sample",
    )(page_table, *([as_rows(cache_k)] * pp), *([as_rows(cache_v)] * pp), qn, kf, vf, lam_pack,
      gain.reshape(1, -1), hd)


def kernel(x_prompt, x_sample, cache_k, cache_v, state_C, state_n, state_m, page_table, meta_tokens, norm_mix, w_in, b_if, q_gain, k_gain, lambda_q1, lambda_k1, lambda_q2, lambda_k2, mlstm_norm, diff_norm, w_pm, w_pd, w_out, norm_ffn, w_gu, w_down):
    assert M_HEAD_DIM ** -0.5 == 2.0 ** round(math.log2(M_HEAD_DIM ** -0.5))
    n_prompt, n_sample, meta_row, r = _rows()
    x_head = x_prompt.reshape(n_prompt, D_MODEL)
    x_tail = jnp.concatenate([x_sample.reshape(n_sample, D_MODEL), meta_tokens.astype(F32),
                              jnp.zeros((r - meta_row - N_META, D_MODEL), F32)], axis=0)
    x = None
    slopes = 2.0 ** (-8.0 * jnp.arange(1, D_HEADS + 1, dtype=F32) / D_HEADS)
    w_in_t = jnp.swapaxes(w_in, 1, 2)
    g0 = 4 * M_WIDTH
    g1 = g0 + 2 * M_HEADS

    k_prompt = v_prompt = c_sample = n_sample_st = None
    meta_kv = []
    outs = [[] for _ in range(6)]
    for l in range(DEPTH):
        lam_init = 0.8 - 0.6 * math.exp(-0.3 * l)
        if l == 0:
            xn = _rmsnorm_split(x_head, x_tail, norm_mix[l], "norm_mix_first")
        else:
            xn = _rmsnorm(x, norm_mix[l], "norm_mix")
        proj_m = _matmul_nt(xn, w_in_t, l, 0, g0, "proj_mlstm")
        graw = _matmul_nt(xn, w_in_t, l, g0, 2 * M_HEADS, "proj_gates")
        g2 = g1 + 2 * D_QK_WIDTH + D_V_WIDTH
        proj_d = _matmul_nt(xn, w_in_t, l, g1, g2 - g1, "proj_diff")
        proj_g = _matmul_nt(xn, w_in_t, l, g2, w_in_t.shape[1] - g2, "proj_merge_gates", BF16)

        hm, c_p, n_p, m_p = _mlstm_prompt(proj_m, proj_m, graw, graw.T, b_if[l], mlstm_norm[l])
        gs = graw[n_prompt:n_prompt + n_sample].reshape(DEC_BATCH, DEC_SEQ, 2 * M_HEADS)
        gs = jnp.pad(gs, ((0, 0), (0, SAMPLE_CHUNK - DEC_SEQ), (0, 0)))
        hm, c_sample, n_sample_st, m_s = _mlstm_sample(
            proj_m, proj_m, gs, gs.transpose(0, 2, 1), b_if[l], mlstm_norm[l], l, state_C.astype(F32),
            state_n.astype(F32), state_m[l].astype(F32), hm, c_sample, n_sample_st)

        qn, kb, vt, k_prompt, v_prompt, kf, vf = _qk_norm(proj_d, q_gain[l], k_gain[l], l, k_prompt, v_prompt)
        lam_pack = jnp.pad(jnp.stack([lambda_q1[l], lambda_k1[l], lambda_q2[l], lambda_k2[l]]).astype(F32),
                           ((0, SUBLANES - 4), (0, LANES - D_QK_DIM)))
        hd = _diff_prompt(qn, kb, vt, lam_pack, diff_norm[l], slopes, lam_init)
        hd = _diff_sample(l, cache_k, cache_v, page_table, qn, kf, vf, lam_pack, diff_norm[l], hd, lam_init)

        z = _merge(hm, hd, w_pm, w_pd, l, proj_g, 0)
        if l == 0:
            x = _matmul_res_joined(z, w_out, l, x_head, x_tail, "out_proj_first")
        else:
            x = _matmul_res(z, w_out, l, x, "out_proj", MATMUL_ROWS, 1024)
        act = _swiglu(_rmsnorm(x, norm_ffn[l], "norm_ffn"), w_gu, l)
        if l < DEPTH - 1:
            x = _matmul_res(act, w_down, l, x, "ffn_down", MATMUL_ROWS // 2, 512)
        else:
            y_head, y_tail = _matmul_res_split(act, w_down, l, x, "ffn_down_last", n_prompt)

        head_shape = (D_HEADS, D_V_DIM)
        s_rows, m_rows = n_sample * D_HEADS, N_META * D_HEADS
        meta_kv.append((kf[s_rows:s_rows + m_rows], vf[s_rows:s_rows + m_rows]))
        vals = (c_p, n_p, m_p[:, :, 0],
                kf[:s_rows].reshape(DEC_BATCH, DEC_SEQ, *head_shape),
                vf[:s_rows].reshape(DEC_BATCH, DEC_SEQ, *head_shape), m_s[:, :, 0])
        for o, v in zip(outs, vals):
            o.append(v)

    for l, (km, vm) in enumerate(meta_kv):
        k_prompt = k_prompt.at[l, :, :km.shape[0]].set(jnp.broadcast_to(km, (BATCH,) + km.shape))
        v_prompt = v_prompt.at[l, :, :vm.shape[0]].set(jnp.broadcast_to(vm, (BATCH,) + vm.shape))
    kv_shape = (DEPTH, BATCH, N_META + SEQ, D_HEADS, D_V_DIM)
    y_prompt = y_head.reshape(BATCH, SEQ, D_MODEL)
    y_sample = y_tail[:n_sample].reshape(DEC_BATCH, DEC_SEQ, D_MODEL)
    c_p, n_p, m_p, k_s, v_s, m_s = (jnp.stack(o) for o in outs)
    return (y_prompt, y_sample, k_prompt.reshape(kv_shape), v_prompt.reshape(kv_shape), c_p, n_p, m_p, k_s, v_s,
            c_sample, n_sample_st, m_s)
```

```python
import functools
import math

import jax
import jax.numpy as jnp
from jax import lax
from jax.experimental import pallas as pl
from jax.experimental.pallas import tpu as pltpu

D_MODEL = 2048
BATCH = 2
SEQ = 4096
DEPTH = 2
DEC_BATCH = 32
DEC_SEQ = 8
PAST_LEN = 8192
PAGE_SIZE = 128
N_META = 16
M_HEADS = 4
M_HEAD_DIM = 256
M_WIDTH = M_HEADS * M_HEAD_DIM
D_HEADS = 8
D_QK_DIM = 64
D_V_DIM = 2 * D_QK_DIM
D_QK_WIDTH = D_HEADS * 2 * D_QK_DIM
D_V_WIDTH = D_HEADS * D_V_DIM
EPS = 1e-6

LANES = 128
SUBLANES = 8
ROW_BLOCK = 256
MATMUL_ROWS = 1100
K_SCALE = M_HEAD_DIM ** -0.5
SAMPLE_CHUNK = 128
KV_UNROLL = (2, 1)
HEAD_GROUP = 8
PAGES_PER_STEP = 16
VMEM_LIMIT = 56 * 1024 * 1024

F32 = jnp.float32
BF16 = jnp.bfloat16
NEG_INF = float("-inf")
LOG2E = math.log2(math.e)
Q_SCALE = D_QK_DIM ** -0.5 * LOG2E
ONES_ROWS = 16


def _ffn_dim():
    return ((8 * D_MODEL // 3 + 255) // 256) * 256


def _rows():
    n_prompt = BATCH * SEQ
    n_sample = DEC_BATCH * DEC_SEQ
    assert n_prompt % ROW_BLOCK == 0 and n_sample % ROW_BLOCK == 0 and N_META <= ROW_BLOCK
    return n_prompt, n_sample, n_prompt + n_sample, n_prompt + n_sample + ROW_BLOCK


def _tile(n, cap, mult=16):
    for t in range(min(cap, n) // mult * mult, 0, -mult):
        if n % t == 0:
            return t
    raise ValueError((n, cap, mult))


def _col_tile(n):
    for t in (1024, 512, 256, 128):
        if n % t == 0:
            return t
    raise ValueError(n)


def _params(*sem):
    return pltpu.CompilerParams(dimension_semantics=sem, vmem_limit_bytes=VMEM_LIMIT)


def _dot(a, b):
    return jnp.dot(a, b, preferred_element_type=F32)


def _dot_nt(a, b):
    return lax.dot_general(a, b, (((1,), (1,)), ((), ())), preferred_element_type=F32)


def _dot_tn(a, b):
    return lax.dot_general(a, b, (((0,), (0,)), ((), ())), preferred_element_type=F32)


def _sigmoid(x):
    return 1.0 / (1.0 + jnp.exp(-x))


def _rms(x, gain):
    ms = jnp.mean(x * x, axis=-1, keepdims=True)
    return x * lax.rsqrt(ms + EPS) * gain


def _rmsnorm_kernel(x_ref, g_ref, o_ref):
    o_ref[...] = _rms(x_ref[...], g_ref[...]).astype(o_ref.dtype)


def _rmsnorm(x, gain, name):
    r, d = x.shape
    tm = _tile(r, MATMUL_ROWS)
    return pl.pallas_call(
        _rmsnorm_kernel,
        grid=(r // tm,),
        in_specs=[pl.BlockSpec((tm, d), lambda i: (i, 0)), pl.BlockSpec((1, d), lambda i: (0, 0))],
        out_specs=pl.BlockSpec((tm, d), lambda i: (i, 0)),
        out_shape=jax.ShapeDtypeStruct((r, d), BF16),
        compiler_params=_params("parallel"),
        name=name,
    )(x, gain.reshape(1, d))


def _rows_of(head_ref, tail_ref, i, n_head):
    return jnp.where(i < n_head, head_ref[...], tail_ref[...])


def _rmsnorm_split_kernel(head_ref, tail_ref, g_ref, o_ref, *, n_head):
    x = _rows_of(head_ref, tail_ref, pl.program_id(0), n_head)
    o_ref[...] = _rms(x, g_ref[...]).astype(o_ref.dtype)


def _rmsnorm_split(head, tail, gain, name):
    tm, d = tail.shape
    n_head = head.shape[0] // tm
    assert head.shape[0] % tm == 0
    return pl.pallas_call(
        functools.partial(_rmsnorm_split_kernel, n_head=n_head),
        grid=(n_head + 1,),
        in_specs=[pl.BlockSpec((tm, d), lambda i: (jnp.minimum(i, n_head - 1), 0)),
                  pl.BlockSpec((tm, d), lambda i: (0, 0)), pl.BlockSpec((1, d), lambda i: (0, 0))],
        out_specs=pl.BlockSpec((tm, d), lambda i: (i, 0)),
        out_shape=jax.ShapeDtypeStruct((head.shape[0] + tm, d), BF16),
        compiler_params=_params("parallel"),
        name=name,
    )(head, tail, gain.reshape(1, d))


def _weight_spec(w, layer, tn, col0):
    if w.ndim == 3:
        return pl.BlockSpec((None, w.shape[1], tn), lambda j, i: (layer, 0, col0 + j))
    return pl.BlockSpec((w.shape[0], tn), lambda j, i: (0, col0 + j))


def _cast_weights(w_refs, wb_refs):
    @pl.when(pl.program_id(1) == 0)
    def _():
        for w_ref, wb_ref in zip(w_refs, wb_refs):
            wb_ref[...] = w_ref[...].astype(BF16)


def _matmul_nt_kernel(a_ref, w_ref, o_ref, wb_ref):
    _cast_weights([w_ref.at[0]], [wb_ref])
    o_ref[...] = _dot_nt(a_ref[...], wb_ref[...]).astype(o_ref.dtype)


def _matmul_nt(a, w_t, layer, row0, n, name, out_dtype=F32):
    r, k = a.shape
    tm, tn = _tile(r, MATMUL_ROWS), _tile(n, 1024, SUBLANES)
    assert row0 % SUBLANES == 0
    return pl.pallas_call(
        _matmul_nt_kernel,
        grid=(n // tn, r // tm),
        in_specs=[pl.BlockSpec((tm, k), lambda j, i: (i, 0)),
                  pl.BlockSpec((pl.Element(1), pl.Element(tn), pl.Element(k)),
                               lambda j, i: (layer, pl.multiple_of(row0 + j * tn, SUBLANES), 0))],
        out_specs=pl.BlockSpec((tm, tn), lambda j, i: (i, j)),
        out_shape=jax.ShapeDtypeStruct((r, n), out_dtype),
        scratch_shapes=[pltpu.VMEM((tn, k), BF16)],
        compiler_params=_params("parallel", "arbitrary"),
        name=name,
    )(a, w_t)


def _swiglu_kernel(a_ref, wg_ref, wu_ref, o_ref, wgb_ref, wub_ref):
    _cast_weights([wg_ref, wu_ref], [wgb_ref, wub_ref])
    a = a_ref[...]
    g = _dot(a, wgb_ref[...])
    u = _dot(a, wub_ref[...])
    o_ref[...] = (g * _sigmoid(g) * u).astype(o_ref.dtype)


def _swiglu(a, w_gu, layer):
    r, k = a.shape
    f = w_gu.shape[2] // 2
    tm, tn = _tile(r, MATMUL_ROWS), min(512, _col_tile(f))
    nj = f // tn
    return pl.pallas_call(
        _swiglu_kernel,
        grid=(nj, r // tm),
        in_specs=[pl.BlockSpec((tm, k), lambda j, i: (i, 0)),
                  _weight_spec(w_gu, layer, tn, 0), _weight_spec(w_gu, layer, tn, nj)],
        out_specs=pl.BlockSpec((tm, tn), lambda j, i: (i, j)),
        out_shape=jax.ShapeDtypeStruct((r, f), BF16),
        scratch_shapes=[pltpu.VMEM((k, tn), BF16)] * 2,
        compiler_params=_params("parallel", "arbitrary"),
        name="ffn_up",
    )(a, w_gu, w_gu)


def _matmul_res_kernel(a_ref, w_ref, r_ref, o_ref, wb_ref):
    _cast_weights([w_ref], [wb_ref])
    o_ref[...] = r_ref[...] + _dot(a_ref[...], wb_ref[...])


def _matmul_res(a, w, layer, res, name, rows, tn_cap):
    r, k = a.shape
    n = w.shape[2]
    tm, tn = _tile(r, rows), min(tn_cap, _col_tile(n))
    return pl.pallas_call(
        _matmul_res_kernel,
        grid=(n // tn, r // tm),
        in_specs=[pl.BlockSpec((tm, k), lambda j, i: (i, 0)), _weight_spec(w, layer, tn, 0),
                  pl.BlockSpec((tm, tn), lambda j, i: (i, j))],
        out_specs=pl.BlockSpec((tm, tn), lambda j, i: (i, j)),
        out_shape=jax.ShapeDtypeStruct((r, n), F32),
        scratch_shapes=[pltpu.VMEM((k, tn), BF16)],
        compiler_params=_params("parallel", "arbitrary"),
        name=name,
    )(a, w, res)


def _matmul_res_joined_kernel(a_ref, w_ref, rh_ref, rt_ref, o_ref, wb_ref, *, n_head):
    _cast_weights([w_ref], [wb_ref])
    o_ref[...] = _rows_of(rh_ref, rt_ref, pl.program_id(1), n_head) + _dot(a_ref[...], wb_ref[...])


def _matmul_res_joined(a, w, layer, res_head, res_tail, name):
    r, k = a.shape
    n = w.shape[2]
    tm, tn = res_tail.shape[0], min(1024, _col_tile(n))
    n_head = res_head.shape[0] // tm
    assert res_head.shape[0] % tm == 0 and r == res_head.shape[0] + tm
    return pl.pallas_call(
        functools.partial(_matmul_res_joined_kernel, n_head=n_head),
        grid=(n // tn, r // tm),
        in_specs=[pl.BlockSpec((tm, k), lambda j, i: (i, 0)), _weight_spec(w, layer, tn, 0),
                  pl.BlockSpec((tm, tn), lambda j, i: (jnp.minimum(i, n_head - 1), j)),
                  pl.BlockSpec((tm, tn), lambda j, i: (0, j))],
        out_specs=pl.BlockSpec((tm, tn), lambda j, i: (i, j)),
        out_shape=jax.ShapeDtypeStruct((r, n), F32),
        scratch_shapes=[pltpu.VMEM((k, tn), BF16)],
        compiler_params=_params("parallel", "arbitrary"),
        name=name,
    )(a, w, res_head, res_tail)


def _matmul_res_split_kernel(a_ref, w_ref, r_ref, head_ref, tail_ref, wb_ref, *, n_head):
    _cast_weights([w_ref], [wb_ref])
    i = pl.program_id(1)
    val = r_ref[...] + _dot(a_ref[...], wb_ref[...])

    @pl.when(i < n_head)
    def _():
        head_ref[...] = val

    @pl.when(i == n_head)
    def _():
        tail_ref[...] = val


def _matmul_res_split(a, w, layer, res, name, head_rows):
    r, k = a.shape
    n = w.shape[2]
    tm, tn = r - head_rows, min(512, _col_tile(n))
    assert head_rows % tm == 0
    n_head = head_rows // tm
    return pl.pallas_call(
        functools.partial(_matmul_res_split_kernel, n_head=n_head),
        grid=(n // tn, r // tm),
        in_specs=[pl.BlockSpec((tm, k), lambda j, i: (i, 0)), _weight_spec(w, layer, tn, 0),
                  pl.BlockSpec((tm, tn), lambda j, i: (i, j))],
        out_specs=[pl.BlockSpec((tm, tn), lambda j, i: (jnp.minimum(i, n_head - 1), j)),
                   pl.BlockSpec((tm, tn), lambda j, i: (0, j))],
        out_shape=[jax.ShapeDtypeStruct((head_rows, n), F32), jax.ShapeDtypeStruct((tm, n), F32)],
        scratch_shapes=[pltpu.VMEM((k, tn), BF16)],
        compiler_params=_params("arbitrary", "arbitrary"),
        name=name,
    )(a, w, res)


def _merge_kernel(hm_ref, hd_ref, wpm_ref, wpd_ref, gm_ref, gd_ref, o_ref, wpmb_ref, wpdb_ref):
    _cast_weights([wpm_ref, wpd_ref], [wpmb_ref, wpdb_ref])
    zm = _dot(hm_ref[...].astype(BF16), wpmb_ref[...])
    zd = _dot(hd_ref[...].astype(BF16), wpdb_ref[...])
    gm = gm_ref[...].astype(F32)
    gd = gd_ref[...].astype(F32)
    o_ref[...] = (_sigmoid(gm) * zm + _sigmoid(gd) * zd).astype(o_ref.dtype)


def _merge(hm, hd, w_pm, w_pd, layer, gates, gate_col):
    r = hm.shape[0]
    n = w_pm.shape[2]
    tm = _tile(r, MATMUL_ROWS // 4)
    assert gate_col % LANES == 0

    def gate_spec(col0):
        return pl.BlockSpec((pl.Element(tm), pl.Element(n)), lambda j, i: (pl.multiple_of(i * tm, tm), col0))

    def resident(w):
        return pl.BlockSpec((None, w.shape[1], n), lambda j, i: (layer, 0, 0), pipeline_mode=pl.Buffered(1))

    return pl.pallas_call(
        _merge_kernel,
        grid=(1, r // tm),
        in_specs=[pl.BlockSpec((tm, M_WIDTH), lambda j, i: (i, 0)),
                  pl.BlockSpec((tm, D_V_WIDTH), lambda j, i: (i, 0)),
                  resident(w_pm), resident(w_pd), gate_spec(gate_col), gate_spec(gate_col + n)],
        out_specs=pl.BlockSpec((tm, n), lambda j, i: (i, 0)),
        out_shape=jax.ShapeDtypeStruct((r, n), BF16),
        scratch_shapes=[pltpu.VMEM((M_WIDTH, n), BF16), pltpu.VMEM((D_V_WIDTH, n), BF16)],
        compiler_params=_params("arbitrary", "arbitrary"),
        name="merge",
    )(hm, hd, w_pm, w_pd, gates, gates)


def _half_norm(x, gain):
    lo = lax.broadcasted_iota(jnp.int32, x.shape, 1) < D_QK_DIM
    x2 = x * x
    s_lo = jnp.sum(jnp.where(lo, x2, 0.0), axis=1, keepdims=True)
    s_hi = jnp.sum(jnp.where(lo, 0.0, x2), axis=1, keepdims=True)
    ms = jnp.where(lo, s_lo, s_hi) * (1.0 / D_QK_DIM)
    return x * lax.rsqrt(ms + EPS) * gain


def _qk_norm_kernel(q_ref, k_ref, v_ref, qg_ref, kg_ref, *refs, n_prompt_tiles):
    qo_ref, kb_ref, vt_ref, kp_ref, vp_ref, kt_ref, vt8_ref = refs[-7:]
    i = pl.program_id(0)
    kn = []
    for h in range(D_HEADS):
        sl = slice(h * LANES, (h + 1) * LANES)
        qo_ref[:, sl] = _half_norm(q_ref[:, sl], qg_ref[...]) * Q_SCALE
        kn.append(_half_norm(k_ref[:, sl], kg_ref[...]))
        kb_ref[:, sl] = kn[h].astype(BF16)
    for j in range(vt_ref.shape[0]):
        vt_ref[j] = v_ref[j * ROW_BLOCK:(j + 1) * ROW_BLOCK, :].T.astype(BF16)

    def scatter(k_dst, v_dst):
        tm = k_ref.shape[0]
        for h in range(D_HEADS):
            rows_h = pl.ds(h, tm, stride=D_HEADS)
            k_dst[rows_h, :] = kn[h]
            v_dst[rows_h, :] = v_ref[:, h * LANES:(h + 1) * LANES]

    @pl.when(i < n_prompt_tiles)
    def _():
        scatter(kp_ref.at[0, 0], vp_ref.at[0, 0])

    @pl.when(i == n_prompt_tiles)
    def _():
        scatter(kt_ref, vt8_ref)


def _qk_norm(qkv, q_gain, k_gain, layer, k_out, v_out):
    r = qkv.shape[0]
    w = D_QK_WIDTH
    n_prompt = _rows()[0]
    tm = r - n_prompt
    assert n_prompt % tm == 0 and SEQ % tm == 0 and tm % ROW_BLOCK == 0
    n_prompt_tiles, tiles_per_seq = n_prompt // tm, SEQ // tm
    spec = lambda c: pl.BlockSpec((tm, w), lambda i: (i, c))
    gspec = pl.BlockSpec((1, LANES), lambda i: (0, 0))

    def prompt_map(i):
        t = jnp.minimum(i, n_prompt_tiles - 1)
        row = pl.multiple_of((N_META + (t % tiles_per_seq) * tm) * D_HEADS, SUBLANES)
        return (layer, t // tiles_per_seq, row, 0)

    pspec = pl.BlockSpec(tuple(pl.Element(s) for s in (1, 1, tm * D_HEADS, LANES)), prompt_map)
    tspec = pl.BlockSpec((tm * D_HEADS, LANES), lambda i: (0, 0))
    out5 = jax.ShapeDtypeStruct((DEPTH, BATCH, (N_META + SEQ) * D_HEADS, LANES), F32)
    tail = jax.ShapeDtypeStruct((tm * D_HEADS, LANES), F32)
    carried = [] if k_out is None else [k_out, v_out]
    n_in = 5 + len(carried)
    return pl.pallas_call(
        functools.partial(_qk_norm_kernel, n_prompt_tiles=n_prompt_tiles),
        grid=(r // tm,),
        in_specs=[spec(0), spec(1), spec(2), gspec, gspec] + [pl.BlockSpec(memory_space=pl.ANY)] * len(carried),
        out_specs=[spec(0), spec(0), pl.BlockSpec((tm // ROW_BLOCK, D_V_WIDTH, ROW_BLOCK), lambda i: (i, 0, 0)),
                   pspec, pspec, tspec, tspec],
        out_shape=[jax.ShapeDtypeStruct((r, w), F32), jax.ShapeDtypeStruct((r, w), BF16),
                   jax.ShapeDtypeStruct((r // ROW_BLOCK, D_V_WIDTH, ROW_BLOCK), BF16), out5, out5, tail, tail],
        input_output_aliases={} if k_out is None else {n_in - 2: 3, n_in - 1: 4},
        compiler_params=_params("arbitrary"),
        name="qk_norm",
    )(qkv, qkv, qkv, jnp.tile(q_gain, 2).reshape(1, LANES), jnp.tile(k_gain, 2).reshape(1, LANES), *carried)


def _split3(x):
    hi = x.astype(BF16)
    r1 = x - hi.astype(F32)
    mid = r1.astype(BF16)
    lo = (r1 - mid.astype(F32)).astype(BF16)
    return hi, mid, lo


def _log_sigmoid(x):
    return jnp.minimum(x, 0.0) - jnp.log1p(jnp.exp(-jnp.abs(x)))


def _mlstm_gates(graw_col, graw_row, bias_col, bias_row, n_valid):
    l = graw_col.shape[0]
    g_col = graw_col + bias_col
    g_row = graw_row + bias_row
    is_f_col = lax.broadcasted_iota(jnp.int32, g_col.shape, 1) >= M_HEADS
    is_f_row = lax.broadcasted_iota(jnp.int32, g_row.shape, 0) >= M_HEADS
    ok_col = lax.broadcasted_iota(jnp.int32, g_col.shape, 0) < n_valid
    ok_row = lax.broadcasted_iota(jnp.int32, g_row.shape, 1) < n_valid
    lf_col = jnp.where(is_f_col & ok_col, _log_sigmoid(g_col), 0.0)
    lf_row = jnp.where(is_f_row & ok_row, _log_sigmoid(g_row), 0.0)
    ig_col = jnp.where(ok_col, g_col, NEG_INF)
    ig_row = jnp.where(ok_row, g_row, NEG_INF)
    rr = lax.broadcasted_iota(jnp.int32, (l, l), 0)
    cc = lax.broadcasted_iota(jnp.int32, (l, l), 1)
    tri_l = (cc <= rr).astype(BF16)
    tri_u = (rr <= cc).astype(BF16)
    b_col = sum(_dot(tri_l, p) for p in _split3(lf_col))
    b_row = sum(_dot(p, tri_u) for p in _split3(lf_row))
    return ig_col, b_col, ig_row, b_row, cc <= rr


def _mlstm_chunk(q, k, v, ig_col, b_col, ig_row, b_row, causal, c_st, n_st, m_st):
    l = q.shape[0]
    d = b_col - b_row + ig_row
    d = jnp.where(causal, d, NEG_INF)
    inter = b_col + m_st
    m_t = jnp.maximum(inter, jnp.max(d, axis=1, keepdims=True))
    w = jnp.exp(d - m_t)
    s = _dot_nt(q, k) * w
    a = jnp.exp(inter - m_t)
    num = _dot(s.astype(BF16), v) + a * _dot_nt(q, c_st.astype(BF16))
    den = jnp.sum(s, axis=1, keepdims=True) + a * jnp.sum(q.astype(F32) * n_st, axis=1, keepdims=True)
    h = num * (1.0 / jnp.maximum(jnp.abs(den), jnp.exp(-m_t)))

    b_last = b_col[l - 1:l, :]
    g_col = b_last - b_col + ig_col
    g_row = b_last - b_row + ig_row
    m_new = jnp.maximum(b_last + m_st, jnp.max(g_row, axis=1, keepdims=True))
    ws_col = jnp.exp(g_col - m_new)
    ws_row = jnp.exp(g_row - m_new)
    decay = jnp.exp(b_last + m_st - m_new)
    wv = (v.astype(F32) * ws_col).astype(BF16)
    c_new = decay * c_st + _dot_tn(wv, k)
    n_new = decay * n_st + _dot(ws_row.astype(BF16), k)
    return h, c_new, n_new, m_new


def _mlstm_heads(q_of, k_of, v_of, om_of, gates, gain_ref, c_ref, n_ref, m_ref, h_out):
    ig_col, b_col, ig_row, b_row, causal = gates
    for h in range(M_HEADS):
        f = M_HEADS + h
        hs = slice(h * M_HEAD_DIM, (h + 1) * M_HEAD_DIM)
        out, c_new, n_new, m_new = _mlstm_chunk(
            q_of(hs), k_of(hs), v_of(hs),
            ig_col[:, h:h + 1], b_col[:, f:f + 1], ig_row[h:h + 1, :], b_row[f:f + 1, :], causal,
            c_ref[0, h], n_ref[0, h:h + 1, :], m_ref[0, h:h + 1, 0:1])
        c_ref[0, h] = c_new
        n_ref[0, h:h + 1, :] = n_new
        m_ref[0, h:h + 1, :] = jnp.broadcast_to(m_new, (1, LANES))
        h_out(hs, _rms(out, gain_ref[:, hs]) * _sigmoid(om_of(hs)))


def _mlstm_prompt_kernel(q_ref, k_ref, v_ref, om_ref, gc_ref, gr_ref, bc_ref, br_ref, gain_ref,
                         h_ref, c_ref, n_ref, m_ref):
    c = pl.program_id(1)

    @pl.when(c == 0)
    def _():
        c_ref[...] = jnp.zeros_like(c_ref)
        n_ref[...] = jnp.zeros_like(n_ref)
        m_ref[...] = jnp.zeros_like(m_ref)

    n_valid = jnp.where(c == 0, N_META, ROW_BLOCK)
    gates = _mlstm_gates(gc_ref[...], gr_ref[...], bc_ref[...], br_ref[...], n_valid)

    def h_out(hs, val):
        h_ref[:, hs] = val

    _mlstm_heads(lambda hs: q_ref[:, hs].astype(BF16), lambda hs: (k_ref[:, hs] * K_SCALE).astype(BF16),
                 lambda hs: v_ref[:, hs].astype(BF16), lambda hs: om_ref[:, hs],
                 gates, gain_ref, c_ref, n_ref, m_ref, h_out)


def _mlstm_prompt(qkv, gates, g_col, g_row, b_if, gain):
    r = qkv.shape[0]
    n_prompt, _, meta_row, _ = _rows()
    nc = SEQ // ROW_BLOCK
    meta_blk = meta_row // ROW_BLOCK

    def rb(b, c):
        return jnp.where(c == 0, meta_blk, b * nc + c - 1)

    def rb_out(b, c):
        return jnp.where((c == 0) & (b > 0), b * nc, rb(b, c))

    spec = lambda col: pl.BlockSpec((ROW_BLOCK, M_WIDTH), lambda b, c: (rb(b, c), col))
    full = lambda shape: pl.BlockSpec(shape, lambda b, c: (0,) * len(shape))
    return pl.pallas_call(
        _mlstm_prompt_kernel,
        grid=(BATCH, nc + 1),
        in_specs=[spec(0), spec(1), spec(2), spec(3),
                  pl.BlockSpec((ROW_BLOCK, 2 * M_HEADS), lambda b, c: (rb(b, c), 0)),
                  pl.BlockSpec((2 * M_HEADS, ROW_BLOCK), lambda b, c: (0, rb(b, c))),
                  full((1, 2 * M_HEADS)), full((2 * M_HEADS, 1)), full((1, M_WIDTH))],
        out_specs=[pl.BlockSpec((ROW_BLOCK, M_WIDTH), lambda b, c: (rb_out(b, c), 0)),
                   pl.BlockSpec((1, M_HEADS, M_HEAD_DIM, M_HEAD_DIM), lambda b, c: (b, 0, 0, 0)),
                   pl.BlockSpec((1, M_HEADS, M_HEAD_DIM), lambda b, c: (b, 0, 0)),
                   pl.BlockSpec((1, M_HEADS, LANES), lambda b, c: (b, 0, 0))],
        out_shape=[jax.ShapeDtypeStruct((r, M_WIDTH), F32),
                   jax.ShapeDtypeStruct((BATCH, M_HEADS, M_HEAD_DIM, M_HEAD_DIM), F32),
                   jax.ShapeDtypeStruct((BATCH, M_HEADS, M_HEAD_DIM), F32),
                   jax.ShapeDtypeStruct((BATCH, M_HEADS, LANES), F32)],
        compiler_params=_params("arbitrary", "arbitrary"),
        name="mlstm_prompt",
    )(qkv, qkv, qkv, gates, g_col, g_row, b_if.reshape(1, -1), b_if.reshape(-1, 1), gain.reshape(1, -1))


def _mlstm_sample_kernel(q_ref, k_ref, v_ref, om_ref, gc_ref, gr_ref, bc_ref, br_ref, gain_ref,
                         c_in, n_in, m_in, *rest):
    h_ref, c_ref, n_ref, m_ref, qp, kp, vp = rest[-7:]
    c_ref[...] = c_in[...]
    n_ref[...] = n_in[...]
    m_ref[...] = m_in[...]
    for src, dst in ((q_ref, qp), (k_ref, kp), (v_ref, vp)):
        dst[...] = jnp.zeros_like(dst)
        dst[0:DEC_SEQ, :] = src[...]
    gates = _mlstm_gates(gc_ref[0], gr_ref[0], bc_ref[...], br_ref[...], DEC_SEQ)
    om = om_ref[...]

    def h_out(hs, val):
        h_ref[:, hs] = val[0:DEC_SEQ, :]

    def om_of(hs):
        return jnp.concatenate([om[:, hs], jnp.zeros((SAMPLE_CHUNK - DEC_SEQ, M_HEAD_DIM), F32)], axis=0)

    _mlstm_heads(lambda hs: qp[:, hs].astype(BF16), lambda hs: (kp[:, hs] * K_SCALE).astype(BF16),
                 lambda hs: vp[:, hs].astype(BF16), om_of,
                 gates, gain_ref, c_ref, n_ref, m_ref, h_out)


def _mlstm_sample(qkv, gates, g_col, g_row, b_if, gain, layer, state_c, state_n, st_m, hm, c_out, n_out):
    assert DEC_SEQ == SUBLANES
    n_prompt = _rows()[0]
    blk0 = n_prompt // DEC_SEQ
    spec = lambda col: pl.BlockSpec((DEC_SEQ, M_WIDTH), lambda b: (blk0 + b, col))
    full = lambda shape: pl.BlockSpec(shape, lambda b: (0,) * len(shape))
    st_spec = lambda shape: pl.BlockSpec((1,) + shape, lambda b: (b,) + (0,) * len(shape))
    layer_spec = lambda shape: pl.BlockSpec((None, 1) + shape, lambda b: (layer, b) + (0,) * len(shape))
    st_specs = [layer_spec((M_HEADS, M_HEAD_DIM, M_HEAD_DIM)), layer_spec((M_HEADS, M_HEAD_DIM)),
                st_spec((M_HEADS, LANES))]
    carried = [] if c_out is None else [c_out, n_out]
    aliases = {12: 0}
    if carried:
        aliases.update({13: 1, 14: 2})
    return pl.pallas_call(
        _mlstm_sample_kernel,
        grid=(DEC_BATCH,),
        in_specs=[spec(0), spec(1), spec(2), spec(3),
                  st_spec((SAMPLE_CHUNK, 2 * M_HEADS)), st_spec((2 * M_HEADS, SAMPLE_CHUNK)),
                  full((1, 2 * M_HEADS)), full((2 * M_HEADS, 1)), full((1, M_WIDTH))] + st_specs
                 + [pl.BlockSpec(memory_space=pl.ANY)] * (1 + len(carried)),
        out_specs=[spec(0)] + st_specs,
        out_shape=[jax.ShapeDtypeStruct(hm.shape, F32),
                   jax.ShapeDtypeStruct(state_c.shape, F32), jax.ShapeDtypeStruct(state_n.shape, F32),
                   jax.ShapeDtypeStruct((DEC_BATCH, M_HEADS, LANES), F32)],
        scratch_shapes=[pltpu.VMEM((SAMPLE_CHUNK, M_WIDTH), F32)] * 3,
        input_output_aliases=aliases,
        compiler_params=_params("arbitrary"),
        name="mlstm_sample",
    )(qkv, qkv, qkv, gates, g_col, g_row, b_if.reshape(1, -1), b_if.reshape(-1, 1), gain.reshape(1, -1),
      state_c, state_n, jnp.broadcast_to(st_m[:, :, None], (DEC_BATCH, M_HEADS, LANES)), hm, *carried)


def _lambda(lam_ref, lam_init):
    p = lam_ref[...]
    s1 = jnp.sum(p[0:1, :] * p[1:2, :], axis=1, keepdims=True)
    s2 = jnp.sum(p[2:3, :] * p[3:4, :], axis=1, keepdims=True)
    return jnp.exp(s1) - jnp.exp(s2) + lam_init


def _diff_prompt_kernel(slope_ref, q_ref, k_ref, vt_ref, km_ref, vtm_ref, lam_ref, gain_ref, o_ref,
                        base_ref, m_ref, acc_ref, *, lam_init):
    b = pl.program_id(0)
    hg = pl.program_id(1)
    is_meta = pl.program_id(2) == 0
    t = pl.program_id(2) - 1
    tq = ROW_BLOCK
    heads = range(HEAD_GROUP)
    cols = [slice(i * LANES, (i + 1) * LANES) for i in heads]
    slopes = [slope_ref[hg * HEAD_GROUP + i] * LOG2E for i in heads]

    @pl.when(jnp.logical_not(is_meta & (b > 0)))
    def _():
        lo = lax.broadcasted_iota(jnp.int32, (tq, LANES), 1) < D_QK_DIM
        q2 = []
        for i in heads:
            q = q_ref[:, cols[i]].astype(BF16)
            zero = jnp.zeros_like(q)
            q2.append(jnp.concatenate([jnp.where(lo, q, zero), jnp.where(lo, zero, q)], axis=0))
            base_ref[i] = slopes[i] * lax.broadcasted_iota(jnp.int32, (tq, 2 * tq), 0).astype(F32)
        q_pos0 = jnp.where(is_meta, 0, N_META + t * tq)
        ones = jnp.ones((ONES_ROWS, tq), BF16)
        m_ref[...] = jnp.full_like(m_ref, NEG_INF)
        acc_ref[...] = jnp.zeros_like(acc_ref)

        def visible(n_keys, k_off):
            key = lax.broadcasted_iota(jnp.int32, (n_keys, 2 * tq), 0) + k_off
            qry = lax.broadcasted_iota(jnp.int32, (n_keys, 2 * tq), 1)
            return key <= jnp.where(qry >= tq, qry - tq, qry)

        def step(i, parts):
            cs = [slopes[i] * k_off.astype(F32) for _, _, k_off in parts]
            m_old = m_ref[i]
            m_new = m_old
            for (s, _, _), c in zip(parts, cs):
                m_new = jnp.maximum(m_new, jnp.max(s, axis=0, keepdims=True) + c)
            acc = jnp.exp2(m_old - m_new) * acc_ref[i]
            for (s, vt, _), c in zip(parts, cs):
                p = jnp.exp2(s - (m_new - c)).astype(BF16)
                acc = acc + _dot(jnp.concatenate([vt, ones[:, 0:vt.shape[1]]], axis=0), p)
            acc_ref[i] = acc
            m_ref[i] = m_new

        def scores(kb, i):
            ks = pl.ds(pl.multiple_of(kb * tq, tq), tq)
            return _dot_nt(k_ref[ks, cols[i]], q2[i])

        def full_blocks(kbs):
            for i in heads:
                step(i, [(scores(kb, i) + base_ref[i], vt_ref[kb, cols[i], :], N_META + kb * tq - q_pos0)
                         for kb in kbs])

        n_full = jnp.where(is_meta, 0, t)
        done = 0
        for width in KV_UNROLL:
            n_iter = (n_full - done) // width

            def body(j, carry, width=width, done=done):
                full_blocks([done + j * width + u for u in range(width)])
                return carry

            lax.fori_loop(0, n_iter, body, 0)
            done = done + n_iter * width

        td = jnp.maximum(t, 0)
        hide = jnp.where(is_meta, 2 * tq, 0)
        for i in heads:
            s_meta = _dot_nt(km_ref[:, cols[i]], q2[i]) + base_ref[i, 0:N_META, :]
            s_diag = scores(td, i) + base_ref[i]
            step(i, [(jnp.where(visible(N_META, -q_pos0), s_meta, NEG_INF), vtm_ref[0, cols[i], 0:N_META], -q_pos0),
                     (jnp.where(visible(tq, hide), s_diag, NEG_INF), vt_ref[td, cols[i], :], jnp.int32(0))])

        lam = _lambda(lam_ref, lam_init)
        for i in heads:
            acc = acc_ref[i]
            o_t = acc[0:D_V_DIM, :] * (1.0 / acc[D_V_DIM:D_V_DIM + 1, :])
            o = (o_t[:, 0:tq] - lam * o_t[:, tq:2 * tq]).T
            o_ref[:, cols[i]] = _rms(o, gain_ref[:, cols[i]]) * (1.0 - lam_init)


def _diff_prompt(qn, kb, vt, lam_pack, gain, slopes, lam_init):
    r = qn.shape[0]
    meta_row = _rows()[2]
    nq = SEQ // ROW_BLOCK
    meta_blk = meta_row // ROW_BLOCK
    gw = HEAD_GROUP * LANES
    assert meta_row % N_META == 0 and D_V_DIM == LANES and D_HEADS % HEAD_GROUP == 0

    def qmap(b, h, qi, *_):
        return (jnp.where(qi == 0, jnp.where(b == 0, meta_blk, b * nq), b * nq + qi - 1), h)

    grid_spec = pltpu.PrefetchScalarGridSpec(
        num_scalar_prefetch=1,
        grid=(BATCH, D_HEADS // HEAD_GROUP, nq + 1),
        in_specs=[pl.BlockSpec((ROW_BLOCK, gw), qmap),
                  pl.BlockSpec((SEQ, gw), lambda b, h, qi, *_: (b, h)),
                  pl.BlockSpec((nq, gw, ROW_BLOCK), lambda b, h, qi, *_: (b, h, 0)),
                  pl.BlockSpec((N_META, gw), lambda b, h, qi, *_: (meta_row // N_META, h)),
                  pl.BlockSpec((1, gw, ROW_BLOCK), lambda b, h, qi, *_: (meta_blk, h, 0)),
                  pl.BlockSpec((SUBLANES, LANES), lambda b, h, qi, *_: (0, 0)),
                  pl.BlockSpec((1, gw), lambda b, h, qi, *_: (0, h))],
        out_specs=pl.BlockSpec((ROW_BLOCK, gw), qmap),
        scratch_shapes=[pltpu.VMEM((HEAD_GROUP, ROW_BLOCK, 2 * ROW_BLOCK), F32),
                        pltpu.VMEM((HEAD_GROUP, 1, 2 * ROW_BLOCK), F32),
                        pltpu.VMEM((HEAD_GROUP, D_V_DIM + ONES_ROWS, 2 * ROW_BLOCK), F32)])
    return pl.pallas_call(
        functools.partial(_diff_prompt_kernel, lam_init=lam_init),
        grid_spec=grid_spec,
        out_shape=jax.ShapeDtypeStruct((r, D_V_WIDTH), F32),
        compiler_params=_params("arbitrary", "arbitrary", "arbitrary"),
        name="diff_prompt",
    )(slopes, qn, kb, vt, kb, vt, lam_pack, gain.reshape(1, -1))


def _diff_sample_kernel(pt_ref, *refs, lam_init):
    del pt_ref
    pp = PAGES_PER_STEP
    k_pages, v_pages = refs[0:pp], refs[pp:2 * pp]
    (q_ref, kn_ref, vn_ref, lam_ref, gain_ref, hd_in, o_ref,
     qb_ref, kp_ref, vp_ref, m_ref, acc_ref) = refs[2 * pp:]
    del hd_in
    g = pl.program_id(1)
    qrows = 2 * DEC_SEQ
    half = D_HEADS * DEC_SEQ
    slopes = [2.0 ** (-8.0 * (h + 1) / D_HEADS) * LOG2E for h in range(D_HEADS)]

    def step(h, s, v):
        m_old = m_ref[h]
        m_new = jnp.maximum(m_old, jnp.max(s, axis=1, keepdims=True))
        p = jnp.exp2(s - m_new).astype(BF16)
        v_ones = jnp.concatenate([v, jnp.ones_like(v)], axis=1)
        acc_ref[h] = jnp.exp2(m_old - m_new) * acc_ref[h] + _dot(p, v_ones)
        m_ref[h] = m_new

    @pl.when(g == 0)
    def _():
        lo = lax.broadcasted_iota(jnp.int32, (DEC_SEQ, LANES), 1) < D_QK_DIM
        for h in range(D_HEADS):
            x = q_ref[:, h * LANES:(h + 1) * LANES]
            qb_ref[h] = jnp.concatenate([jnp.where(lo, x, 0.0), jnp.where(lo, 0.0, x)], axis=0).astype(BF16)
        m_ref[...] = jnp.full_like(m_ref, NEG_INF)
        acc_ref[...] = jnp.zeros_like(acc_ref)

    k_pos = (lax.broadcasted_iota(jnp.int32, (1, pp * PAGE_SIZE), 1) + (g * (pp * PAGE_SIZE) - PAST_LEN)).astype(F32)
    for h in range(D_HEADS):
        rows_h = pl.ds(h, PAGE_SIZE, stride=D_HEADS)
        kh = jnp.concatenate([k_pages[p][rows_h, :] for p in range(pp)], axis=0).astype(BF16)
        vh = jnp.concatenate([v_pages[p][rows_h, :] for p in range(pp)], axis=0).astype(BF16)
        step(h, _dot_nt(qb_ref[h], kh) + slopes[h] * k_pos, vh)

    @pl.when(g == pl.num_programs(1) - 1)
    def _():
        for src, dst in ((kn_ref, kp_ref), (vn_ref, vp_ref)):
            dst[...] = jnp.zeros_like(dst)
            dst[0:half, :] = src[...]
        kn = kp_ref[...].astype(BF16)
        vn = vp_ref[...].astype(BF16)
        c = lax.broadcasted_iota(jnp.int32, (qrows, LANES), 1)
        r_tok = lax.broadcasted_iota(jnp.int32, (qrows, LANES), 0) % DEC_SEQ
        c_tok = c // D_HEADS
        lam = _lambda(lam_ref, lam_init)
        for h in range(D_HEADS):
            ok = (c % D_HEADS == h) & (c_tok <= r_tok)
            s = _dot_nt(qb_ref[h], kn) + slopes[h] * c_tok.astype(F32)
            step(h, jnp.where(ok, s, NEG_INF), vn)
            acc = acc_ref[h]
            o = acc[:, 0:LANES] * (1.0 / acc[:, LANES:2 * LANES])
            oh = o[0:DEC_SEQ, :] - lam * o[DEC_SEQ:qrows, :]
            cs = slice(h * LANES, (h + 1) * LANES)
            o_ref[:, cs] = _rms(oh, gain_ref[:, cs]) * (1.0 - lam_init)


def _diff_sample(layer, cache_k, cache_v, page_table, qn, kf, vf, lam_pack, gain, hd, lam_init):
    assert DEC_SEQ == SUBLANES and D_HEADS == SUBLANES and D_V_DIM == LANES
    assert D_HEADS * DEC_SEQ <= LANES and PAST_LEN % (PAGE_SIZE * PAGES_PER_STEP) == 0
    n_prompt = _rows()[0]
    blk0 = n_prompt // DEC_SEQ
    pp = PAGES_PER_STEP
    width = D_HEADS * LANES
    rows = 2 * D_HEADS * DEC_SEQ

    def as_rows(cache):
        return cache.reshape(cache.shape[0], cache.shape[1], PAGE_SIZE * D_HEADS, LANES)

    def page_spec(p):
        return pl.BlockSpec((None, None, PAGE_SIZE * D_HEADS, LANES),
                            lambda b, g, pt: (layer, pt[b, g * pp + p], 0, 0))

    row_spec = pl.BlockSpec((DEC_SEQ, width), lambda b, g, pt: (blk0 + b, 0))
    new_spec = pl.BlockSpec((DEC_SEQ * D_HEADS, LANES), lambda b, g, pt: (b, 0))
    grid_spec = pltpu.PrefetchScalarGridSpec(
        num_scalar_prefetch=1,
        grid=(DEC_BATCH, PAST_LEN // (PAGE_SIZE * pp)),
        in_specs=[page_spec(p) for p in range(pp)] * 2
                 + [row_spec, new_spec, new_spec,
                    pl.BlockSpec((SUBLANES, LANES), lambda b, g, pt: (0, 0)),
                    pl.BlockSpec((1, width), lambda b, g, pt: (0, 0)),
                    pl.BlockSpec(memory_space=pl.ANY)],
        out_specs=row_spec,
        scratch_shapes=[pltpu.VMEM((D_HEADS, 2 * DEC_SEQ, LANES), BF16),
                        pltpu.VMEM((LANES, LANES), F32), pltpu.VMEM((LANES, LANES), F32),
                        pltpu.VMEM((D_HEADS, 2 * DEC_SEQ, 1), F32), pltpu.VMEM((D_HEADS, 2 * DEC_SEQ, 2 * LANES), F32)])
    n_in = 2 * pp + 6
    return pl.pallas_call(
        functools.partial(_diff_sample_kernel, lam_init=lam_init),
        grid_spec=grid_spec,
        out_shape=jax.ShapeDtypeStruct(hd.shape, F32),
        input_output_aliases={n_in: 0},
        compiler_params=_params("arbitrary", "arbitrary"),
        name="diff_sample",
    )(page_table, *([as_rows(cache_k)] * pp), *([as_rows(cache_v)] * pp), qn, kf, vf, lam_pack,
      gain.reshape(1, -1), hd)


def kernel(x_prompt, x_sample, cache_k, cache_v, state_C, state_n, state_m, page_table, meta_tokens, norm_mix, w_in, b_if, q_gain, k_gain, lambda_q1, lambda_k1, lambda_q2, lambda_k2, mlstm_norm, diff_norm, w_pm, w_pd, w_out, norm_ffn, w_gu, w_down):
    assert M_HEAD_DIM ** -0.5 == 2.0 ** round(math.log2(M_HEAD_DIM ** -0.5))
    n_prompt, n_sample, meta_row, r = _rows()
    x_head = x_prompt.reshape(n_prompt, D_MODEL)
    x_tail = jnp.concatenate([x_sample.reshape(n_sample, D_MODEL), meta_tokens.astype(F32),
                              jnp.zeros((r - meta_row - N_META, D_MODEL), F32)], axis=0)
    x = None
    slopes = 2.0 ** (-8.0 * jnp.arange(1, D_HEADS + 1, dtype=F32) / D_HEADS)
    w_in_t = jnp.swapaxes(w_in, 1, 2)
    g0 = 4 * M_WIDTH
    g1 = g0 + 2 * M_HEADS

    k_prompt = v_prompt = c_sample = n_sample_st = None
    meta_kv = []
    outs = [[] for _ in range(6)]
    for l in range(DEPTH):
        lam_init = 0.8 - 0.6 * math.exp(-0.3 * l)
        if l == 0:
            xn = _rmsnorm_split(x_head, x_tail, norm_mix[l], "norm_mix_first")
        else:
            xn = _rmsnorm(x, norm_mix[l], "norm_mix")
        proj_m = _matmul_nt(xn, w_in_t, l, 0, g0, "proj_mlstm")
        graw = _matmul_nt(xn, w_in_t, l, g0, 2 * M_HEADS, "proj_gates")
        g2 = g1 + 2 * D_QK_WIDTH + D_V_WIDTH
        proj_d = _matmul_nt(xn, w_in_t, l, g1, g2 - g1, "proj_diff")
        proj_g = _matmul_nt(xn, w_in_t, l, g2, w_in_t.shape[1] - g2, "proj_merge_gates", BF16)

        hm, c_p, n_p, m_p = _mlstm_prompt(proj_m, proj_m, graw, graw.T, b_if[l], mlstm_norm[l])
        gs = graw[n_prompt:n_prompt + n_sample].reshape(DEC_BATCH, DEC_SEQ, 2 * M_HEADS)
        gs = jnp.pad(gs, ((0, 0), (0, SAMPLE_CHUNK - DEC_SEQ), (0, 0)))
        hm, c_sample, n_sample_st, m_s = _mlstm_sample(
            proj_m, proj_m, gs, gs.transpose(0, 2, 1), b_if[l], mlstm_norm[l], l, state_C.astype(F32),
            state_n.astype(F32), state_m[l].astype(F32), hm, c_sample, n_sample_st)

        qn, kb, vt, k_prompt, v_prompt, kf, vf = _qk_norm(proj_d, q_gain[l], k_gain[l], l, k_prompt, v_prompt)
        lam_pack = jnp.pad(jnp.stack([lambda_q1[l], lambda_k1[l], lambda_q2[l], lambda_k2[l]]).astype(F32),
                           ((0, SUBLANES - 4), (0, LANES - D_QK_DIM)))
        hd = _diff_prompt(qn, kb, vt, lam_pack, diff_norm[l], slopes, lam_init)
        hd = _diff_sample(l, cache_k, cache_v, page_table, qn, kf, vf, lam_pack, diff_norm[l], hd, lam_init)

        z = _merge(hm, hd, w_pm, w_pd, l, proj_g, 0)
        if l == 0:
            x = _matmul_res_joined(z, w_out, l, x_head, x_tail, "out_proj_first")
        else:
            x = _matmul_res(z, w_out, l, x, "out_proj", MATMUL_ROWS, 1024)
        act = _swiglu(_rmsnorm(x, norm_ffn[l], "norm_ffn"), w_gu, l)
        if l < DEPTH - 1:
            x = _matmul_res(act, w_down, l, x, "ffn_down", MATMUL_ROWS // 2, 512)
        else:
            y_head, y_tail = _matmul_res_split(act, w_down, l, x, "ffn_down_last", n_prompt)

        head_shape = (D_HEADS, D_V_DIM)
        s_rows, m_rows = n_sample * D_HEADS, N_META * D_HEADS
        meta_kv.append((kf[s_rows:s_rows + m_rows], vf[s_rows:s_rows + m_rows]))
        vals = (c_p, n_p, m_p[:, :, 0],
                kf[:s_rows].reshape(DEC_BATCH, DEC_SEQ, *head_shape),
                vf[:s_rows].reshape(DEC_BATCH, DEC_SEQ, *head_shape), m_s[:, :, 0])
        for o, v in zip(outs, vals):
            o.append(v)

    for l, (km, vm) in enumerate(meta_kv):
        k_prompt = k_prompt.at[l, :, :km.shape[0]].set(jnp.broadcast_to(km, (BATCH,) + km.shape))
        v_prompt = v_prompt.at[l, :, :vm.shape[0]].set(jnp.broadcast_to(vm, (BATCH,) + vm.shape))
    kv_shape = (DEPTH, BATCH, N_META + SEQ, D_HEADS, D_V_DIM)
    y_prompt = y_head.reshape(BATCH, SEQ, D_MODEL)
    y_sample = y_tail[:n_sample].reshape(DEC_BATCH, DEC_SEQ, D_MODEL)
    c_p, n_p, m_p, k_s, v_s, m_s = (jnp.stack(o) for o in outs)
    return (y_prompt, y_sample, k_prompt.reshape(kv_shape), v_prompt.reshape(kv_shape), c_p, n_p, m_p, k_s, v_s,
            c_sample, n_sample_st, m_s)
```

```python
import functools
import math

import jax
import jax.numpy as jnp
from jax import lax
from jax.experimental import pallas as pl
from jax.experimental.pallas import tpu as pltpu

D_MODEL = 2048
BATCH = 2
SEQ = 4096
DEPTH = 2
DEC_BATCH = 32
DEC_SEQ = 8
PAST_LEN = 8192
PAGE_SIZE = 128
N_META = 16
M_HEADS = 4
M_HEAD_DIM = 256
M_WIDTH = M_HEADS * M_HEAD_DIM
D_HEADS = 8
D_QK_DIM = 64
D_V_DIM = 2 * D_QK_DIM
D_QK_WIDTH = D_HEADS * 2 * D_QK_DIM
D_V_WIDTH = D_HEADS * D_V_DIM
EPS = 1e-6

LANES = 128
SUBLANES = 8
ROW_BLOCK = 256
MATMUL_ROWS = 1100
K_SCALE = M_HEAD_DIM ** -0.5
SAMPLE_CHUNK = 128
KV_UNROLL = (8, 4, 2, 1)
HEAD_GROUP = 8
PAGES_PER_STEP = 16
VMEM_LIMIT = 56 * 1024 * 1024

F32 = jnp.float32
BF16 = jnp.bfloat16
NEG_INF = float("-inf")
LOG2E = math.log2(math.e)
Q_SCALE = D_QK_DIM ** -0.5 * LOG2E
ONES_ROWS = 16


def _ffn_dim():
    return ((8 * D_MODEL // 3 + 255) // 256) * 256


def _rows():
    n_prompt = BATCH * SEQ
    n_sample = DEC_BATCH * DEC_SEQ
    assert n_prompt % ROW_BLOCK == 0 and n_sample % ROW_BLOCK == 0 and N_META <= ROW_BLOCK
    return n_prompt, n_sample, n_prompt + n_sample, n_prompt + n_sample + ROW_BLOCK


def _tile(n, cap, mult=16):
    for t in range(min(cap, n) // mult * mult, 0, -mult):
        if n % t == 0:
            return t
    raise ValueError((n, cap, mult))


def _col_tile(n):
    for t in (1024, 512, 256, 128):
        if n % t == 0:
            return t
    raise ValueError(n)


def _params(*sem):
    return pltpu.CompilerParams(dimension_semantics=sem, vmem_limit_bytes=VMEM_LIMIT)


def _dot(a, b):
    return jnp.dot(a, b, preferred_element_type=F32)


def _dot_nt(a, b):
    return lax.dot_general(a, b, (((1,), (1,)), ((), ())), preferred_element_type=F32)


def _dot_tn(a, b):
    return lax.dot_general(a, b, (((0,), (0,)), ((), ())), preferred_element_type=F32)


def _sigmoid(x):
    return 1.0 / (1.0 + jnp.exp(-x))


def _rms(x, gain):
    ms = jnp.mean(x * x, axis=-1, keepdims=True)
    return x * lax.rsqrt(ms + EPS) * gain


def _rmsnorm_kernel(x_ref, g_ref, o_ref):
    o_ref[...] = _rms(x_ref[...], g_ref[...]).astype(o_ref.dtype)


def _rmsnorm(x, gain, name):
    r, d = x.shape
    tm = _tile(r, MATMUL_ROWS)
    return pl.pallas_call(
        _rmsnorm_kernel,
        grid=(r // tm,),
        in_specs=[pl.BlockSpec((tm, d), lambda i: (i, 0)), pl.BlockSpec((1, d), lambda i: (0, 0))],
        out_specs=pl.BlockSpec((tm, d), lambda i: (i, 0)),
        out_shape=jax.ShapeDtypeStruct((r, d), BF16),
        compiler_params=_params("parallel"),
        name=name,
    )(x, gain.reshape(1, d))


def _rows_of(head_ref, tail_ref, i, n_head):
    return jnp.where(i < n_head, head_ref[...], tail_ref[...])


def _rmsnorm_split_kernel(head_ref, tail_ref, g_ref, o_ref, *, n_head):
    x = _rows_of(head_ref, tail_ref, pl.program_id(0), n_head)
    o_ref[...] = _rms(x, g_ref[...]).astype(o_ref.dtype)


def _rmsnorm_split(head, tail, gain, name):
    tm, d = tail.shape
    n_head = head.shape[0] // tm
    assert head.shape[0] % tm == 0
    return pl.pallas_call(
        functools.partial(_rmsnorm_split_kernel, n_head=n_head),
        grid=(n_head + 1,),
        in_specs=[pl.BlockSpec((tm, d), lambda i: (jnp.minimum(i, n_head - 1), 0)),
                  pl.BlockSpec((tm, d), lambda i: (0, 0)), pl.BlockSpec((1, d), lambda i: (0, 0))],
        out_specs=pl.BlockSpec((tm, d), lambda i: (i, 0)),
        out_shape=jax.ShapeDtypeStruct((head.shape[0] + tm, d), BF16),
        compiler_params=_params("parallel"),
        name=name,
    )(head, tail, gain.reshape(1, d))


def _weight_spec(w, layer, tn, col0):
    if w.ndim == 3:
        return pl.BlockSpec((None, w.shape[1], tn), lambda j, i: (layer, 0, col0 + j))
    return pl.BlockSpec((w.shape[0], tn), lambda j, i: (0, col0 + j))


def _cast_weights(w_refs, wb_refs):
    @pl.when(pl.program_id(1) == 0)
    def _():
        for w_ref, wb_ref in zip(w_refs, wb_refs):
            wb_ref[...] = w_ref[...].astype(BF16)


def _matmul_nt_kernel(a_ref, w_ref, o_ref, wb_ref):
    _cast_weights([w_ref.at[0]], [wb_ref])
    o_ref[...] = _dot_nt(a_ref[...], wb_ref[...]).astype(o_ref.dtype)


def _matmul_nt(a, w_t, layer, row0, n, name, out_dtype=F32):
    r, k = a.shape
    tm, tn = _tile(r, MATMUL_ROWS), _tile(n, 1024, SUBLANES)
    assert row0 % SUBLANES == 0
    return pl.pallas_call(
        _matmul_nt_kernel,
        grid=(n // tn, r // tm),
        in_specs=[pl.BlockSpec((tm, k), lambda j, i: (i, 0)),
                  pl.BlockSpec((pl.Element(1), pl.Element(tn), pl.Element(k)),
                               lambda j, i: (layer, pl.multiple_of(row0 + j * tn, SUBLANES), 0))],
        out_specs=pl.BlockSpec((tm, tn), lambda j, i: (i, j)),
        out_shape=jax.ShapeDtypeStruct((r, n), out_dtype),
        scratch_shapes=[pltpu.VMEM((tn, k), BF16)],
        compiler_params=_params("parallel", "arbitrary"),
        name=name,
    )(a, w_t)


def _swiglu_kernel(a_ref, wg_ref, wu_ref, o_ref, wgb_ref, wub_ref):
    _cast_weights([wg_ref, wu_ref], [wgb_ref, wub_ref])
    a = a_ref[...]
    g = _dot(a, wgb_ref[...])
    u = _dot(a, wub_ref[...])
    o_ref[...] = (g * _sigmoid(g) * u).astype(o_ref.dtype)


def _swiglu(a, w_gu, layer):
    r, k = a.shape
    f = w_gu.shape[2] // 2
    tm, tn = _tile(r, MATMUL_ROWS), min(512, _col_tile(f))
    nj = f // tn
    return pl.pallas_call(
        _swiglu_kernel,
        grid=(nj, r // tm),
        in_specs=[pl.BlockSpec((tm, k), lambda j, i: (i, 0)),
                  _weight_spec(w_gu, layer, tn, 0), _weight_spec(w_gu, layer, tn, nj)],
        out_specs=pl.BlockSpec((tm, tn), lambda j, i: (i, j)),
        out_shape=jax.ShapeDtypeStruct((r, f), BF16),
        scratch_shapes=[pltpu.VMEM((k, tn), BF16)] * 2,
        compiler_params=_params("parallel", "arbitrary"),
        name="ffn_up",
    )(a, w_gu, w_gu)


def _matmul_res_kernel(a_ref, w_ref, r_ref, o_ref, wb_ref):
    _cast_weights([w_ref], [wb_ref])
    o_ref[...] = r_ref[...] + _dot(a_ref[...], wb_ref[...])


def _matmul_res(a, w, layer, res, name, rows, tn_cap):
    r, k = a.shape
    n = w.shape[2]
    tm, tn = _tile(r, rows), min(tn_cap, _col_tile(n))
    return pl.pallas_call(
        _matmul_res_kernel,
        grid=(n // tn, r // tm),
        in_specs=[pl.BlockSpec((tm, k), lambda j, i: (i, 0)), _weight_spec(w, layer, tn, 0),
                  pl.BlockSpec((tm, tn), lambda j, i: (i, j))],
        out_specs=pl.BlockSpec((tm, tn), lambda j, i: (i, j)),
        out_shape=jax.ShapeDtypeStruct((r, n), F32),
        scratch_shapes=[pltpu.VMEM((k, tn), BF16)],
        compiler_params=_params("parallel", "arbitrary"),
        name=name,
    )(a, w, res)


def _matmul_res_joined_kernel(a_ref, w_ref, rh_ref, rt_ref, o_ref, wb_ref, *, n_head):
    _cast_weights([w_ref], [wb_ref])
    o_ref[...] = _rows_of(rh_ref, rt_ref, pl.program_id(1), n_head) + _dot(a_ref[...], wb_ref[...])


def _matmul_res_joined(a, w, layer, res_head, res_tail, name):
    r, k = a.shape
    n = w.shape[2]
    tm, tn = res_tail.shape[0], min(1024, _col_tile(n))
    n_head = res_head.shape[0] // tm
    assert res_head.shape[0] % tm == 0 and r == res_head.shape[0] + tm
    return pl.pallas_call(
        functools.partial(_matmul_res_joined_kernel, n_head=n_head),
        grid=(n // tn, r // tm),
        in_specs=[pl.BlockSpec((tm, k), lambda j, i: (i, 0)), _weight_spec(w, layer, tn, 0),
                  pl.BlockSpec((tm, tn), lambda j, i: (jnp.minimum(i, n_head - 1), j)),
                  pl.BlockSpec((tm, tn), lambda j, i: (0, j))],
        out_specs=pl.BlockSpec((tm, tn), lambda j, i: (i, j)),
        out_shape=jax.ShapeDtypeStruct((r, n), F32),
        scratch_shapes=[pltpu.VMEM((k, tn), BF16)],
        compiler_params=_params("parallel", "arbitrary"),
        name=name,
    )(a, w, res_head, res_tail)


def _matmul_res_split_kernel(a_ref, w_ref, r_ref, head_ref, tail_ref, wb_ref, *, n_head):
    _cast_weights([w_ref], [wb_ref])
    i = pl.program_id(1)
    val = r_ref[...] + _dot(a_ref[...], wb_ref[...])

    @pl.when(i < n_head)
    def _():
        head_ref[...] = val

    @pl.when(i == n_head)
    def _():
        tail_ref[...] = val


def _matmul_res_split(a, w, layer, res, name, head_rows):
    r, k = a.shape
    n = w.shape[2]
    tm, tn = r - head_rows, min(512, _col_tile(n))
    assert head_rows % tm == 0
    n_head = head_rows // tm
    return pl.pallas_call(
        functools.partial(_matmul_res_split_kernel, n_head=n_head),
        grid=(n // tn, r // tm),
        in_specs=[pl.BlockSpec((tm, k), lambda j, i: (i, 0)), _weight_spec(w, layer, tn, 0),
                  pl.BlockSpec((tm, tn), lambda j, i: (i, j))],
        out_specs=[pl.BlockSpec((tm, tn), lambda j, i: (jnp.minimum(i, n_head - 1), j)),
                   pl.BlockSpec((tm, tn), lambda j, i: (0, j))],
        out_shape=[jax.ShapeDtypeStruct((head_rows, n), F32), jax.ShapeDtypeStruct((tm, n), F32)],
        scratch_shapes=[pltpu.VMEM((k, tn), BF16)],
        compiler_params=_params("arbitrary", "arbitrary"),
        name=name,
    )(a, w, res)


def _merge_kernel(hm_ref, hd_ref, wpm_ref, wpd_ref, gm_ref, gd_ref, o_ref, wpmb_ref, wpdb_ref):
    _cast_weights([wpm_ref, wpd_ref], [wpmb_ref, wpdb_ref])
    zm = _dot(hm_ref[...].astype(BF16), wpmb_ref[...])
    zd = _dot(hd_ref[...].astype(BF16), wpdb_ref[...])
    gm = gm_ref[...].astype(F32)
    gd = gd_ref[...].astype(F32)
    o_ref[...] = (_sigmoid(gm) * zm + _sigmoid(gd) * zd).astype(o_ref.dtype)


def _merge(hm, hd, w_pm, w_pd, layer, gates, gate_col):
    r = hm.shape[0]
    n = w_pm.shape[2]
    tm = _tile(r, MATMUL_ROWS // 4)
    assert gate_col % LANES == 0

    def gate_spec(col0):
        return pl.BlockSpec((pl.Element(tm), pl.Element(n)), lambda j, i: (pl.multiple_of(i * tm, tm), col0))

    def resident(w):
        return pl.BlockSpec((None, w.shape[1], n), lambda j, i: (layer, 0, 0), pipeline_mode=pl.Buffered(1))

    return pl.pallas_call(
        _merge_kernel,
        grid=(1, r // tm),
        in_specs=[pl.BlockSpec((tm, M_WIDTH), lambda j, i: (i, 0)),
                  pl.BlockSpec((tm, D_V_WIDTH), lambda j, i: (i, 0)),
                  resident(w_pm), resident(w_pd), gate_spec(gate_col), gate_spec(gate_col + n)],
        out_specs=pl.BlockSpec((tm, n), lambda j, i: (i, 0)),
        out_shape=jax.ShapeDtypeStruct((r, n), BF16),
        scratch_shapes=[pltpu.VMEM((M_WIDTH, n), BF16), pltpu.VMEM((D_V_WIDTH, n), BF16)],
        compiler_params=_params("arbitrary", "arbitrary"),
        name="merge",
    )(hm, hd, w_pm, w_pd, gates, gates)


def _half_norm(x, gain):
    lo = lax.broadcasted_iota(jnp.int32, x.shape, 1) < D_QK_DIM
    x2 = x * x
    s_lo = jnp.sum(jnp.where(lo, x2, 0.0), axis=1, keepdims=True)
    s_hi = jnp.sum(jnp.where(lo, 0.0, x2), axis=1, keepdims=True)
    ms = jnp.where(lo, s_lo, s_hi) * (1.0 / D_QK_DIM)
    return x * lax.rsqrt(ms + EPS) * gain


def _qk_norm_kernel(q_ref, k_ref, v_ref, qg_ref, kg_ref, *refs, n_prompt_tiles):
    qo_ref, kb_ref, vt_ref, kp_ref, vp_ref, kt_ref, vt8_ref = refs[-7:]
    i = pl.program_id(0)
    kn = []
    for h in range(D_HEADS):
        sl = slice(h * LANES, (h + 1) * LANES)
        qo_ref[:, sl] = _half_norm(q_ref[:, sl], qg_ref[...]) * Q_SCALE
        kn.append(_half_norm(k_ref[:, sl], kg_ref[...]))
        kb_ref[:, sl] = kn[h].astype(BF16)
    for j in range(vt_ref.shape[0]):
        vt_ref[j] = v_ref[j * ROW_BLOCK:(j + 1) * ROW_BLOCK, :].T.astype(BF16)

    def scatter(k_dst, v_dst):
        tm = k_ref.shape[0]
        for h in range(D_HEADS):
            rows_h = pl.ds(h, tm, stride=D_HEADS)
            k_dst[rows_h, :] = kn[h]
            v_dst[rows_h, :] = v_ref[:, h * LANES:(h + 1) * LANES]

    @pl.when(i < n_prompt_tiles)
    def _():
        scatter(kp_ref.at[0, 0], vp_ref.at[0, 0])

    @pl.when(i == n_prompt_tiles)
    def _():
        scatter(kt_ref, vt8_ref)


def _qk_norm(qkv, q_gain, k_gain, layer, k_out, v_out):
    r = qkv.shape[0]
    w = D_QK_WIDTH
    n_prompt = _rows()[0]
    tm = r - n_prompt
    assert n_prompt % tm == 0 and SEQ % tm == 0 and tm % ROW_BLOCK == 0
    n_prompt_tiles, tiles_per_seq = n_prompt // tm, SEQ // tm
    spec = lambda c: pl.BlockSpec((tm, w), lambda i: (i, c))
    gspec = pl.BlockSpec((1, LANES), lambda i: (0, 0))

    def prompt_map(i):
        t = jnp.minimum(i, n_prompt_tiles - 1)
        row = pl.multiple_of((N_META + (t % tiles_per_seq) * tm) * D_HEADS, SUBLANES)
        return (layer, t // tiles_per_seq, row, 0)

    pspec = pl.BlockSpec(tuple(pl.Element(s) for s in (1, 1, tm * D_HEADS, LANES)), prompt_map)
    tspec = pl.BlockSpec((tm * D_HEADS, LANES), lambda i: (0, 0))
    out5 = jax.ShapeDtypeStruct((DEPTH, BATCH, (N_META + SEQ) * D_HEADS, LANES), F32)
    tail = jax.ShapeDtypeStruct((tm * D_HEADS, LANES), F32)
    carried = [] if k_out is None else [k_out, v_out]
    n_in = 5 + len(carried)
    return pl.pallas_call(
        functools.partial(_qk_norm_kernel, n_prompt_tiles=n_prompt_tiles),
        grid=(r // tm,),
        in_specs=[spec(0), spec(1), spec(2), gspec, gspec] + [pl.BlockSpec(memory_space=pl.ANY)] * len(carried),
        out_specs=[spec(0), spec(0), pl.BlockSpec((tm // ROW_BLOCK, D_V_WIDTH, ROW_BLOCK), lambda i: (i, 0, 0)),
                   pspec, pspec, tspec, tspec],
        out_shape=[jax.ShapeDtypeStruct((r, w), F32), jax.ShapeDtypeStruct((r, w), BF16),
                   jax.ShapeDtypeStruct((r // ROW_BLOCK, D_V_WIDTH, ROW_BLOCK), BF16), out5, out5, tail, tail],
        input_output_aliases={} if k_out is None else {n_in - 2: 3, n_in - 1: 4},
        compiler_params=_params("arbitrary"),
        name="qk_norm",
    )(qkv, qkv, qkv, jnp.tile(q_gain, 2).reshape(1, LANES), jnp.tile(k_gain, 2).reshape(1, LANES), *carried)


def _split3(x):
    hi = x.astype(BF16)
    r1 = x - hi.astype(F32)
    mid = r1.astype(BF16)
    lo = (r1 - mid.astype(F32)).astype(BF16)
    return hi, mid, lo


def _log_sigmoid(x):
    return jnp.minimum(x, 0.0) - jnp.log1p(jnp.exp(-jnp.abs(x)))


def _mlstm_gates(graw_col, graw_row, bias_col, bias_row, n_valid):
    l = graw_col.shape[0]
    g_col = graw_col + bias_col
    g_row = graw_row + bias_row
    is_f_col = lax.broadcasted_iota(jnp.int32, g_col.shape, 1) >= M_HEADS
    is_f_row = lax.broadcasted_iota(jnp.int32, g_row.shape, 0) >= M_HEADS
    ok_col = lax.broadcasted_iota(jnp.int32, g_col.shape, 0) < n_valid
    ok_row = lax.broadcasted_iota(jnp.int32, g_row.shape, 1) < n_valid
    lf_col = jnp.where(is_f_col & ok_col, _log_sigmoid(g_col), 0.0)
    lf_row = jnp.where(is_f_row & ok_row, _log_sigmoid(g_row), 0.0)
    ig_col = jnp.where(ok_col, g_col, NEG_INF)
    ig_row = jnp.where(ok_row, g_row, NEG_INF)
    rr = lax.broadcasted_iota(jnp.int32, (l, l), 0)
    cc = lax.broadcasted_iota(jnp.int32, (l, l), 1)
    tri_l = (cc <= rr).astype(BF16)
    tri_u = (rr <= cc).astype(BF16)
    b_col = sum(_dot(tri_l, p) for p in _split3(lf_col))
    b_row = sum(_dot(p, tri_u) for p in _split3(lf_row))
    return ig_col, b_col, ig_row, b_row, cc <= rr


def _mlstm_chunk(q, k, v, ig_col, b_col, ig_row, b_row, causal, c_st, n_st, m_st):
    l = q.shape[0]
    d = b_col - b_row + ig_row
    d = jnp.where(causal, d, NEG_INF)
    inter = b_col + m_st
    m_t = jnp.maximum(inter, jnp.max(d, axis=1, keepdims=True))
    w = jnp.exp(d - m_t)
    s = _dot_nt(q, k) * w
    a = jnp.exp(inter - m_t)
    num = _dot(s.astype(BF16), v) + a * _dot_nt(q, c_st.astype(BF16))
    den = jnp.sum(s, axis=1, keepdims=True) + a * jnp.sum(q.astype(F32) * n_st, axis=1, keepdims=True)
    h = num * (1.0 / jnp.maximum(jnp.abs(den), jnp.exp(-m_t)))

    b_last = b_col[l - 1:l, :]
    g_col = b_last - b_col + ig_col
    g_row = b_last - b_row + ig_row
    m_new = jnp.maximum(b_last + m_st, jnp.max(g_row, axis=1, keepdims=True))
    ws_col = jnp.exp(g_col - m_new)
    ws_row = jnp.exp(g_row - m_new)
    decay = jnp.exp(b_last + m_st - m_new)
    wv = (v.astype(F32) * ws_col).astype(BF16)
    c_new = decay * c_st + _dot_tn(wv, k)
    n_new = decay * n_st + _dot(ws_row.astype(BF16), k)
    return h, c_new, n_new, m_new


def _mlstm_heads(q_of, k_of, v_of, om_of, gates, gain_ref, c_ref, n_ref, m_ref, h_out):
    ig_col, b_col, ig_row, b_row, causal = gates
    for h in range(M_HEADS):
        f = M_HEADS + h
        hs = slice(h * M_HEAD_DIM, (h + 1) * M_HEAD_DIM)
        out, c_new, n_new, m_new = _mlstm_chunk(
            q_of(hs), k_of(hs), v_of(hs),
            ig_col[:, h:h + 1], b_col[:, f:f + 1], ig_row[h:h + 1, :], b_row[f:f + 1, :], causal,
            c_ref[0, h], n_ref[0, h:h + 1, :], m_ref[0, h:h + 1, 0:1])
        c_ref[0, h] = c_new
        n_ref[0, h:h + 1, :] = n_new
        m_ref[0, h:h + 1, :] = jnp.broadcast_to(m_new, (1, LANES))
        h_out(hs, _rms(out, gain_ref[:, hs]) * _sigmoid(om_of(hs)))


def _mlstm_prompt_kernel(q_ref, k_ref, v_ref, om_ref, gc_ref, gr_ref, bc_ref, br_ref, gain_ref,
                         h_ref, c_ref, n_ref, m_ref):
    c = pl.program_id(1)

    @pl.when(c == 0)
    def _():
        c_ref[...] = jnp.zeros_like(c_ref)
        n_ref[...] = jnp.zeros_like(n_ref)
        m_ref[...] = jnp.zeros_like(m_ref)

    n_valid = jnp.where(c == 0, N_META, ROW_BLOCK)
    gates = _mlstm_gates(gc_ref[...], gr_ref[...], bc_ref[...], br_ref[...], n_valid)

    def h_out(hs, val):
        h_ref[:, hs] = val

    _mlstm_heads(lambda hs: q_ref[:, hs].astype(BF16), lambda hs: (k_ref[:, hs] * K_SCALE).astype(BF16),
                 lambda hs: v_ref[:, hs].astype(BF16), lambda hs: om_ref[:, hs],
                 gates, gain_ref, c_ref, n_ref, m_ref, h_out)


def _mlstm_prompt(qkv, gates, g_col, g_row, b_if, gain):
    r = qkv.shape[0]
    n_prompt, _, meta_row, _ = _rows()
    nc = SEQ // ROW_BLOCK
    meta_blk = meta_row // ROW_BLOCK

    def rb(b, c):
        return jnp.where(c == 0, meta_blk, b * nc + c - 1)

    def rb_out(b, c):
        return jnp.where((c == 0) & (b > 0), b * nc, rb(b, c))

    spec = lambda col: pl.BlockSpec((ROW_BLOCK, M_WIDTH), lambda b, c: (rb(b, c), col))
    full = lambda shape: pl.BlockSpec(shape, lambda b, c: (0,) * len(shape))
    return pl.pallas_call(
        _mlstm_prompt_kernel,
        grid=(BATCH, nc + 1),
        in_specs=[spec(0), spec(1), spec(2), spec(3),
                  pl.BlockSpec((ROW_BLOCK, 2 * M_HEADS), lambda b, c: (rb(b, c), 0)),
                  pl.BlockSpec((2 * M_HEADS, ROW_BLOCK), lambda b, c: (0, rb(b, c))),
                  full((1, 2 * M_HEADS)), full((2 * M_HEADS, 1)), full((1, M_WIDTH))],
        out_specs=[pl.BlockSpec((ROW_BLOCK, M_WIDTH), lambda b, c: (rb_out(b, c), 0)),
                   pl.BlockSpec((1, M_HEADS, M_HEAD_DIM, M_HEAD_DIM), lambda b, c: (b, 0, 0, 0)),
                   pl.BlockSpec((1, M_HEADS, M_HEAD_DIM), lambda b, c: (b, 0, 0)),
                   pl.BlockSpec((1, M_HEADS, LANES), lambda b, c: (b, 0, 0))],
        out_shape=[jax.ShapeDtypeStruct((r, M_WIDTH), F32),
                   jax.ShapeDtypeStruct((BATCH, M_HEADS, M_HEAD_DIM, M_HEAD_DIM), F32),
                   jax.ShapeDtypeStruct((BATCH, M_HEADS, M_HEAD_DIM), F32),
                   jax.ShapeDtypeStruct((BATCH, M_HEADS, LANES), F32)],
        compiler_params=_params("arbitrary", "arbitrary"),
        name="mlstm_prompt",
    )(qkv, qkv, qkv, gates, g_col, g_row, b_if.reshape(1, -1), b_if.reshape(-1, 1), gain.reshape(1, -1))


def _mlstm_sample_kernel(q_ref, k_ref, v_ref, om_ref, gc_ref, gr_ref, bc_ref, br_ref, gain_ref,
                         c_in, n_in, m_in, *rest):
    h_ref, c_ref, n_ref, m_ref, qp, kp, vp = rest[-7:]
    c_ref[...] = c_in[...]
    n_ref[...] = n_in[...]
    m_ref[...] = m_in[...]
    for src, dst in ((q_ref, qp), (k_ref, kp), (v_ref, vp)):
        dst[...] = jnp.zeros_like(dst)
        dst[0:DEC_SEQ, :] = src[...]
    gates = _mlstm_gates(gc_ref[0], gr_ref[0], bc_ref[...], br_ref[...], DEC_SEQ)
    om = om_ref[...]

    def h_out(hs, val):
        h_ref[:, hs] = val[0:DEC_SEQ, :]

    def om_of(hs):
        return jnp.concatenate([om[:, hs], jnp.zeros((SAMPLE_CHUNK - DEC_SEQ, M_HEAD_DIM), F32)], axis=0)

    _mlstm_heads(lambda hs: qp[:, hs].astype(BF16), lambda hs: (kp[:, hs] * K_SCALE).astype(BF16),
                 lambda hs: vp[:, hs].astype(BF16), om_of,
                 gates, gain_ref, c_ref, n_ref, m_ref, h_out)


def _mlstm_sample(qkv, gates, g_col, g_row, b_if, gain, layer, state_c, state_n, st_m, hm, c_out, n_out):
    assert DEC_SEQ == SUBLANES
    n_prompt = _rows()[0]
    blk0 = n_prompt // DEC_SEQ
    spec = lambda col: pl.BlockSpec((DEC_SEQ, M_WIDTH), lambda b: (blk0 + b, col))
    full = lambda shape: pl.BlockSpec(shape, lambda b: (0,) * len(shape))
    st_spec = lambda shape: pl.BlockSpec((1,) + shape, lambda b: (b,) + (0,) * len(shape))
    layer_spec = lambda shape: pl.BlockSpec((None, 1) + shape, lambda b: (layer, b) + (0,) * len(shape))
    st_specs = [layer_spec((M_HEADS, M_HEAD_DIM, M_HEAD_DIM)), layer_spec((M_HEADS, M_HEAD_DIM)),
                st_spec((M_HEADS, LANES))]
    carried = [] if c_out is None else [c_out, n_out]
    aliases = {12: 0}
    if carried:
        aliases.update({13: 1, 14: 2})
    return pl.pallas_call(
        _mlstm_sample_kernel,
        grid=(DEC_BATCH,),
        in_specs=[spec(0), spec(1), spec(2), spec(3),
                  st_spec((SAMPLE_CHUNK, 2 * M_HEADS)), st_spec((2 * M_HEADS, SAMPLE_CHUNK)),
                  full((1, 2 * M_HEADS)), full((2 * M_HEADS, 1)), full((1, M_WIDTH))] + st_specs
                 + [pl.BlockSpec(memory_space=pl.ANY)] * (1 + len(carried)),
        out_specs=[spec(0)] + st_specs,
        out_shape=[jax.ShapeDtypeStruct(hm.shape, F32),
                   jax.ShapeDtypeStruct(state_c.shape, F32), jax.ShapeDtypeStruct(state_n.shape, F32),
                   jax.ShapeDtypeStruct((DEC_BATCH, M_HEADS, LANES), F32)],
        scratch_shapes=[pltpu.VMEM((SAMPLE_CHUNK, M_WIDTH), F32)] * 3,
        input_output_aliases=aliases,
        compiler_params=_params("arbitrary"),
        name="mlstm_sample",
    )(qkv, qkv, qkv, gates, g_col, g_row, b_if.reshape(1, -1), b_if.reshape(-1, 1), gain.reshape(1, -1),
      state_c, state_n, jnp.broadcast_to(st_m[:, :, None], (DEC_BATCH, M_HEADS, LANES)), hm, *carried)


def _lambda(lam_ref, lam_init):
    p = lam_ref[...]
    s1 = jnp.sum(p[0:1, :] * p[1:2, :], axis=1, keepdims=True)
    s2 = jnp.sum(p[2:3, :] * p[3:4, :], axis=1, keepdims=True)
    return jnp.exp(s1) - jnp.exp(s2) + lam_init


def _diff_prompt_kernel(slope_ref, q_ref, k_ref, vt_ref, km_ref, vtm_ref, lam_ref, gain_ref, o_ref,
                        base_ref, m_ref, acc_ref, *, lam_init):
    b = pl.program_id(0)
    hg = pl.program_id(1)
    is_meta = pl.program_id(2) == 0
    t = pl.program_id(2) - 1
    tq = ROW_BLOCK
    heads = range(HEAD_GROUP)
    cols = [slice(i * LANES, (i + 1) * LANES) for i in heads]
    slopes = [slope_ref[hg * HEAD_GROUP + i] * LOG2E for i in heads]

    @pl.when(jnp.logical_not(is_meta & (b > 0)))
    def _():
        lo = lax.broadcasted_iota(jnp.int32, (tq, LANES), 1) < D_QK_DIM
        q2 = []
        for i in heads:
            q = q_ref[:, cols[i]].astype(BF16)
            zero = jnp.zeros_like(q)
            q2.append(jnp.concatenate([jnp.where(lo, q, zero), jnp.where(lo, zero, q)], axis=0))
            base_ref[i] = slopes[i] * lax.broadcasted_iota(jnp.int32, (tq, 2 * tq), 0).astype(F32)
        q_pos0 = jnp.where(is_meta, 0, N_META + t * tq)
        ones = jnp.ones((ONES_ROWS, tq), BF16)
        m_ref[...] = jnp.full_like(m_ref, NEG_INF)
        acc_ref[...] = jnp.zeros_like(acc_ref)

        def visible(n_keys, k_off):
            key = lax.broadcasted_iota(jnp.int32, (n_keys, 2 * tq), 0) + k_off
            qry = lax.broadcasted_iota(jnp.int32, (n_keys, 2 * tq), 1)
            return key <= jnp.where(qry >= tq, qry - tq, qry)

        def step(i, parts):
            cs = [slopes[i] * k_off.astype(F32) for _, _, k_off in parts]
            m_old = m_ref[i]
            m_new = m_old
            for (s, _, _), c in zip(parts, cs):
                m_new = jnp.maximum(m_new, jnp.max(s, axis=0, keepdims=True) + c)
            acc = jnp.exp2(m_old - m_new) * acc_ref[i]
            for (s, vt, _), c in zip(parts, cs):
                p = jnp.exp2(s - (m_new - c)).astype(BF16)
                acc = acc + _dot(jnp.concatenate([vt, ones[:, 0:vt.shape[1]]], axis=0), p)
            acc_ref[i] = acc
            m_ref[i] = m_new

        def scores(kb, i):
            ks = pl.ds(pl.multiple_of(kb * tq, tq), tq)
            return _dot_nt(k_ref[ks, cols[i]], q2[i])

        def full_blocks(kbs):
            for i in heads:
                step(i, [(scores(kb, i) + base_ref[i], vt_ref[kb, cols[i], :], N_META + kb * tq - q_pos0)
                         for kb in kbs])

        n_full = jnp.where(is_meta, 0, t)
        done = 0
        for width in KV_UNROLL:
            n_iter = (n_full - done) // width

            def body(j, carry, width=width, done=done):
                full_blocks([done + j * width + u for u in range(width)])
                return carry

            lax.fori_loop(0, n_iter, body, 0)
            done = done + n_iter * width

        td = jnp.maximum(t, 0)
        hide = jnp.where(is_meta, 2 * tq, 0)
        for i in heads:
            s_meta = _dot_nt(km_ref[:, cols[i]], q2[i]) + base_ref[i, 0:N_META, :]
            s_diag = scores(td, i) + base_ref[i]
            step(i, [(jnp.where(visible(N_META, -q_pos0), s_meta, NEG_INF), vtm_ref[0, cols[i], 0:N_META], -q_pos0),
                     (jnp.where(visible(tq, hide), s_diag, NEG_INF), vt_ref[td, cols[i], :], jnp.int32(0))])

        lam = _lambda(lam_ref, lam_init)
        for i in heads:
            acc = acc_ref[i]
            o_t = acc[0:D_V_DIM, :] * (1.0 / acc[D_V_DIM:D_V_DIM + 1, :])
            o = (o_t[:, 0:tq] - lam * o_t[:, tq:2 * tq]).T
            o_ref[:, cols[i]] = _rms(o, gain_ref[:, cols[i]]) * (1.0 - lam_init)


def _diff_prompt(qn, kb, vt, lam_pack, gain, slopes, lam_init):
    r = qn.shape[0]
    meta_row = _rows()[2]
    nq = SEQ // ROW_BLOCK
    meta_blk = meta_row // ROW_BLOCK
    gw = HEAD_GROUP * LANES
    assert meta_row % N_META == 0 and D_V_DIM == LANES and D_HEADS % HEAD_GROUP == 0

    def qmap(b, h, qi, *_):
        return (jnp.where(qi == 0, jnp.where(b == 0, meta_blk, b * nq), b * nq + qi - 1), h)

    grid_spec = pltpu.PrefetchScalarGridSpec(
        num_scalar_prefetch=1,
        grid=(BATCH, D_HEADS // HEAD_GROUP, nq + 1),
        in_specs=[pl.BlockSpec((ROW_BLOCK, gw), qmap),
                  pl.BlockSpec((SEQ, gw), lambda b, h, qi, *_: (b, h)),
                  pl.BlockSpec((nq, gw, ROW_BLOCK), lambda b, h, qi, *_: (b, h, 0)),
                  pl.BlockSpec((N_META, gw), lambda b, h, qi, *_: (meta_row // N_META, h)),
                  pl.BlockSpec((1, gw, ROW_BLOCK), lambda b, h, qi, *_: (meta_blk, h, 0)),
                  pl.BlockSpec((SUBLANES, LANES), lambda b, h, qi, *_: (0, 0)),
                  pl.BlockSpec((1, gw), lambda b, h, qi, *_: (0, h))],
        out_specs=pl.BlockSpec((ROW_BLOCK, gw), qmap),
        scratch_shapes=[pltpu.VMEM((HEAD_GROUP, ROW_BLOCK, 2 * ROW_BLOCK), F32),
                        pltpu.VMEM((HEAD_GROUP, 1, 2 * ROW_BLOCK), F32),
                        pltpu.VMEM((HEAD_GROUP, D_V_DIM + ONES_ROWS, 2 * ROW_BLOCK), F32)])
    return pl.pallas_call(
        functools.partial(_diff_prompt_kernel, lam_init=lam_init),
        grid_spec=grid_spec,
        out_shape=jax.ShapeDtypeStruct((r, D_V_WIDTH), F32),
        compiler_params=_params("arbitrary", "arbitrary", "arbitrary"),
        name="diff_prompt",
    )(slopes, qn, kb, vt, kb, vt, lam_pack, gain.reshape(1, -1))


def _diff_sample_kernel(pt_ref, *refs, lam_init):
    del pt_ref
    pp = PAGES_PER_STEP
    k_pages, v_pages = refs[0:pp], refs[pp:2 * pp]
    (q_ref, kn_ref, vn_ref, lam_ref, gain_ref, hd_in, o_ref,
     qb_ref, kp_ref, vp_ref, m_ref, acc_ref) = refs[2 * pp:]
    del hd_in
    g = pl.program_id(1)
    qrows = 2 * DEC_SEQ
    half = D_HEADS * DEC_SEQ
    slopes = [2.0 ** (-8.0 * (h + 1) / D_HEADS) * LOG2E for h in range(D_HEADS)]

    def step(h, s, v):
        m_old = m_ref[h]
        m_new = jnp.maximum(m_old, jnp.max(s, axis=1, keepdims=True))
        p = jnp.exp2(s - m_new).astype(BF16)
        v_ones = jnp.concatenate([v, jnp.ones_like(v)], axis=1)
        acc_ref[h] = jnp.exp2(m_old - m_new) * acc_ref[h] + _dot(p, v_ones)
        m_ref[h] = m_new

    @pl.when(g == 0)
    def _():
        lo = lax.broadcasted_iota(jnp.int32, (DEC_SEQ, LANES), 1) < D_QK_DIM
        for h in range(D_HEADS):
            x = q_ref[:, h * LANES:(h + 1) * LANES]
            qb_ref[h] = jnp.concatenate([jnp.where(lo, x, 0.0), jnp.where(lo, 0.0, x)], axis=0).astype(BF16)
        m_ref[...] = jnp.full_like(m_ref, NEG_INF)
        acc_ref[...] = jnp.zeros_like(acc_ref)

    k_pos = (lax.broadcasted_iota(jnp.int32, (1, pp * PAGE_SIZE), 1) + (g * (pp * PAGE_SIZE) - PAST_LEN)).astype(F32)
    for h in range(D_HEADS):
        rows_h = pl.ds(h, PAGE_SIZE, stride=D_HEADS)
        kh = jnp.concatenate([k_pages[p][rows_h, :] for p in range(pp)], axis=0).astype(BF16)
        vh = jnp.concatenate([v_pages[p][rows_h, :] for p in range(pp)], axis=0).astype(BF16)
        step(h, _dot_nt(qb_ref[h], kh) + slopes[h] * k_pos, vh)

    @pl.when(g == pl.num_programs(1) - 1)
    def _():
        for src, dst in ((kn_ref, kp_ref), (vn_ref, vp_ref)):
            dst[...] = jnp.zeros_like(dst)
            dst[0:half, :] = src[...]
        kn = kp_ref[...].astype(BF16)
        vn = vp_ref[...].astype(BF16)
        c = lax.broadcasted_iota(jnp.int32, (qrows, LANES), 1)
        r_tok = lax.broadcasted_iota(jnp.int32, (qrows, LANES), 0) % DEC_SEQ
        c_tok = c // D_HEADS
        lam = _lambda(lam_ref, lam_init)
        for h in range(D_HEADS):
            ok = (c % D_HEADS == h) & (c_tok <= r_tok)
            s = _dot_nt(qb_ref[h], kn) + slopes[h] * c_tok.astype(F32)
            step(h, jnp.where(ok, s, NEG_INF), vn)
            acc = acc_ref[h]
            o = acc[:, 0:LANES] * (1.0 / acc[:, LANES:2 * LANES])
            oh = o[0:DEC_SEQ, :] - lam * o[DEC_SEQ:qrows, :]
            cs = slice(h * LANES, (h + 1) * LANES)
            o_ref[:, cs] = _rms(oh, gain_ref[:, cs]) * (1.0 - lam_init)


def _diff_sample(layer, cache_k, cache_v, page_table, qn, kf, vf, lam_pack, gain, hd, lam_init):
    assert DEC_SEQ == SUBLANES and D_HEADS == SUBLANES and D_V_DIM == LANES
    assert D_HEADS * DEC_SEQ <= LANES and PAST_LEN % (PAGE_SIZE * PAGES_PER_STEP) == 0
    n_prompt = _rows()[0]
    blk0 = n_prompt // DEC_SEQ
    pp = PAGES_PER_STEP
    width = D_HEADS * LANES
    rows = 2 * D_HEADS * DEC_SEQ

    def as_rows(cache):
        return cache.reshape(cache.shape[0], cache.shape[1], PAGE_SIZE * D_HEADS, LANES)

    def page_spec(p):
        return pl.BlockSpec((None, None, PAGE_SIZE * D_HEADS, LANES),
                            lambda b, g, pt: (layer, pt[b, g * pp + p], 0, 0))

    row_spec = pl.BlockSpec((DEC_SEQ, width), lambda b, g, pt: (blk0 + b, 0))
    new_spec = pl.BlockSpec((DEC_SEQ * D_HEADS, LANES), lambda b, g, pt: (b, 0))
    grid_spec = pltpu.PrefetchScalarGridSpec(
        num_scalar_prefetch=1,
        grid=(DEC_BATCH, PAST_LEN // (PAGE_SIZE * pp)),
        in_specs=[page_spec(p) for p in range(pp)] * 2
                 + [row_spec, new_spec, new_spec,
                    pl.BlockSpec((SUBLANES, LANES), lambda b, g, pt: (0, 0)),
                    pl.BlockSpec((1, width), lambda b, g, pt: (0, 0)),
                    pl.BlockSpec(memory_space=pl.ANY)],
        out_specs=row_spec,
        scratch_shapes=[pltpu.VMEM((D_HEADS, 2 * DEC_SEQ, LANES), BF16),
                        pltpu.VMEM((LANES, LANES), F32), pltpu.VMEM((LANES, LANES), F32),
                        pltpu.VMEM((D_HEADS, 2 * DEC_SEQ, 1), F32), pltpu.VMEM((D_HEADS, 2 * DEC_SEQ, 2 * LANES), F32)])
    n_in = 2 * pp + 6
    return pl.pallas_call(
        functools.partial(_diff_sample_kernel, lam_init=lam_init),
        grid_spec=grid_spec,
        out_shape=jax.ShapeDtypeStruct(hd.shape, F32),
        input_output_aliases={n_in: 0},
        compiler_params=_params("arbitrary", "arbitrary"),
        name="diff_sample",
    )(page_table, *([as_rows(cache_k)] * pp), *([as_rows(cache_v)] * pp), qn, kf, vf, lam_pack,
      gain.reshape(1, -1), hd)


def kernel(x_prompt, x_sample, cache_k, cache_v, state_C, state_n, state_m, page_table, meta_tokens, norm_mix, w_in, b_if, q_gain, k_gain, lambda_q1, lambda_k1, lambda_q2, lambda_k2, mlstm_norm, diff_norm, w_pm, w_pd, w_out, norm_ffn, w_gu, w_down):
    assert M_HEAD_DIM ** -0.5 == 2.0 ** round(math.log2(M_HEAD_DIM ** -0.5))
    n_prompt, n_sample, meta_row, r = _rows()
    x_head = x_prompt.reshape(n_prompt, D_MODEL)
    x_tail = jnp.concatenate([x_sample.reshape(n_sample, D_MODEL), meta_tokens.astype(F32),
                              jnp.zeros((r - meta_row - N_META, D_MODEL), F32)], axis=0)
    x = None
    slopes = 2.0 ** (-8.0 * jnp.arange(1, D_HEADS + 1, dtype=F32) / D_HEADS)
    w_in_t = jnp.swapaxes(w_in, 1, 2)
    g0 = 4 * M_WIDTH
    g1 = g0 + 2 * M_HEADS

    k_prompt = v_prompt = c_sample = n_sample_st = None
    meta_kv = []
    outs = [[] for _ in range(6)]
    for l in range(DEPTH):
        lam_init = 0.8 - 0.6 * math.exp(-0.3 * l)
        if l == 0:
            xn = _rmsnorm_split(x_head, x_tail, norm_mix[l], "norm_mix_first")
        else:
            xn = _rmsnorm(x, norm_mix[l], "norm_mix")
        proj_m = _matmul_nt(xn, w_in_t, l, 0, g0, "proj_mlstm")
        graw = _matmul_nt(xn, w_in_t, l, g0, 2 * M_HEADS, "proj_gates")
        g2 = g1 + 2 * D_QK_WIDTH + D_V_WIDTH
        proj_d = _matmul_nt(xn, w_in_t, l, g1, g2 - g1, "proj_diff")
        proj_g = _matmul_nt(xn, w_in_t, l, g2, w_in_t.shape[1] - g2, "proj_merge_gates", BF16)

        hm, c_p, n_p, m_p = _mlstm_prompt(proj_m, proj_m, graw, graw.T, b_if[l], mlstm_norm[l])
        gs = graw[n_prompt:n_prompt + n_sample].reshape(DEC_BATCH, DEC_SEQ, 2 * M_HEADS)
        gs = jnp.pad(gs, ((0, 0), (0, SAMPLE_CHUNK - DEC_SEQ), (0, 0)))
        hm, c_sample, n_sample_st, m_s = _mlstm_sample(
            proj_m, proj_m, gs, gs.transpose(0, 2, 1), b_if[l], mlstm_norm[l], l, state_C.astype(F32),
            state_n.astype(F32), state_m[l].astype(F32), hm, c_sample, n_sample_st)

        qn, kb, vt, k_prompt, v_prompt, kf, vf = _qk_norm(proj_d, q_gain[l], k_gain[l], l, k_prompt, v_prompt)
        lam_pack = jnp.pad(jnp.stack([lambda_q1[l], lambda_k1[l], lambda_q2[l], lambda_k2[l]]).astype(F32),
                           ((0, SUBLANES - 4), (0, LANES - D_QK_DIM)))
        hd = _diff_prompt(qn, kb, vt, lam_pack, diff_norm[l], slopes, lam_init)
        hd = _diff_sample(l, cache_k, cache_v, page_table, qn, kf, vf, lam_pack, diff_norm[l], hd, lam_init)

        z = _merge(hm, hd, w_pm, w_pd, l, proj_g, 0)
        if l == 0:
            x = _matmul_res_joined(z, w_out, l, x_head, x_tail, "out_proj_first")
        else:
            x = _matmul_res(z, w_out, l, x, "out_proj", MATMUL_ROWS, 1024)
        act = _swiglu(_rmsnorm(x, norm_ffn[l], "norm_ffn"), w_gu, l)
        if l < DEPTH - 1:
            x = _matmul_res(act, w_down, l, x, "ffn_down", MATMUL_ROWS // 2, 512)
        else:
            y_head, y_tail = _matmul_res_split(act, w_down, l, x, "ffn_down_last", n_prompt)

        head_shape = (D_HEADS, D_V_DIM)
        s_rows, m_rows = n_sample * D_HEADS, N_META * D_HEADS
        meta_kv.append((kf[s_rows:s_rows + m_rows], vf[s_rows:s_rows + m_rows]))
        vals = (c_p, n_p, m_p[:, :, 0],
                kf[:s_rows].reshape(DEC_BATCH, DEC_SEQ, *head_shape),
                vf[:s_rows].reshape(DEC_BATCH, DEC_SEQ, *head_shape), m_s[:, :, 0])
        for o, v in zip(outs, vals):
            o.append(v)

    for l, (km, vm) in enumerate(meta_kv):
        k_prompt = k_prompt.at[l, :, :km.shape[0]].set(jnp.broadcast_to(km, (BATCH,) + km.shape))
        v_prompt = v_prompt.at[l, :, :vm.shape[0]].set(jnp.broadcast_to(vm, (BATCH,) + vm.shape))
    kv_shape = (DEPTH, BATCH, N_META + SEQ, D_HEADS, D_V_DIM)
    y_prompt = y_head.reshape(BATCH, SEQ, D_MODEL)
    y_sample = y_tail[:n_sample].reshape(DEC_BATCH, DEC_SEQ, D_MODEL)
    c_p, n_p, m_p, k_s, v_s, m_s = (jnp.stack(o) for o in outs)
    return (y_prompt, y_sample, k_prompt.reshape(kv_shape), v_prompt.reshape(kv_shape), c_p, n_p, m_p, k_s, v_s,
            c_sample, n_sample_st, m_s)
```
